```python
import math
import jax
import jax.numpy as jnp
from jax import lax
import numpy as np

D_MODEL = 2048
BATCH = 4
SEQ = 4096
DEPTH = 4

CTX_LEN = 256
GRID_W = 64
N_MIXERS = 2

HEAD_DIM = 128
N_Q_HEADS = D_MODEL // HEAD_DIM
N_KV_HEADS = max(N_Q_HEADS // 4, 1)
Q_PER_KV = N_Q_HEADS // N_KV_HEADS
QKV_DIM = (N_Q_HEADS + 2 * N_KV_HEADS) * HEAD_DIM
WINDOW = 128
BLOCK = 128
AXIS_ROT = HEAD_DIM // 2
ROPE_BASE = 10000.0

SSM_GROUP = 16
SSM_GROUPS = D_MODEL // SSM_GROUP
SSM_STATE = 64
DT_MIN = 1e-3
DT_MAX = 1e-1

N_EXPERTS = 32
TOP_K = 4
D_EXPERT = D_MODEL // 2
SWIGLU_LIMIT = 7.0
SWIGLU_ALPHA = 1.702

EPS = 1e-6
NEG_INF = -1e30

kernel_name = 'hybrid_swa_s5_moe_diffusion_trunk'


def rms_norm(x, w):
    x32 = x.astype(jnp.float32)
    y = x32 * lax.rsqrt(jnp.mean(x32 * x32, axis=-1, keepdims=True) + EPS)
    return (y * w.astype(jnp.float32)).astype(x.dtype)


def head_rms(x, g):
    x32 = x.astype(jnp.float32)
    y = x32 * lax.rsqrt(jnp.mean(x32 * x32, axis=-1, keepdims=True) + EPS)
    return (y * g.astype(jnp.float32)).astype(x.dtype)


def modulate(h, shift, scale):
    return h * (1 + scale) + shift


def axial_rope_tables(n_rows):
    rows = jnp.repeat(jnp.arange(n_rows, dtype=jnp.float32), GRID_W)
    cols = jnp.tile(jnp.arange(GRID_W, dtype=jnp.float32), n_rows)
    inv_freq = ROPE_BASE ** (-jnp.arange(0, AXIS_ROT, 2, dtype=jnp.float32) / AXIS_ROT)
    ang = jnp.concatenate([rows[:, None] * inv_freq, cols[:, None] * inv_freq], axis=-1)
    return jnp.cos(ang), jnp.sin(ang)


def apply_axial_rope(x, cos, sin):
    b, l, h, d = x.shape
    half = AXIS_ROT // 2
    xr = x.astype(jnp.float32).reshape(b, l, h, 2, 2, half)
    x_a, x_b = xr[..., 0, :], xr[..., 1, :]
    cs = cos.reshape(l, 1, 2, half)
    sn = sin.reshape(l, 1, 2, half)
    out = jnp.stack([x_a * cs - x_b * sn, x_b * cs + x_a * sn], axis=-2)
    return out.reshape(b, l, h, d).astype(x.dtype)


def windowed_gqa_attention(u_lat, u_ctx, w_qkv, w_o, q_gain, k_gain, sinks, cos, sin, ctx_out):
    b, l, d = u_lat.shape
    lc = u_ctx.shape[1]
    nb = l // BLOCK
    scale = HEAD_DIM ** -0.5
    nq = N_Q_HEADS * HEAD_DIM
    nk = N_KV_HEADS * HEAD_DIM

    def project(u):
        n = u.shape[1]
        qkv = u @ w_qkv
        q = qkv[..., :nq].reshape(b, n, N_Q_HEADS, HEAD_DIM)
        k = qkv[..., nq:nq + nk].reshape(b, n, N_KV_HEADS, HEAD_DIM)
        v = qkv[..., nq + nk:].reshape(b, n, N_KV_HEADS, HEAD_DIM)
        return head_rms(q, q_gain), head_rms(k, k_gain), v

    q_l, k_l, v_l = project(u_lat)
    q_c, k_c, v_c = project(u_ctx)
    q_l = apply_axial_rope(q_l, cos, sin)
    k_l = apply_axial_rope(k_l, cos, sin)
    sink_logit = sinks.astype(jnp.float32).reshape(N_KV_HEADS, Q_PER_KV)

    qb = q_l.reshape(b, nb, BLOCK, N_KV_HEADS, Q_PER_KV, HEAD_DIM)
    pad = ((0, 0), (BLOCK, BLOCK), (0, 0), (0, 0))
    kp = jnp.pad(k_l, pad).reshape(b, nb + 2, BLOCK, N_KV_HEADS, HEAD_DIM)
    vp = jnp.pad(v_l, pad).reshape(b, nb + 2, BLOCK, N_KV_HEADS, HEAD_DIM)
    k_band = jnp.concatenate([kp[:, :-2], kp[:, 1:-1], kp[:, 2:]], axis=2)
    v_band = jnp.concatenate([vp[:, :-2], vp[:, 1:-1], vp[:, 2:]], axis=2)
    s_loc = jnp.einsum('bnqkgd,bnskd->bnkgqs', qb, k_band).astype(jnp.float32) * scale
    qpos = jnp.arange(nb)[:, None, None] * BLOCK + jnp.arange(BLOCK)[None, :, None]
    kpos = (jnp.arange(nb)[:, None, None] - 1) * BLOCK + jnp.arange(3 * BLOCK)[None, None, :]
    valid = (jnp.abs(kpos - qpos) <= WINDOW) & (kpos >= 0) & (kpos < l)
    s_loc = jnp.where(valid[None, :, None, None], s_loc, NEG_INF)
    s_ctx = jnp.einsum('bnqkgd,bskd->bnkgqs', qb, k_c).astype(jnp.float32) * scale
    sink_col = jnp.broadcast_to(sink_logit[None, None, :, :, None, None], s_loc.shape[:-1] + (1,))
    p = jax.nn.softmax(jnp.concatenate([s_loc, s_ctx, sink_col], axis=-1), axis=-1).astype(v_l.dtype)
    o = (jnp.einsum('bnkgqs,bnskd->bnqkgd', p[..., :3 * BLOCK], v_band)
         + jnp.einsum('bnkgqs,bskd->bnqkgd', p[..., 3 * BLOCK:3 * BLOCK + lc], v_c))
    y_lat = o.reshape(b, l, nq) @ w_o

    y_ctx = None
    if ctx_out:
        qc = q_c.reshape(b, lc, N_KV_HEADS, Q_PER_KV, HEAD_DIM)
        s_cc = jnp.einsum('bqkgd,bskd->bkgqs', qc, k_c).astype(jnp.float32) * scale
        sink_cc = jnp.broadcast_to(sink_logit[None, :, :, None, None], s_cc.shape[:-1] + (1,))
        pc = jax.nn.softmax(jnp.concatenate([s_cc, sink_cc], axis=-1), axis=-1).astype(v_c.dtype)
        oc = jnp.einsum('bkgqs,bskd->bqkgd', pc[..., :lc], v_c)
        y_ctx = oc.reshape(b, lc, nq) @ w_o
    return y_lat, y_ctx


def ssm_scan(bu, lam_dt, reverse, h0=None):
    n = bu.shape[1]
    steps = jnp.ones((1, n, 1, 1), jnp.float32)

    def combine(left, right):
        k1, h1 = left
        k2, h2 = right
        return k1 + k2, jnp.exp(lam_dt * k2) * h1 + h2

    _, h = lax.associative_scan(combine, (steps, bu), reverse=reverse, axis=1)
    if h0 is not None:
        t = jnp.arange(n, dtype=jnp.float32)
        count = (n - t) if reverse else (t + 1.0)
        h = h + jnp.exp(lam_dt * count[:, None, None]) * h0[:, None]
    return h


def s5_mixer(u_lat, u_ctx, a_re, a_im, log_dt, b_re, b_im, c_re, c_im, d_skip, w_glu, b_glu, ctx_out):
    b, l, d = u_lat.shape
    lc = u_ctx.shape[1]
    lam = lax.complex(jnp.minimum(a_re.astype(jnp.float32), -1e-4), a_im.astype(jnp.float32))
    lam_dt = lam * jnp.exp(log_dt.astype(jnp.float32))[..., None]
    b_bar = ((jnp.exp(lam_dt) - 1.0) / lam)[..., None] * lax.complex(
        b_re.astype(jnp.float32), b_im.astype(jnp.float32))
    c_mat = lax.complex(c_re.astype(jnp.float32), c_im.astype(jnp.float32))
    ug_l = u_lat.astype(jnp.float32).reshape(b, l, SSM_GROUPS, SSM_GROUP).astype(jnp.complex64)
    ug_c = u_ctx.astype(jnp.float32).reshape(b, lc, SSM_GROUPS, SSM_GROUP).astype(jnp.complex64)
    dsk = d_skip.astype(jnp.float32)
    y_l = dsk * u_lat.astype(jnp.float32)
    y_c = dsk * u_ctx.astype(jnp.float32)
    for direction, reverse in ((0, False), (1, True)):
        bu_c = jnp.einsum('bsgp,gnp->bsgn', ug_c, b_bar[direction])
        h_c = ssm_scan(bu_c, lam_dt[direction], reverse)
        h0 = h_c[:, 0] if reverse else h_c[:, -1]
        bu_l = jnp.einsum('bsgp,gnp->bsgn', ug_l, b_bar[direction])
        h_l = ssm_scan(bu_l, lam_dt[direction], reverse, h0)
        y_l = y_l + jnp.einsum('bsgn,gpn->bsgp', h_l, c_mat[direction]).real.reshape(b, l, d)
        if ctx_out:
            y_c = y_c + jnp.einsum('bsgn,gpn->bsgp', h_c, c_mat[direction]).real.reshape(b, lc, d)

    def glu(y):
        g = jax.nn.gelu(y).astype(u_lat.dtype)
        z = g @ w_glu + b_glu
        return z[..., :d] * jax.nn.sigmoid(z[..., d:])

    return glu(y_l), (glu(y_c) if ctx_out else None)


def moe_ffn(x2d, w_router, b_router, w_gu, b_gu, w_down, b_down):
    logits = x2d.astype(jnp.float32) @ w_router.astype(jnp.float32) + b_router.astype(jnp.float32)
    top_val, top_idx = lax.top_k(logits, TOP_K)
    top_w = jax.nn.softmax(top_val, axis=-1)
    gates = jnp.sum(jax.nn.one_hot(top_idx, N_EXPERTS, dtype=jnp.float32) * top_w[..., None], axis=1)

    def expert_step(acc, params):
        wgu, bgu, wdn, bdn, g = params
        h = x2d @ wgu + bgu
        gate = jnp.minimum(h[..., 0::2], SWIGLU_LIMIT)
        up = jnp.clip(h[..., 1::2], -SWIGLU_LIMIT, SWIGLU_LIMIT)
        act = (up + 1) * (gate * jax.nn.sigmoid(SWIGLU_ALPHA * gate))
        y = act @ wdn + bdn
        return acc + g[:, None] * y.astype(jnp.float32), None

    acc0 = jnp.zeros(x2d.shape, jnp.float32)
    acc, _ = lax.scan(expert_step, acc0, (w_gu, b_gu, w_down, b_down, gates.T))
    return acc.astype(x2d.dtype)


def setup_inputs(seed: int = 0) -> dict:
    key = jax.random.key(seed)
    ks = jax.random.split(key, 32)
    f32 = jnp.float32
    n_attn = (DEPTH + 1) // 2
    n_ssm = DEPTH // 2
    D = D_MODEL

    def nrm(k, shape, s):
        return s * jax.random.normal(k, shape, f32)

    a_im_base = jnp.pi * jnp.arange(SSM_STATE, dtype=f32)
    return {
        'x': nrm(ks[0], (BATCH, SEQ, D), 1.0),
        'c': nrm(ks[1], (BATCH, D), 1.0),
        'ctx': nrm(ks[2], (BATCH, CTX_LEN, D), 1.0),
        'c_ctx': nrm(ks[3], (D,), 1.0),
        'w_mod': nrm(ks[4], (DEPTH, D, 6 * D), 0.5 * D ** -0.5),
        'b_mod': nrm(ks[5], (DEPTH, 6 * D), 0.02),
        'norm_mix': 1.0 + nrm(ks[6], (DEPTH, D), 0.05),
        'norm_ffn': 1.0 + nrm(ks[7], (DEPTH, D), 0.05),
        'w_router': nrm(ks[8], (DEPTH, D, N_EXPERTS), D ** -0.5),
        'b_router': nrm(ks[9], (DEPTH, N_EXPERTS), 0.01),
        'w_gate_up': nrm(ks[10], (DEPTH, N_EXPERTS, D, 2 * D_EXPERT), D ** -0.5),
        'b_gate_up': nrm(ks[11], (DEPTH, N_EXPERTS, 2 * D_EXPERT), 0.02),
        'w_down': nrm(ks[12], (DEPTH, N_EXPERTS, D_EXPERT, D), D_EXPERT ** -0.5),
        'b_down': nrm(ks[13], (DEPTH, N_EXPERTS, D), 0.02),
        'attn_w_qkv': nrm(ks[14], (n_attn, D, QKV_DIM), D ** -0.5),
        'attn_w_o': nrm(ks[15], (n_attn, N_Q_HEADS * HEAD_DIM, D), (N_Q_HEADS * HEAD_DIM) ** -0.5),
        'attn_q_gain': 1.0 + nrm(ks[16], (n_attn, HEAD_DIM), 0.05),
        'attn_k_gain': 1.0 + nrm(ks[17], (n_attn, HEAD_DIM), 0.05),
        'attn_sinks': nrm(ks[18], (n_attn, N_Q_HEADS), 0.5),
        'ssm_a_re': -0.5 + nrm(ks[19], (n_ssm, 2, SSM_GROUPS, SSM_STATE), 0.01),
        'ssm_a_im': a_im_base + nrm(ks[20], (n_ssm, 2, SSM_GROUPS, SSM_STATE), 0.01),
        'ssm_log_dt': jax.random.uniform(ks[21], (n_ssm, 2, SSM_GROUPS), f32,
                                         math.log(DT_MIN), math.log(DT_MAX)),
        'ssm_b_re': nrm(ks[22], (n_ssm, 2, SSM_GROUPS, SSM_STATE, SSM_GROUP), (2 * SSM_GROUP) ** -0.5),
        'ssm_b_im': nrm(ks[23], (n_ssm, 2, SSM_GROUPS, SSM_STATE, SSM_GROUP), (2 * SSM_GROUP) ** -0.5),
        'ssm_c_re': nrm(ks[24], (n_ssm, 2, SSM_GROUPS, SSM_GROUP, SSM_STATE), SSM_STATE ** -0.5),
        'ssm_c_im': nrm(ks[25], (n_ssm, 2, SSM_GROUPS, SSM_GROUP, SSM_STATE), SSM_STATE ** -0.5),
        'ssm_d': nrm(ks[26], (n_ssm, D), 1.0),
        'ssm_w_glu': nrm(ks[27], (n_ssm, D, 2 * D), D ** -0.5),
        'ssm_b_glu': nrm(ks[28], (n_ssm, 2 * D), 0.02),
    }


def reference(x, c, ctx, c_ctx, w_mod, b_mod, norm_mix, norm_ffn, w_router, b_router,
              w_gate_up, b_gate_up, w_down, b_down, attn_w_qkv, attn_w_o, attn_q_gain,
              attn_k_gain, attn_sinks, ssm_a_re, ssm_a_im, ssm_log_dt, ssm_b_re, ssm_b_im,
              ssm_c_re, ssm_c_im, ssm_d, ssm_w_glu, ssm_b_glu):
    b, seq, d = x.shape
    lc = ctx.shape[1]
    n_rows = seq // GRID_W
    cos, sin = axial_rope_tables(n_rows)
    silu_c = jax.nn.silu(c)
    silu_cc = jax.nn.silu(c_ctx)
    for i in range(DEPTH):
        last = i == DEPTH - 1
        j = i // N_MIXERS
        mod_l = silu_c @ w_mod[i] + b_mod[i]
        mod_c = silu_cc @ w_mod[i] + b_mod[i]
        sh_m, sc_m, g_m, sh_f, sc_f, g_f = jnp.split(mod_l[:, None, :], 6, axis=-1)
        csh_m, csc_m, cg_m, csh_f, csc_f, cg_f = jnp.split(mod_c, 6)
        u_l = modulate(rms_norm(x, norm_mix[i]), sh_m, sc_m)
        u_c = modulate(rms_norm(ctx, norm_mix[i]), csh_m, csc_m)
        if i % N_MIXERS == 0:
            o_l, o_c = windowed_gqa_attention(u_l, u_c, attn_w_qkv[j], attn_w_o[j], attn_q_gain[j],
                                              attn_k_gain[j], attn_sinks[j], cos, sin, not last)
        else:
            o_l, o_c = s5_mixer(u_l, u_c, ssm_a_re[j], ssm_a_im[j], ssm_log_dt[j], ssm_b_re[j],
                                ssm_b_im[j], ssm_c_re[j], ssm_c_im[j], ssm_d[j], ssm_w_glu[j],
                                ssm_b_glu[j], not last)
        x = x + g_m * o_l
        h_l = modulate(rms_norm(x, norm_ffn[i]), sh_f, sc_f).reshape(b * seq, d)
        moe_params = (w_router[i], b_router[i], w_gate_up[i], b_gate_up[i], w_down[i], b_down[i])
        if last:
            f_l = moe_ffn(h_l, *moe_params)
        else:
            ctx = ctx + cg_m * o_c
            h_c = modulate(rms_norm(ctx, norm_ffn[i]), csh_f, csc_f).reshape(b * lc, d)
            f = moe_ffn(jnp.concatenate([h_l, h_c], axis=0), *moe_params)
            f_l = f[:b * seq]
            ctx = ctx + cg_f * f[b * seq:].reshape(b, lc, d)
        x = x + g_f * f_l.reshape(b, seq, d)
    return x
```

```python
import functools
import math

import jax
import jax.numpy as jnp
from jax import lax
from jax.experimental import pallas as pl
from jax.experimental.pallas import tpu as pltpu

F32 = jnp.float32
BF16 = jnp.bfloat16

HEAD_DIM = 128
ATT_BLOCK = 128
GRID_W = 64
ROPE_BASE = 10000.0
SSM_GROUP = 16
SSM_STATE = 64
SSM_CHUNK = 16
TOP_K = 4
SWIGLU_LIMIT = 7.0
SWIGLU_ALPHA = 1.702
EPS = 1e-6
NEG_INF = -1e30
DT_FLOOR_RE = -1e-4

LANES = 128
VMEM_LIMIT = 56 * 1024 * 1024

ROW_TILE = 256
MM_TILE_M = 512
MM_TILE_N = 512
MOE_TILE = 512
GATHER_ROWS = 512
COMBINE_ROWS = 128
SSM_GROUP_BLOCK = 8


def _cparams(sem):
    return pltpu.CompilerParams(dimension_semantics=sem, vmem_limit_bytes=VMEM_LIMIT)


def _pick(n, pref, mult=8):
    t = min(n, pref)
    while n % t or t % mult:
        t -= 1
    return t


def _mod_kernel(c_ref, w_ref, b_ref, o_ref):
    c = c_ref[...]
    s = c * jax.nn.sigmoid(c)
    o_ref[0] = jnp.dot(s.astype(BF16), w_ref[0].astype(BF16), preferred_element_type=F32) + b_ref[0]


def _mod_all(cvec, w_mod, b_mod):
    depth, d, n = w_mod.shape
    tn = _pick(n, 1024, LANES)
    return pl.pallas_call(
        _mod_kernel,
        out_shape=jax.ShapeDtypeStruct((depth, cvec.shape[0], n), F32),
        grid=(depth, n // tn),
        in_specs=[pl.BlockSpec(cvec.shape, lambda l, j: (0, 0)),
                  pl.BlockSpec((1, d, tn), lambda l, j: (l, 0, j)),
                  pl.BlockSpec((1, 1, tn), lambda l, j: (l, 0, j))],
        out_specs=pl.BlockSpec((1, cvec.shape[0], tn), lambda l, j: (l, 0, j)),
        compiler_params=_cparams(("arbitrary", "arbitrary")),
        name="adaln_rows",
    )(cvec, w_mod, b_mod.reshape(depth, 1, n))


def _normed(x, w, sh, sc):
    r = lax.rsqrt(jnp.mean(x * x, axis=-1, keepdims=True) + EPS)
    return (x * r * w) * (1.0 + sc) + sh


def _norm_mod_kernel(x_ref, w_ref, sh_ref, sc_ref, o_ref):
    o_ref[0] = _normed(x_ref[0], w_ref[...], sh_ref[0], sc_ref[0]).astype(o_ref.dtype)


def _mod_spec(d, which, ctx_tiles):
    return pl.BlockSpec((1, 1, d), lambda b, i: (b * 2 + jnp.where(i >= ctx_tiles, 1, 0), 0, which))


def _norm_mod(x, w, modt, which_shift, ctx_len):
    b, p, d = x.shape
    tm = _pick(math.gcd(p, ctx_len), ROW_TILE)
    ct = ctx_len // tm
    return pl.pallas_call(
        _norm_mod_kernel,
        out_shape=jax.ShapeDtypeStruct((b, p, d), BF16),
        grid=(b, p // tm),
        in_specs=[pl.BlockSpec((1, tm, d), lambda bb, i: (bb, i, 0)),
                  pl.BlockSpec((1, d), lambda bb, i: (0, 0)),
                  _mod_spec(d, which_shift, ct), _mod_spec(d, which_shift + 1, ct)],
        out_specs=pl.BlockSpec((1, tm, d), lambda bb, i: (bb, i, 0)),
        compiler_params=_cparams(("arbitrary", "arbitrary")),
        name="norm_mod",
    )(x, w.reshape(1, d), modt, modt)


def _pack_halves(h):
    half = h.shape[-1] // 2
    bits = lax.bitcast_convert_type(h.astype(BF16).astype(F32), jnp.uint32)
    return (bits[:, :half] >> 16) | (bits[:, half:] & jnp.uint32(0xFFFF0000))


def _unpack_halves(xp):
    lo = lax.bitcast_convert_type(xp << 16, F32)
    hi = lax.bitcast_convert_type(xp & jnp.uint32(0xFFFF0000), F32)
    return lo, hi


def _norm_router_kernel(x_ref, w_ref, sh_ref, sc_ref, wr_ref, br_ref, hp_ref, idx_ref, gw_ref):
    h = _normed(x_ref[0], w_ref[...], sh_ref[0], sc_ref[0])
    hp_ref[0] = _pack_halves(h)
    logits = jnp.dot(h, wr_ref[...], precision=lax.Precision.HIGHEST,
                     preferred_element_type=F32) + br_ref[...]
    n_exp = logits.shape[-1]
    lane = lax.broadcasted_iota(jnp.int32, logits.shape, 1)
    out_lane = lax.broadcasted_iota(jnp.int32, idx_ref.shape[1:], 1)
    vals, idxs = [], []
    rest = logits
    for _ in range(TOP_K):
        m = jnp.max(rest, axis=-1, keepdims=True)
        idx = jnp.min(jnp.where(rest == m, lane, n_exp), axis=-1, keepdims=True)
        vals.append(m)
        idxs.append(idx)
        rest = jnp.where(lane == idx, -jnp.inf, rest)
    exps = [jnp.exp(v - vals[0]) for v in vals]
    tot = exps[0]
    for e in exps[1:]:
        tot = tot + e
    idx_out = jnp.zeros(idx_ref.shape[1:], jnp.int32)
    gw_out = jnp.zeros(gw_ref.shape[1:], F32)
    for k in range(TOP_K):
        idx_out = jnp.where(out_lane == k, idxs[k], idx_out)
        gw_out = jnp.where(out_lane == k, exps[k] / tot, gw_out)
    idx_ref[0] = idx_out
    gw_ref[0] = gw_out


def _norm_router(x, w, modt, w_router, b_router, ctx_len):
    b, p, d = x.shape
    n_exp = w_router.shape[-1]
    tm = _pick(math.gcd(p, ctx_len), ROW_TILE)
    ct = ctx_len // tm
    row = lambda bb, i: (bb, i, 0)
    return pl.pallas_call(
        _norm_router_kernel,
        out_shape=(jax.ShapeDtypeStruct((b, p, d // 2), jnp.uint32),
                   jax.ShapeDtypeStruct((b, p, LANES), jnp.int32),
                   jax.ShapeDtypeStruct((b, p, LANES), F32)),
        grid=(b, p // tm),
        in_specs=[pl.BlockSpec((1, tm, d), row),
                  pl.BlockSpec((1, d), lambda bb, i: (0, 0)),
                  _mod_spec(d, 3, ct), _mod_spec(d, 4, ct),
                  pl.BlockSpec((d, n_exp), lambda bb, i: (0, 0)),
                  pl.BlockSpec((1, n_exp), lambda bb, i: (0, 0))],
        out_specs=(pl.BlockSpec((1, tm, d // 2), row),
                   pl.BlockSpec((1, tm, LANES), row),
                   pl.BlockSpec((1, tm, LANES), row)),
        compiler_params=_cparams(("arbitrary", "arbitrary")),
        name="norm_router",
    )(x, w.reshape(1, d), modt, modt, w_router, b_router.reshape(1, n_exp))


def _qkv_kernel(a_ref, w_ref, g_ref, cos_ref, sin_ref, o_ref, *, n_norm_blocks):
    j = pl.program_id(0)
    acc = jnp.dot(a_ref[...], w_ref[...], preferred_element_type=F32)
    heads = acc.shape[1] // HEAD_DIM

    @pl.when(j < n_norm_blocks)
    def _():
        cos = cos_ref[...]
        sin = sin_ref[...]
        lane = lax.broadcasted_iota(jnp.int32, cos.shape, 1)
        first = (lane // (HEAD_DIM // 4)) % 2 == 0
        for hh in range(heads):
            sl = slice(hh * HEAD_DIM, (hh + 1) * HEAD_DIM)
            xh = acc[:, sl]
            r = lax.rsqrt(jnp.mean(xh * xh, axis=-1, keepdims=True) + EPS)
            y = xh * r * g_ref[:, sl]
            partner = jnp.where(first, pltpu.roll(y, HEAD_DIM - HEAD_DIM // 4, 1),
                                pltpu.roll(y, HEAD_DIM // 4, 1))
            o_ref[:, sl] = (y * cos + partner * sin).astype(o_ref.dtype)

    @pl.when(j >= n_norm_blocks)
    def _():
        o_ref[...] = acc.astype(o_ref.dtype)


def _qkv_proj(u2d, w_bf16, gain_row, cos_t, sin_t, n_norm_cols, pos_tiles_of):
    m, k = u2d.shape
    n = w_bf16.shape[1]
    tm = pos_tiles_of[0]
    n_pos_tiles = pos_tiles_of[1]
    tn = MM_TILE_N if (n % MM_TILE_N == 0 and n_norm_cols % MM_TILE_N == 0) else HEAD_DIM
    return pl.pallas_call(
        functools.partial(_qkv_kernel, n_norm_blocks=n_norm_cols // tn),
        out_shape=jax.ShapeDtypeStruct((m, n), BF16),
        grid=(n // tn, m // tm),
        in_specs=[pl.BlockSpec((tm, k), lambda j, i: (i, 0)),
                  pl.BlockSpec((k, tn), lambda j, i: (0, j)),
                  pl.BlockSpec((1, tn), lambda j, i: (0, j)),
                  pl.BlockSpec((tm, HEAD_DIM), lambda j, i: (i % n_pos_tiles, 0)),
                  pl.BlockSpec((tm, HEAD_DIM), lambda j, i: (i % n_pos_tiles, 0))],
        out_specs=pl.BlockSpec((tm, tn), lambda j, i: (i, j)),
        compiler_params=_cparams(("arbitrary", "arbitrary")),
        name="qkv_proj",
    )(u2d, w_bf16, gain_row, cos_t, sin_t)


def _attn_kernel(sink_ref, q_ref, kc_ref, vc_ref, kp_ref, ks_ref, kn_ref, vp_ref, vs_ref, vn_ref,
                 o_ref, *, q_per_kv, ctx_blocks, n_blocks):
    h = pl.program_id(1)
    i = pl.program_id(2)
    blk = ATT_BLOCK
    q = q_ref[0]
    qs = jnp.concatenate([q[:, g * HEAD_DIM:(g + 1) * HEAD_DIM] for g in range(q_per_kv)], axis=0)
    rows = qs.shape[0]
    dn = (((1,), (1,)), ((), ()))
    k_loc = jnp.concatenate([kp_ref[0], ks_ref[0], kn_ref[0]], axis=0)
    v_loc = jnp.concatenate([vp_ref[0], vs_ref[0], vn_ref[0]], axis=0)
    s_loc = lax.dot_general(qs, k_loc, dn, preferred_element_type=F32)
    s_ctx = lax.dot_general(qs, kc_ref[0], dn, preferred_element_type=F32)

    qi = lax.broadcasted_iota(jnp.int32, s_loc.shape, 0) % blk
    kj = lax.broadcasted_iota(jnp.int32, s_loc.shape, 1)
    is_lat = i >= ctx_blocks
    ok_prev = (kj < blk) & (kj >= qi) & (i > ctx_blocks)
    ok_self = (kj >= blk) & (kj < 2 * blk)
    ok_next = (kj >= 2 * blk) & (kj - 2 * blk <= qi) & (i < n_blocks - 1)
    valid = (ok_prev | ok_self | ok_next) & is_lat
    s_loc = jnp.where(valid, s_loc, NEG_INF)

    row_head = lax.broadcasted_iota(jnp.int32, (rows, 1), 0) // blk
    sink = jnp.zeros((rows, 1), F32)
    for g in range(q_per_kv):
        sink = jnp.where(row_head == g, sink_ref[h * q_per_kv + g], sink)

    m = jnp.maximum(jnp.maximum(jnp.max(s_loc, axis=-1, keepdims=True),
                                jnp.max(s_ctx, axis=-1, keepdims=True)), sink)
    p_loc = jnp.exp(s_loc - m)
    p_ctx = jnp.exp(s_ctx - m)
    denom = (jnp.sum(p_loc, axis=-1, keepdims=True) + jnp.sum(p_ctx, axis=-1, keepdims=True)
             + jnp.exp(sink - m))
    o = (jnp.dot(p_loc.astype(BF16), v_loc, preferred_element_type=F32)
         + jnp.dot(p_ctx.astype(BF16), vc_ref[0], preferred_element_type=F32)) / denom
    o = o.astype(o_ref.dtype)
    for g in range(q_per_kv):
        o_ref[0, :, g * HEAD_DIM:(g + 1) * HEAD_DIM] = o[g * blk:(g + 1) * blk]


def _attention(qkv, sinks, n_q, n_kv, ctx_len):
    b, p, _ = qkv.shape
    blk = ATT_BLOCK
    nb = p // blk
    cb = ctx_len // blk
    qpk = n_q // n_kv
    qw = qpk * HEAD_DIM
    kcol = n_q
    vcol = n_q + n_kv

    def loc(col0, off):
        return pl.BlockSpec(
            (1, blk, HEAD_DIM),
            lambda bb, h, i, s: (bb, jnp.clip(i + off, cb, nb - 1), col0 + h))

    return pl.pallas_call(
        functools.partial(_attn_kernel, q_per_kv=qpk, ctx_blocks=cb, n_blocks=nb),
        out_shape=jax.ShapeDtypeStruct((b, p, n_q * HEAD_DIM), BF16),
        grid_spec=pltpu.PrefetchScalarGridSpec(
            num_scalar_prefetch=1,
            grid=(b, n_kv, nb),
            in_specs=[pl.BlockSpec((1, blk, qw), lambda bb, h, i, s: (bb, i, h)),
                      pl.BlockSpec((1, ctx_len, HEAD_DIM), lambda bb, h, i, s: (bb, 0, kcol + h)),
                      pl.BlockSpec((1, ctx_len, HEAD_DIM), lambda bb, h, i, s: (bb, 0, vcol + h)),
                      loc(kcol, -1), loc(kcol, 0), loc(kcol, 1),
                      loc(vcol, -1), loc(vcol, 0), loc(vcol, 1)],
            out_specs=pl.BlockSpec((1, blk, qw), lambda bb, h, i, s: (bb, i, h))),
        compiler_params=_cparams(("arbitrary", "arbitrary", "arbitrary")),
        name="window_attn",
    )(sinks, qkv, qkv, qkv, qkv, qkv, qkv, qkv, qkv, qkv)


def _gate_spec(d, which, ctx_tiles, tiles_per_batch, tn):
    per = d // tn

    def imap(j, i):
        bb = i // tiles_per_batch
        seg = jnp.where(i % tiles_per_batch >= ctx_tiles, 1, 0)
        return (bb * 2 + seg, 0, which * per + j)

    return pl.BlockSpec((1, 1, tn), imap)


def _proj_res_kernel(a_ref, w_ref, x_ref, g_ref, o_ref):
    y = jnp.dot(a_ref[...], w_ref[...], preferred_element_type=F32)
    o_ref[...] = x_ref[...] + g_ref[0] * y


def _proj_residual(a2d, w_bf16, x2d, modt, which_gate, tm, ctx_tiles, tiles_per_batch):
    m, k = a2d.shape
    n = w_bf16.shape[1]
    tn = _pick(n, MM_TILE_N, LANES)
    return pl.pallas_call(
        _proj_res_kernel,
        out_shape=jax.ShapeDtypeStruct((m, n), F32),
        grid=(n // tn, m // tm),
        in_specs=[pl.BlockSpec((tm, k), lambda j, i: (i, 0)),
                  pl.BlockSpec((k, tn), lambda j, i: (0, j)),
                  pl.BlockSpec((tm, tn), lambda j, i: (i, j)),
                  _gate_spec(n, which_gate, ctx_tiles, tiles_per_batch, tn)],
        out_specs=pl.BlockSpec((tm, tn), lambda j, i: (i, j)),
        compiler_params=_cparams(("arbitrary", "arbitrary")),
        name="proj_residual",
    )(a2d, w_bf16, x2d, modt)


def _glu_res_kernel(a_ref, w1_ref, w2_ref, b1_ref, b2_ref, x_ref, g_ref, o_ref):
    a = a_ref[...]
    z1 = jnp.dot(a, w1_ref[...], preferred_element_type=F32) + b1_ref[...]
    z2 = jnp.dot(a, w2_ref[...], preferred_element_type=F32) + b2_ref[...]
    o_ref[...] = x_ref[...] + g_ref[0] * (z1 * jax.nn.sigmoid(z2))


def _glu_residual(a2d, w_bf16, b_glu, x2d, modt, which_gate, tm, ctx_tiles, tiles_per_batch):
    m, k = a2d.shape
    d = w_bf16.shape[1] // 2
    tn = _pick(d, MM_TILE_N, LANES)
    nj = d // tn
    b2 = b_glu.reshape(1, 2 * d)
    return pl.pallas_call(
        _glu_res_kernel,
        out_shape=jax.ShapeDtypeStruct((m, d), F32),
        grid=(nj, m // tm),
        in_specs=[pl.BlockSpec((tm, k), lambda j, i: (i, 0)),
                  pl.BlockSpec((k, tn), lambda j, i: (0, j)),
                  pl.BlockSpec((k, tn), lambda j, i: (0, j + nj)),
                  pl.BlockSpec((1, tn), lambda j, i: (0, j)),
                  pl.BlockSpec((1, tn), lambda j, i: (0, j + nj)),
                  pl.BlockSpec((tm, tn), lambda j, i: (i, j)),
                  _gate_spec(d, which_gate, ctx_tiles, tiles_per_batch, tn)],
        out_specs=pl.BlockSpec((tm, tn), lambda j, i: (i, j)),
        compiler_params=_cparams(("arbitrary", "arbitrary")),
        name="glu_residual",
    )(a2d, w_bf16, w_bf16, b2, b2, x2d, modt)


def _s5_tables(a_re, a_im, log_dt, b_re, b_im, c_re, c_im, d_skip):
    q, n_state, pch = SSM_CHUNK, a_re.shape[-1], b_re.shape[-1]
    g = a_re.shape[1]
    hp = lax.Precision.HIGHEST
    lam = lax.complex(jnp.minimum(a_re.astype(F32), DT_FLOOR_RE), a_im.astype(F32))
    lam_dt = lam * jnp.exp(log_dt.astype(F32))[..., None]
    b_bar = ((jnp.exp(lam_dt) - 1.0) / lam)[..., None] * lax.complex(b_re.astype(F32), b_im.astype(F32))
    c_mat = lax.complex(c_re.astype(F32), c_im.astype(F32))
    tau = jnp.arange(q + 1, dtype=F32)
    pw = jnp.exp(lam_dt[:, :, None, :] * tau[None, None, :, None])
    kern = jnp.einsum('dgpn,dgtn,dgnr->dgtpr', c_mat, pw[:, :, :q], b_bar, precision=hp).real
    t_i = jnp.arange(q)[:, None]
    s_i = jnp.arange(q)[None, :]
    kf = jnp.where((t_i >= s_i)[None, :, :, None, None], kern[0][:, jnp.clip(t_i - s_i, 0, q - 1)], 0.0)
    kb = jnp.where((s_i >= t_i)[None, :, :, None, None], kern[1][:, jnp.clip(s_i - t_i, 0, q - 1)], 0.0)
    dsk = d_skip.astype(F32).reshape(g, pch)
    diag = (jnp.eye(q)[None, :, :, None, None] * jnp.eye(pch)[None, None, None] * dsk[:, None, None, :, None])
    mt = (kf + kb + diag).transpose(0, 2, 4, 1, 3).reshape(g, q * pch, q * pch)
    wsf = pw[0][:, ::-1][:, 1:, :, None] * b_bar[0][:, None]
    wsb = pw[1][:, :q, :, None] * b_bar[1][:, None]
    wsf = wsf.transpose(0, 1, 3, 2).reshape(g, q * pch, n_state)
    wsb = wsb.transpose(0, 1, 3, 2).reshape(g, q * pch, n_state)
    w1 = jnp.concatenate([mt, wsf.real, wsb.real, wsf.imag, wsb.imag], axis=-1)
    cf = c_mat[0][:, None] * pw[0][:, 1:, None, :]
    cb = c_mat[1][:, None] * pw[1][:, ::-1][:, :q, None, :]
    cf = cf.transpose(0, 3, 1, 2).reshape(g, n_state, q * pch)
    cb = cb.transpose(0, 3, 1, 2).reshape(g, n_state, q * pch)
    w3 = jnp.concatenate([cf.real, cb.real, -cf.imag, -cb.imag], axis=1)
    aq = pw[:, :, q]
    a_r = jnp.concatenate([aq[0].real, aq[1].real], axis=-1)
    a_i = jnp.concatenate([aq[0].imag, aq[1].imag], axis=-1)
    return w1.astype(BF16), w3.astype(BF16), a_r, a_i


def _s5_in_kernel(u_ref, w_ref, y_ref, sr_ref, si_ref):
    width = u_ref.shape[-1]
    ns2 = sr_ref.shape[-1]
    for g in range(u_ref.shape[0]):
        r = jnp.dot(u_ref[g], w_ref[g], preferred_element_type=F32)
        y_ref[g] = r[:, :width]
        sr_ref[0, g] = r[:, width:width + ns2]
        si_ref[0, g] = r[:, width + ns2:]


def _s5_scan_kernel(sr_ref, si_ref, ar_ref, ai_ref, hr_ref, hi_ref, fr, fi, br, bi, *, n_chunks, ctx_chunks):
    gb = ar_ref.shape[0]
    a_r = ar_ref[...]
    a_i = ai_ref[...]
    lane = lax.broadcasted_iota(jnp.int32, a_r.shape, 1)
    fwd = lane < (a_r.shape[1] // 2)

    def rows(ref, c):
        return ref[pl.ds(c, gb, stride=n_chunks), :]

    def body(i, carry):
        h_r, h_i = carry
        cf = i
        cb = jnp.where(i < ctx_chunks, ctx_chunks - 1 - i, n_chunks + ctx_chunks - 1 - i)
        fr[pl.ds(cf, gb, stride=n_chunks), :] = h_r
        fi[pl.ds(cf, gb, stride=n_chunks), :] = h_i
        br[pl.ds(cb, gb, stride=n_chunks), :] = h_r
        bi[pl.ds(cb, gb, stride=n_chunks), :] = h_i
        s_r = jnp.where(fwd, rows(sr_ref, cf), rows(sr_ref, cb))
        s_i = jnp.where(fwd, rows(si_ref, cf), rows(si_ref, cb))
        return (a_r * h_r - a_i * h_i + s_r, a_r * h_i + a_i * h_r + s_i)

    zero = jnp.zeros(a_r.shape, F32)
    lax.fori_loop(0, n_chunks, body, (zero, zero))
    full = lax.broadcasted_iota(jnp.int32, hr_ref.shape, 1) < (a_r.shape[1] // 2)
    hr_ref[...] = jnp.where(full, fr[...], br[...])
    hi_ref[...] = jnp.where(full, fi[...], bi[...])


def _s5_out_kernel(y_ref, hr_ref, hi_ref, w_ref, o_ref):
    ns2 = hr_ref.shape[-1]
    for g in range(y_ref.shape[0]):
        y = (y_ref[g]
             + jnp.dot(hr_ref[0, g].astype(BF16), w_ref[g, :ns2], preferred_element_type=F32)
             + jnp.dot(hi_ref[0, g].astype(BF16), w_ref[g, ns2:], preferred_element_type=F32))
        o_ref[g] = jax.nn.gelu(y).astype(o_ref.dtype)


def _s5_mix(u, tables, ctx_len):
    w1, w3, a_r, a_i = tables
    b, p, d = u.shape
    q, pch = SSM_CHUNK, SSM_GROUP
    g = d // pch
    nc = p // q
    width = q * pch
    ns2 = 2 * SSM_STATE
    gb = _pick(g, SSM_GROUP_BLOCK, 8)
    ut = u.reshape(b, nc, q, g, pch).transpose(3, 0, 1, 2, 4).reshape(g, b * nc, width)

    y_in, s_r, s_i = pl.pallas_call(
        _s5_in_kernel,
        out_shape=(jax.ShapeDtypeStruct((g, b * nc, width), F32),
                   jax.ShapeDtypeStruct((b, g, nc, ns2), F32),
                   jax.ShapeDtypeStruct((b, g, nc, ns2), F32)),
        grid=(g // gb, b),
        in_specs=[pl.BlockSpec((gb, nc, width), lambda gi, bb: (gi, bb, 0)),
                  pl.BlockSpec((gb, width, width + 2 * ns2), lambda gi, bb: (gi, 0, 0))],
        out_specs=(pl.BlockSpec((gb, nc, width), lambda gi, bb: (gi, bb, 0)),
                   pl.BlockSpec((1, gb, nc, ns2), lambda gi, bb: (bb, gi, 0, 0)),
                   pl.BlockSpec((1, gb, nc, ns2), lambda gi, bb: (bb, gi, 0, 0))),
        compiler_params=_cparams(("arbitrary", "arbitrary")),
        name="s5_chunk_in",
    )(ut, w1)

    flat = lambda gi, bb: (bb * (g // gb) + gi, 0)
    h_r, h_i = pl.pallas_call(
        functools.partial(_s5_scan_kernel, n_chunks=nc, ctx_chunks=ctx_len // q),
        out_shape=(jax.ShapeDtypeStruct((b * g * nc, ns2), F32),) * 2,
        grid=(g // gb, b),
        in_specs=[pl.BlockSpec((gb * nc, ns2), flat), pl.BlockSpec((gb * nc, ns2), flat),
                  pl.BlockSpec((gb, ns2), lambda gi, bb: (gi, 0)),
                  pl.BlockSpec((gb, ns2), lambda gi, bb: (gi, 0))],
        out_specs=(pl.BlockSpec((gb * nc, ns2), flat),) * 2,
        scratch_shapes=[pltpu.VMEM((gb * nc, ns2), F32)] * 4,
        compiler_params=_cparams(("arbitrary", "arbitrary")),
        name="s5_chunk_scan",
    )(s_r.reshape(b * g * nc, ns2), s_i.reshape(b * g * nc, ns2), a_r, a_i)

    yt = pl.pallas_call(
        _s5_out_kernel,
        out_shape=jax.ShapeDtypeStruct((g, b * nc, width), BF16),
        grid=(g // gb, b),
        in_specs=[pl.BlockSpec((gb, nc, width), lambda gi, bb: (gi, bb, 0)),
                  pl.BlockSpec((1, gb, nc, ns2), lambda gi, bb: (bb, gi, 0, 0)),
                  pl.BlockSpec((1, gb, nc, ns2), lambda gi, bb: (bb, gi, 0, 0)),
                  pl.BlockSpec((gb, 2 * ns2, width), lambda gi, bb: (gi, 0, 0))],
        out_specs=pl.BlockSpec((gb, nc, width), lambda gi, bb: (gi, bb, 0)),
        compiler_params=_cparams(("arbitrary", "arbitrary")),
        name="s5_chunk_out",
    )(y_in, h_r.reshape(b, g, nc, ns2), h_i.reshape(b, g, nc, ns2), w3)
    return yt.reshape(g, b, nc, q, pch).transpose(1, 2, 3, 0, 4).reshape(b, p, d)


def _moe_schedule(idx, n_exp, tm, n_tiles):
    t = idx.shape[0]
    e_flat = idx.reshape(-1)
    n_asg = e_flat.shape[0]
    order = jnp.argsort(e_flat, stable=True).astype(jnp.int32)
    counts = jnp.sum((e_flat[:, None] == jnp.arange(n_exp)[None, :]).astype(jnp.int32), axis=0)
    padded = (counts + tm - 1) // tm * tm
    gstart = jnp.cumsum(padded) - padded
    cstart = jnp.cumsum(counts) - counts
    gend = gstart + padded
    n_active = (jnp.sum(padded) // tm).astype(jnp.int32)
    tile_start = jnp.arange(n_tiles, dtype=jnp.int32) * tm
    tile_e = jnp.minimum(jnp.searchsorted(gend, tile_start, side='right'), n_exp - 1).astype(jnp.int32)
    tile_e = jnp.where(jnp.arange(n_tiles) < n_active, tile_e, tile_e[jnp.maximum(n_active - 1, 0)])
    row_e = jnp.repeat(tile_e, tm)
    j = jnp.arange(n_tiles * tm, dtype=jnp.int32) - gstart[row_e].astype(jnp.int32)
    valid = (j < counts[row_e]) & (jnp.arange(n_tiles * tm) < n_active * tm)
    sidx = jnp.clip(cstart[row_e] + j, 0, n_asg - 1)
    row_token = jnp.where(valid, order[sidx] // TOP_K, 0).astype(jnp.int32)
    sorted_e = e_flat[order]
    dest = (gstart[sorted_e] + jnp.arange(n_asg) - cstart[sorted_e]).astype(jnp.int32)
    inv = jnp.argsort(order).astype(jnp.int32)
    pos = dest[inv].reshape(t, TOP_K)
    return tile_e, n_active.reshape(1), row_token, pos


def _gather_kernel(tok_ref, h_hbm, o_hbm, sem, *, rows):
    i = pl.program_id(0)
    n = pl.num_programs(0)

    def wait_rows(slot):
        def body(r, c):
            pltpu.make_async_copy(h_hbm.at[pl.ds(0, 1)], o_hbm.at[pl.ds(0, 1)], sem.at[slot]).wait()
            return c
        lax.fori_loop(0, rows, body, 0)

    def body(r, c):
        tok = tok_ref[0, 0, r]
        pltpu.make_async_copy(h_hbm.at[pl.ds(tok, 1)], o_hbm.at[pl.ds(i * rows + r, 1)],
                              sem.at[i % 2]).start()
        return c
    lax.fori_loop(0, rows, body, 0)

    @pl.when(i > 0)
    def _():
        wait_rows((i - 1) % 2)

    @pl.when(i == n - 1)
    def _():
        wait_rows(i % 2)


def _moe_gather(hp2d, row_token):
    r_max = row_token.shape[0]
    rows = _pick(r_max, GATHER_ROWS, 8)
    n_steps = r_max // rows
    return pl.pallas_call(
        functools.partial(_gather_kernel, rows=rows),
        out_shape=jax.ShapeDtypeStruct((r_max, hp2d.shape[1]), hp2d.dtype),
        grid=(n_steps,),
        in_specs=[pl.BlockSpec((1, 1, rows), lambda i: (i, 0, 0), memory_space=pltpu.SMEM),
                  pl.BlockSpec(memory_space=pl.ANY)],
        out_specs=pl.BlockSpec(memory_space=pl.ANY),
        scratch_shapes=[pltpu.SemaphoreType.DMA((2,))],
        compiler_params=_cparams(("arbitrary",)),
        name="moe_gather",
    )(row_token.reshape(n_steps, 1, rows), hp2d)


def _moe_kernel(te_ref, na_ref, x_ref, wg_ref, wu_ref, bg_ref, bu_ref, wd_ref, bd_ref, o_ref):
    i = pl.program_id(0)

    @pl.when(i < na_ref[0])
    def _():
        lo, hi = _unpack_halves(x_ref[...])
        lo = lo.astype(BF16)
        hi = hi.astype(BF16)
        half = lo.shape[1]

        def proj(w_ref, b_ref):
            return (jnp.dot(lo, w_ref[0, :half], preferred_element_type=F32)
                    + jnp.dot(hi, w_ref[0, half:], preferred_element_type=F32) + b_ref[0])

        gate = jnp.minimum(proj(wg_ref, bg_ref), SWIGLU_LIMIT)
        up = jnp.clip(proj(wu_ref, bu_ref), -SWIGLU_LIMIT, SWIGLU_LIMIT)
        act = (up + 1.0) * (gate * jax.nn.sigmoid(SWIGLU_ALPHA * gate))
        o_ref[...] = jnp.dot(act.astype(BF16), wd_ref[0], preferred_element_type=F32) + bd_ref[0]

    @pl.when(i >= na_ref[0])
    def _():
        o_ref[...] = jnp.zeros(o_ref.shape, o_ref.dtype)


def _moe_experts(x_sorted, tile_e, n_active, wg, wu, bg, bu, wd, bd, tm):
    r_max, half = x_sorted.shape
    n_exp, d, f = wg.shape
    n_tiles = r_max // tm
    row = lambda i, te, na: (jnp.minimum(i, jnp.maximum(na[0] - 1, 0)), 0)
    exp3 = lambda i, te, na: (te[i], 0, 0)
    return pl.pallas_call(
        _moe_kernel,
        out_shape=jax.ShapeDtypeStruct((r_max, d), F32),
        grid_spec=pltpu.PrefetchScalarGridSpec(
            num_scalar_prefetch=2,
            grid=(n_tiles,),
            in_specs=[pl.BlockSpec((tm, half), row),
                      pl.BlockSpec((1, d, f), exp3), pl.BlockSpec((1, d, f), exp3),
                      pl.BlockSpec((1, 1, f), exp3), pl.BlockSpec((1, 1, f), exp3),
                      pl.BlockSpec((1, f, d), exp3), pl.BlockSpec((1, 1, d), exp3)],
            out_specs=pl.BlockSpec((tm, d), lambda i, te, na: (i, 0))),
        compiler_params=_cparams(("arbitrary",)),
        name="moe_experts",
    )(tile_e, n_active, x_sorted, wg, wu, bg, bu, wd, bd)


def _combine_kernel(pos_ref, posn_ref, gw_ref, x_ref, g_ref, y_hbm, o_ref, buf, sem, *, rows):
    i = pl.program_id(0)
    n = pl.num_programs(0)

    def issue(p_ref, slot):
        def body(r, c):
            for k in range(TOP_K):
                src = p_ref[0, 0, r * TOP_K + k]
                pltpu.make_async_copy(y_hbm.at[pl.ds(src, 1)], buf.at[slot, k, pl.ds(r, 1)],
                                      sem.at[slot]).start()
            return c
        lax.fori_loop(0, rows, body, 0)

    @pl.when(i == 0)
    def _():
        issue(pos_ref, 0)

    @pl.when(i + 1 < n)
    def _():
        issue(posn_ref, (i + 1) % 2)

    slot = i % 2

    def wbody(r, c):
        pltpu.make_async_copy(y_hbm.at[pl.ds(0, 1)], buf.at[slot, 0, pl.ds(0, 1)], sem.at[slot]).wait()
        return c
    lax.fori_loop(0, rows * TOP_K, wbody, 0)

    gw = gw_ref[...]
    acc = gw[:, 0:1] * buf[slot, 0]
    for k in range(1, TOP_K):
        acc = acc + gw[:, k:k + 1] * buf[slot, k]
    o_ref[...] = x_ref[...] + g_ref[0] * acc


def _moe_combine(y_sorted, pos, gw2d, x2d, modt, tm_row, ctx_tiles, tiles_per_batch):
    t, d = x2d.shape
    rows = _pick(tm_row, COMBINE_ROWS, 8)
    n_steps = t // rows
    sub = tm_row // rows
    pos3 = pos.reshape(n_steps, 1, rows * TOP_K)

    def gmap(i):
        tile = i // sub
        bb = tile // tiles_per_batch
        seg = jnp.where(tile % tiles_per_batch >= ctx_tiles, 1, 0)
        return (bb * 2 + seg, 0, 5)

    return pl.pallas_call(
        functools.partial(_combine_kernel, rows=rows),
        out_shape=jax.ShapeDtypeStruct((t, d), F32),
        grid=(n_steps,),
        in_specs=[pl.BlockSpec((1, 1, rows * TOP_K), lambda i: (i, 0, 0), memory_space=pltpu.SMEM),
                  pl.BlockSpec((1, 1, rows * TOP_K), lambda i: (jnp.minimum(i + 1, n_steps - 1), 0, 0),
                               memory_space=pltpu.SMEM),
                  pl.BlockSpec((rows, LANES), lambda i: (i, 0)),
                  pl.BlockSpec((rows, d), lambda i: (i, 0)),
                  pl.BlockSpec((1, 1, d), gmap),
                  pl.BlockSpec(memory_space=pl.ANY)],
        out_specs=pl.BlockSpec((rows, d), lambda i: (i, 0)),
        scratch_shapes=[pltpu.VMEM((2, TOP_K, rows, d), F32), pltpu.SemaphoreType.DMA((2,))],
        compiler_params=_cparams(("arbitrary",)),
        name="moe_combine",
    )(pos3, pos3, gw2d, x2d, modt, y_sorted)


def _rope_tables(seq, ctx_len):
    n_rows = seq // GRID_W
    axis_rot = HEAD_DIM // 2
    rows = jnp.repeat(jnp.arange(n_rows, dtype=F32), GRID_W)
    cols = jnp.tile(jnp.arange(GRID_W, dtype=F32), n_rows)
    inv_freq = ROPE_BASE ** (-jnp.arange(0, axis_rot, 2, dtype=F32) / axis_rot)
    ang_r = rows[:, None] * inv_freq
    ang_c = cols[:, None] * inv_freq
    ang = jnp.concatenate([ang_r, ang_r, ang_c, ang_c], axis=-1)
    sign = jnp.tile(jnp.concatenate([-jnp.ones(axis_rot // 2, F32), jnp.ones(axis_rot // 2, F32)]), 2)
    cos = jnp.concatenate([jnp.ones((ctx_len, HEAD_DIM), F32), jnp.cos(ang)], axis=0)
    sin = jnp.concatenate([jnp.zeros((ctx_len, HEAD_DIM), F32), jnp.sin(ang) * sign], axis=0)
    return cos, sin


def kernel(x, c, ctx, c_ctx, w_mod, b_mod, norm_mix, norm_ffn, w_router, b_router, w_gate_up, b_gate_up, w_down, b_down, attn_w_qkv, attn_w_o, attn_q_gain, attn_k_gain, attn_sinks, ssm_a_re, ssm_a_im, ssm_log_dt, ssm_b_re, ssm_b_im, ssm_c_re, ssm_c_im, ssm_d, ssm_w_glu, ssm_b_glu):
    b, seq, d = x.shape
    ctx_len = ctx.shape[1]
    depth = w_mod.shape[0]
    p = ctx_len + seq
    t = b * p
    n_exp = w_router.shape[-1]
    n_q = d // HEAD_DIM
    n_kv = (attn_w_qkv.shape[-1] // HEAD_DIM - n_q) // 2
    assert ctx_len % ATT_BLOCK == 0 and seq % ATT_BLOCK == 0 and seq % GRID_W == 0
    assert d % (2 * LANES) == 0 and ctx_len % SSM_CHUNK == 0 and seq % SSM_CHUNK == 0

    tm = _pick(math.gcd(p, ctx_len), ROW_TILE)
    mm_tm = _pick(math.gcd(p, ctx_len), MM_TILE_M)
    tiles_pb = p // mm_tm
    ctx_tiles = ctx_len // mm_tm

    pad = (-(b + 1)) % 8
    cvec = jnp.concatenate([c, c_ctx[None, :], jnp.zeros((pad, d), F32)], axis=0)
    mod_all = _mod_all(cvec, w_mod, b_mod)

    cos_t, sin_t = _rope_tables(seq, ctx_len)
    xs = jnp.concatenate([ctx, x], axis=1)

    tm_moe = min(MOE_TILE, _pick(t * TOP_K, MOE_TILE, 8))
    n_tiles = (t * TOP_K) // tm_moe + n_exp
    scale = HEAD_DIM ** -0.5

    for i in range(depth):
        j = i // 2
        lat = mod_all[i, :b]
        cx = jnp.broadcast_to(mod_all[i, b][None], lat.shape)
        modt = jnp.stack([cx, lat], axis=1).reshape(b * 2, 1, 6 * d)

        u = _norm_mod(xs, norm_mix[i], modt, 0, ctx_len)
        x2d = xs.reshape(t, d)
        if i % 2 == 0:
            gain = jnp.concatenate([jnp.tile(attn_q_gain[j] * scale, n_q), jnp.tile(attn_k_gain[j], n_kv),
                                    jnp.ones((n_kv * HEAD_DIM,), F32)]).reshape(1, -1)
            qkv = _qkv_proj(u.reshape(t, d), attn_w_qkv[j].astype(BF16), gain, cos_t, sin_t,
                            (n_q + n_kv) * HEAD_DIM, (mm_tm, tiles_pb))
            o = _attention(qkv.reshape(b, p, -1), attn_sinks[j].astype(F32), n_q, n_kv, ctx_len)
            x2d = _proj_residual(o.reshape(t, d), attn_w_o[j].astype(BF16), x2d, modt, 2,
                                 mm_tm, ctx_tiles, tiles_pb)
        else:
            tables = _s5_tables(ssm_a_re[j], ssm_a_im[j], ssm_log_dt[j], ssm_b_re[j], ssm_b_im[j],
                                ssm_c_re[j], ssm_c_im[j], ssm_d[j])
            gy = _s5_mix(u, tables, ctx_len)
            x2d = _glu_residual(gy.reshape(t, d), ssm_w_glu[j].astype(BF16), ssm_b_glu[j], x2d, modt, 2,
                                mm_tm, ctx_tiles, tiles_pb)

        hp, idx, gw = _norm_router(x2d.reshape(b, p, d), norm_ffn[i], modt, w_router[i], b_router[i], ctx_len)
        idx2d = idx.reshape(t, LANES)[:, :TOP_K]
        tile_e, n_active, row_token, pos = _moe_schedule(idx2d, n_exp, tm_moe, n_tiles)
        x_sorted = _moe_gather(hp.reshape(t, d // 2), row_token)
        wg = w_gate_up[i][:, :, 0::2].astype(BF16)
        wu = w_gate_up[i][:, :, 1::2].astype(BF16)
        bg = b_gate_up[i][:, 0::2].reshape(n_exp, 1, -1)
        bu = b_gate_up[i][:, 1::2].reshape(n_exp, 1, -1)
        y_sorted = _moe_experts(x_sorted, tile_e, n_active, wg, wu, bg, bu,
                                w_down[i].astype(BF16), b_down[i].reshape(n_exp, 1, d), tm_moe)
        x2d = _moe_combine(y_sorted, pos, gw.reshape(t, LANES), x2d, modt, tm, ctx_len // tm, p // tm)
        xs = x2d.reshape(b, p, d)

    return xs[:, ctx_len:]
```

```python
import functools
import math

import jax
import jax.numpy as jnp
from jax import lax
from jax.experimental import pallas as pl
from jax.experimental.pallas import tpu as pltpu

F32 = jnp.float32
BF16 = jnp.bfloat16

HEAD_DIM = 128
ATT_BLOCK = 128
GRID_W = 64
ROPE_BASE = 10000.0
SSM_GROUP = 16
SSM_STATE = 64
SSM_CHUNK = 16
TOP_K = 4
SWIGLU_LIMIT = 7.0
SWIGLU_ALPHA = 1.702
EPS = 1e-6
NEG_INF = -1e30
DT_FLOOR_RE = -1e-4

LANES = 128
VMEM_LIMIT = 56 * 1024 * 1024

ROW_TILE = 256
MM_TILE_M = 512
MM_TILE_N = 512
MOE_TILE = 512
GATHER_ROWS = 512
COMBINE_ROWS = 128
SSM_GROUP_BLOCK = 8


def _cparams(sem):
    return pltpu.CompilerParams(dimension_semantics=sem, vmem_limit_bytes=VMEM_LIMIT)


def _pick(n, pref, mult=8):
    t = min(n, pref)
    while n % t or t % mult:
        t -= 1
    return t


def _mod_kernel(c_ref, w_ref, b_ref, o_ref):
    c = c_ref[...]
    s = c * jax.nn.sigmoid(c)
    o_ref[0] = jnp.dot(s.astype(BF16), w_ref[0].astype(BF16), preferred_element_type=F32) + b_ref[0]


def _mod_all(cvec, w_mod, b_mod):
    depth, d, n = w_mod.shape
    tn = _pick(n, 1024, LANES)
    return pl.pallas_call(
        _mod_kernel,
        out_shape=jax.ShapeDtypeStruct((depth, cvec.shape[0], n), F32),
        grid=(depth, n // tn),
        in_specs=[pl.BlockSpec(cvec.shape, lambda l, j: (0, 0)),
                  pl.BlockSpec((1, d, tn), lambda l, j: (l, 0, j)),
                  pl.BlockSpec((1, 1, tn), lambda l, j: (l, 0, j))],
        out_specs=pl.BlockSpec((1, cvec.shape[0], tn), lambda l, j: (l, 0, j)),
        compiler_params=_cparams(("arbitrary", "arbitrary")),
        name="adaln_rows",
    )(cvec, w_mod, b_mod.reshape(depth, 1, n))


def _normed(x, w, sh, sc):
    r = lax.rsqrt(jnp.mean(x * x, axis=-1, keepdims=True) + EPS)
    return (x * r * w) * (1.0 + sc) + sh


def _norm_mod_kernel(x_ref, w_ref, sh_ref, sc_ref, o_ref):
    o_ref[0] = _normed(x_ref[0], w_ref[...], sh_ref[0], sc_ref[0]).astype(o_ref.dtype)


def _mod_spec(d, which, ctx_tiles):
    return pl.BlockSpec((1, 1, d), lambda b, i: (b * 2 + jnp.where(i >= ctx_tiles, 1, 0), 0, which))


def _norm_mod(x, w, modt, which_shift, ctx_len):
    b, p, d = x.shape
    tm = _pick(math.gcd(p, ctx_len), ROW_TILE)
    ct = ctx_len // tm
    return pl.pallas_call(
        _norm_mod_kernel,
        out_shape=jax.ShapeDtypeStruct((b, p, d), BF16),
        grid=(b, p // tm),
        in_specs=[pl.BlockSpec((1, tm, d), lambda bb, i: (bb, i, 0)),
                  pl.BlockSpec((1, d), lambda bb, i: (0, 0)),
                  _mod_spec(d, which_shift, ct), _mod_spec(d, which_shift + 1, ct)],
        out_specs=pl.BlockSpec((1, tm, d), lambda bb, i: (bb, i, 0)),
        compiler_params=_cparams(("arbitrary", "arbitrary")),
        name="norm_mod",
    )(x, w.reshape(1, d), modt, modt)


def _pack_halves(h):
    half = h.shape[-1] // 2
    bits = lax.bitcast_convert_type(h.astype(BF16).astype(F32), jnp.uint32)
    return (bits[:, :half] >> 16) | (bits[:, half:] & jnp.uint32(0xFFFF0000))


def _unpack_halves(xp):
    lo = lax.bitcast_convert_type(xp << 16, F32)
    hi = lax.bitcast_convert_type(xp & jnp.uint32(0xFFFF0000), F32)
    return lo, hi


def _norm_router_kernel(x_ref, w_ref, sh_ref, sc_ref, wr_ref, br_ref, hp_ref, idx_ref, gw_ref):
    h = _normed(x_ref[0], w_ref[...], sh_ref[0], sc_ref[0])
    hp_ref[0] = _pack_halves(h)
    logits = jnp.dot(h, wr_ref[...], precision=lax.Precision.HIGHEST,
                     preferred_element_type=F32) + br_ref[...]
    n_exp = logits.shape[-1]
    lane = lax.broadcasted_iota(jnp.int32, logits.shape, 1)
    out_lane = lax.broadcasted_iota(jnp.int32, idx_ref.shape[1:], 1)
    vals, idxs = [], []
    rest = logits
    for _ in range(TOP_K):
        m = jnp.max(rest, axis=-1, keepdims=True)
        idx = jnp.min(jnp.where(rest == m, lane, n_exp), axis=-1, keepdims=True)
        vals.append(m)
        idxs.append(idx)
        rest = jnp.where(lane == idx, -jnp.inf, rest)
    exps = [jnp.exp(v - vals[0]) for v in vals]
    tot = exps[0]
    for e in exps[1:]:
        tot = tot + e
    idx_out = jnp.zeros(idx_ref.shape[1:], jnp.int32)
    gw_out = jnp.zeros(gw_ref.shape[1:], F32)
    for k in range(TOP_K):
        idx_out = jnp.where(out_lane == k, idxs[k], idx_out)
        gw_out = jnp.where(out_lane == k, exps[k] / tot, gw_out)
    idx_ref[0] = idx_out
    gw_ref[0] = gw_out


def _norm_router(x, w, modt, w_router, b_router, ctx_len):
    b, p, d = x.shape
    n_exp = w_router.shape[-1]
    tm = _pick(math.gcd(p, ctx_len), ROW_TILE)
    ct = ctx_len // tm
    row = lambda bb, i: (bb, i, 0)
    return pl.pallas_call(
        _norm_router_kernel,
        out_shape=(jax.ShapeDtypeStruct((b, p, d // 2), jnp.uint32),
                   jax.ShapeDtypeStruct((b, p, LANES), jnp.int32),
                   jax.ShapeDtypeStruct((b, p, LANES), F32)),
        grid=(b, p // tm),
        in_specs=[pl.BlockSpec((1, tm, d), row),
                  pl.BlockSpec((1, d), lambda bb, i: (0, 0)),
                  _mod_spec(d, 3, ct), _mod_spec(d, 4, ct),
                  pl.BlockSpec((d, n_exp), lambda bb, i: (0, 0)),
                  pl.BlockSpec((1, n_exp), lambda bb, i: (0, 0))],
        out_specs=(pl.BlockSpec((1, tm, d // 2), row),
                   pl.BlockSpec((1, tm, LANES), row),
                   pl.BlockSpec((1, tm, LANES), row)),
        compiler_params=_cparams(("arbitrary", "arbitrary")),
        name="norm_router",
    )(x, w.reshape(1, d), modt, modt, w_router, b_router.reshape(1, n_exp))


def _qkv_kernel(a_ref, w_ref, g_ref, cos_ref, sin_ref, o_ref, *, n_norm_blocks):
    j = pl.program_id(0)
    acc = jnp.dot(a_ref[...], w_ref[...], preferred_element_type=F32)
    heads = acc.shape[1] // HEAD_DIM

    @pl.when(j < n_norm_blocks)
    def _():
        cos = cos_ref[...]
        sin = sin_ref[...]
        lane = lax.broadcasted_iota(jnp.int32, cos.shape, 1)
        first = (lane // (HEAD_DIM // 4)) % 2 == 0
        for hh in range(heads):
            sl = slice(hh * HEAD_DIM, (hh + 1) * HEAD_DIM)
            xh = acc[:, sl]
            r = lax.rsqrt(jnp.mean(xh * xh, axis=-1, keepdims=True) + EPS)
            y = xh * r * g_ref[:, sl]
            partner = jnp.where(first, pltpu.roll(y, HEAD_DIM - HEAD_DIM // 4, 1),
                                pltpu.roll(y, HEAD_DIM // 4, 1))
            o_ref[:, sl] = (y * cos + partner * sin).astype(o_ref.dtype)

    @pl.when(j >= n_norm_blocks)
    def _():
        o_ref[...] = acc.astype(o_ref.dtype)


def _qkv_proj(u2d, w_bf16, gain_row, cos_t, sin_t, n_norm_cols, pos_tiles_of):
    m, k = u2d.shape
    n = w_bf16.shape[1]
    tm = pos_tiles_of[0]
    n_pos_tiles = pos_tiles_of[1]
    tn = MM_TILE_N if (n % MM_TILE_N == 0 and n_norm_cols % MM_TILE_N == 0) else HEAD_DIM
    return pl.pallas_call(
        functools.partial(_qkv_kernel, n_norm_blocks=n_norm_cols // tn),
        out_shape=jax.ShapeDtypeStruct((m, n), BF16),
        grid=(n // tn, m // tm),
        in_specs=[pl.BlockSpec((tm, k), lambda j, i: (i, 0)),
                  pl.BlockSpec((k, tn), lambda j, i: (0, j)),
                  pl.BlockSpec((1, tn), lambda j, i: (0, j)),
                  pl.BlockSpec((tm, HEAD_DIM), lambda j, i: (i % n_pos_tiles, 0)),
                  pl.BlockSpec((tm, HEAD_DIM), lambda j, i: (i % n_pos_tiles, 0))],
        out_specs=pl.BlockSpec((tm, tn), lambda j, i: (i, j)),
        compiler_params=_cparams(("arbitrary", "arbitrary")),
        name="qkv_proj",
    )(u2d, w_bf16, gain_row, cos_t, sin_t)


def _attn_kernel(sink_ref, q_ref, kc_ref, vc_ref, kp_ref, ks_ref, kn_ref, vp_ref, vs_ref, vn_ref,
                 o_ref, *, q_per_kv, ctx_blocks, n_blocks):
    h = pl.program_id(1)
    i = pl.program_id(2)
    blk = ATT_BLOCK
    q = q_ref[0]
    qs = jnp.concatenate([q[:, g * HEAD_DIM:(g + 1) * HEAD_DIM] for g in range(q_per_kv)], axis=0)
    rows = qs.shape[0]
    dn = (((1,), (1,)), ((), ()))
    k_loc = jnp.concatenate([kp_ref[0], ks_ref[0], kn_ref[0]], axis=0)
    v_loc = jnp.concatenate([vp_ref[0], vs_ref[0], vn_ref[0]], axis=0)
    s_loc = lax.dot_general(qs, k_loc, dn, preferred_element_type=F32)
    s_ctx = lax.dot_general(qs, kc_ref[0], dn, preferred_element_type=F32)

    qi = lax.broadcasted_iota(jnp.int32, s_loc.shape, 0) % blk
    kj = lax.broadcasted_iota(jnp.int32, s_loc.shape, 1)
    is_lat = i >= ctx_blocks
    ok_prev = (kj < blk) & (kj >= qi) & (i > ctx_blocks)
    ok_self = (kj >= blk) & (kj < 2 * blk)
    ok_next = (kj >= 2 * blk) & (kj - 2 * blk <= qi) & (i < n_blocks - 1)
    valid = (ok_prev | ok_self | ok_next) & is_lat
    s_loc = jnp.where(valid, s_loc, NEG_INF)

    row_head = lax.broadcasted_iota(jnp.int32, (rows, 1), 0) // blk
    sink = jnp.zeros((rows, 1), F32)
    for g in range(q_per_kv):
        sink = jnp.where(row_head == g, sink_ref[h * q_per_kv + g], sink)

    m = jnp.maximum(jnp.maximum(jnp.max(s_loc, axis=-1, keepdims=True),
                                jnp.max(s_ctx, axis=-1, keepdims=True)), sink)
    p_loc = jnp.exp(s_loc - m)
    p_ctx = jnp.exp(s_ctx - m)
    denom = (jnp.sum(p_loc, axis=-1, keepdims=True) + jnp.sum(p_ctx, axis=-1, keepdims=True)
             + jnp.exp(sink - m))
    o = (jnp.dot(p_loc.astype(BF16), v_loc, preferred_element_type=F32)
         + jnp.dot(p_ctx.astype(BF16), vc_ref[0], preferred_element_type=F32)) / denom
    o = o.astype(o_ref.dtype)
    for g in range(q_per_kv):
        o_ref[0, :, g * HEAD_DIM:(g + 1) * HEAD_DIM] = o[g * blk:(g + 1) * blk]


def _attention(qkv, sinks, n_q, n_kv, ctx_len):
    b, p, _ = qkv.shape
    blk = ATT_BLOCK
    nb = p // blk
    cb = ctx_len // blk
    qpk = n_q // n_kv
    qw = qpk * HEAD_DIM
    kcol = n_q
    vcol = n_q + n_kv

    def loc(col0, off):
        return pl.BlockSpec(
            (1, blk, HEAD_DIM),
            lambda bb, h, i, s: (bb, jnp.clip(i + off, cb, nb - 1), col0 + h))

    return pl.pallas_call(
        functools.partial(_attn_kernel, q_per_kv=qpk, ctx_blocks=cb, n_blocks=nb),
        out_shape=jax.ShapeDtypeStruct((b, p, n_q * HEAD_DIM), BF16),
        grid_spec=pltpu.PrefetchScalarGridSpec(
            num_scalar_prefetch=1,
            grid=(b, n_kv, nb),
            in_specs=[pl.BlockSpec((1, blk, qw), lambda bb, h, i, s: (bb, i, h)),
                      pl.BlockSpec((1, ctx_len, HEAD_DIM), lambda bb, h, i, s: (bb, 0, kcol + h)),
                      pl.BlockSpec((1, ctx_len, HEAD_DIM), lambda bb, h, i, s: (bb, 0, vcol + h)),
                      loc(kcol, -1), loc(kcol, 0), loc(kcol, 1),
                      loc(vcol, -1), loc(vcol, 0), loc(vcol, 1)],
            out_specs=pl.BlockSpec((1, blk, qw), lambda bb, h, i, s: (bb, i, h))),
        compiler_params=_cparams(("arbitrary", "arbitrary", "arbitrary")),
        name="window_attn",
    )(sinks, qkv, qkv, qkv, qkv, qkv, qkv, qkv, qkv, qkv)


def _gate_spec(d, which, ctx_tiles, tiles_per_batch, tn):
    per = d // tn

    def imap(j, i):
        bb = i // tiles_per_batch
        seg = jnp.where(i % tiles_per_batch >= ctx_tiles, 1, 0)
        return (bb * 2 + seg, 0, which * per + j)

    return pl.BlockSpec((1, 1, tn), imap)


def _proj_res_kernel(a_ref, w_ref, x_ref, g_ref, o_ref):
    y = jnp.dot(a_ref[...], w_ref[...], preferred_element_type=F32)
    o_ref[...] = x_ref[...] + g_ref[0] * y


def _proj_residual(a2d, w_bf16, x2d, modt, which_gate, tm, ctx_tiles, tiles_per_batch):
    m, k = a2d.shape
    n = w_bf16.shape[1]
    tn = _pick(n, MM_TILE_N, LANES)
    return pl.pallas_call(
        _proj_res_kernel,
        out_shape=jax.ShapeDtypeStruct((m, n), F32),
        grid=(n // tn, m // tm),
        in_specs=[pl.BlockSpec((tm, k), lambda j, i: (i, 0)),
                  pl.BlockSpec((k, tn), lambda j, i: (0, j)),
                  pl.BlockSpec((tm, tn), lambda j, i: (i, j)),
                  _gate_spec(n, which_gate, ctx_tiles, tiles_per_batch, tn)],
        out_specs=pl.BlockSpec((tm, tn), lambda j, i: (i, j)),
        compiler_params=_cparams(("arbitrary", "arbitrary")),
        name="proj_residual",
    )(a2d, w_bf16, x2d, modt)


def _glu_res_kernel(a_ref, w1_ref, w2_ref, b1_ref, b2_ref, x_ref, g_ref, o_ref):
    a = a_ref[...]
    z1 = jnp.dot(a, w1_ref[...], preferred_element_type=F32) + b1_ref[...]
    z2 = jnp.dot(a, w2_ref[...], preferred_element_type=F32) + b2_ref[...]
    o_ref[...] = x_ref[...] + g_ref[0] * (z1 * jax.nn.sigmoid(z2))


def _glu_residual(a2d, w_bf16, b_glu, x2d, modt, which_gate, tm, ctx_tiles, tiles_per_batch):
    m, k = a2d.shape
    d = w_bf16.shape[1] // 2
    tn = _pick(d, MM_TILE_N, LANES)
    nj = d // tn
    b2 = b_glu.reshape(1, 2 * d)
    return pl.pallas_call(
        _glu_res_kernel,
        out_shape=jax.ShapeDtypeStruct((m, d), F32),
        grid=(nj, m // tm),
        in_specs=[pl.BlockSpec((tm, k), lambda j, i: (i, 0)),
                  pl.BlockSpec((k, tn), lambda j, i: (0, j)),
                  pl.BlockSpec((k, tn), lambda j, i: (0, j + nj)),
                  pl.BlockSpec((1, tn), lambda j, i: (0, j)),
                  pl.BlockSpec((1, tn), lambda j, i: (0, j + nj)),
                  pl.BlockSpec((tm, tn), lambda j, i: (i, j)),
                  _gate_spec(d, which_gate, ctx_tiles, tiles_per_batch, tn)],
        out_specs=pl.BlockSpec((tm, tn), lambda j, i: (i, j)),
        compiler_params=_cparams(("arbitrary", "arbitrary")),
        name="glu_residual",
    )(a2d, w_bf16, w_bf16, b2, b2, x2d, modt)


def _s5_tables(a_re, a_im, log_dt, b_re, b_im, c_re, c_im, d_skip):
    q, n_state, pch = SSM_CHUNK, a_re.shape[-1], b_re.shape[-1]
    g = a_re.shape[1]
    hp = lax.Precision.HIGHEST
    lam = lax.complex(jnp.minimum(a_re.astype(F32), DT_FLOOR_RE), a_im.astype(F32))
    lam_dt = lam * jnp.exp(log_dt.astype(F32))[..., None]
    b_bar = ((jnp.exp(lam_dt) - 1.0) / lam)[..., None] * lax.complex(b_re.astype(F32), b_im.astype(F32))
    c_mat = lax.complex(c_re.astype(F32), c_im.astype(F32))
    tau = jnp.arange(q + 1, dtype=F32)
    pw = jnp.exp(lam_dt[:, :, None, :] * tau[None, None, :, None])
    kern = jnp.einsum('dgpn,dgtn,dgnr->dgtpr', c_mat, pw[:, :, :q], b_bar, precision=hp).real
    t_i = jnp.arange(q)[:, None]
    s_i = jnp.arange(q)[None, :]
    kf = jnp.where((t_i >= s_i)[None, :, :, None, None], kern[0][:, jnp.clip(t_i - s_i, 0, q - 1)], 0.0)
    kb = jnp.where((s_i >= t_i)[None, :, :, None, None], kern[1][:, jnp.clip(s_i - t_i, 0, q - 1)], 0.0)
    dsk = d_skip.astype(F32).reshape(g, pch)
    diag = (jnp.eye(q)[None, :, :, None, None] * jnp.eye(pch)[None, None, None] * dsk[:, None, None, :, None])
    mt = (kf + kb + diag).transpose(0, 2, 4, 1, 3).reshape(g, q * pch, q * pch)
    wsf = pw[0][:, ::-1][:, 1:, :, None] * b_bar[0][:, None]
    wsb = pw[1][:, :q, :, None] * b_bar[1][:, None]
    wsf = wsf.transpose(0, 1, 3, 2).reshape(g, q * pch, n_state)
    wsb = wsb.transpose(0, 1, 3, 2).reshape(g, q * pch, n_state)
    w1 = jnp.concatenate([mt, wsf.real, wsb.real, wsf.imag, wsb.imag], axis=-1)
    cf = c_mat[0][:, None] * pw[0][:, 1:, None, :]
    cb = c_mat[1][:, None] * pw[1][:, ::-1][:, :q, None, :]
    cf = cf.transpose(0, 3, 1, 2).reshape(g, n_state, q * pch)
    cb = cb.transpose(0, 3, 1, 2).reshape(g, n_state, q * pch)
    w3 = jnp.concatenate([cf.real, cb.real, -cf.imag, -cb.imag], axis=1)
    aq = pw[:, :, q]
    a_r = jnp.concatenate([aq[0].real, aq[1].real], axis=-1)
    a_i = jnp.concatenate([aq[0].imag, aq[1].imag], axis=-1)
    return w1.astype(BF16), w3.astype(BF16), a_r, a_i


def _s5_in_kernel(u_ref, w_ref, y_ref, sr_ref, si_ref):
    width = u_ref.shape[-1]
    ns2 = sr_ref.shape[-1]
    for g in range(u_ref.shape[0]):
        r = jnp.dot(u_ref[g], w_ref[g], preferred_element_type=F32)
        y_ref[g] = r[:, :width]
        sr_ref[0, g] = r[:, width:width + ns2]
        si_ref[0, g] = r[:, width + ns2:]


def _s5_scan_kernel(sr_ref, si_ref, ar_ref, ai_ref, hr_ref, hi_ref, fr, fi, br, bi, *, n_chunks, ctx_chunks):
    gb = ar_ref.shape[0]
    a_r = ar_ref[...]
    a_i = ai_ref[...]
    lane = lax.broadcasted_iota(jnp.int32, a_r.shape, 1)
    fwd = lane < (a_r.shape[1] // 2)

    def rows(ref, c):
        return ref[pl.ds(c, gb, stride=n_chunks), :]

    def body(i, carry):
        h_r, h_i = carry
        cf = i
        cb = jnp.where(i < ctx_chunks, ctx_chunks - 1 - i, n_chunks + ctx_chunks - 1 - i)
        fr[pl.ds(cf, gb, stride=n_chunks), :] = h_r
        fi[pl.ds(cf, gb, stride=n_chunks), :] = h_i
        br[pl.ds(cb, gb, stride=n_chunks), :] = h_r
        bi[pl.ds(cb, gb, stride=n_chunks), :] = h_i
        s_r = jnp.where(fwd, rows(sr_ref, cf), rows(sr_ref, cb))
        s_i = jnp.where(fwd, rows(si_ref, cf), rows(si_ref, cb))
        return (a_r * h_r - a_i * h_i + s_r, a_r * h_i + a_i * h_r + s_i)

    zero = jnp.zeros(a_r.shape, F32)
    lax.fori_loop(0, n_chunks, body, (zero, zero))
    full = lax.broadcasted_iota(jnp.int32, hr_ref.shape, 1) < (a_r.shape[1] // 2)
    hr_ref[...] = jnp.where(full, fr[...], br[...])
    hi_ref[...] = jnp.where(full, fi[...], bi[...])


def _s5_out_kernel(y_ref, hr_ref, hi_ref, w_ref, o_ref):
    ns2 = hr_ref.shape[-1]
    for g in range(y_ref.shape[0]):
        y = (y_ref[g]
             + jnp.dot(hr_ref[0, g].astype(BF16), w_ref[g, :ns2], preferred_element_type=F32)
             + jnp.dot(hi_ref[0, g].astype(BF16), w_ref[g, ns2:], preferred_element_type=F32))
        o_ref[g] = jax.nn.gelu(y).astype(o_ref.dtype)


def _s5_mix(u, tables, ctx_len):
    w1, w3, a_r, a_i = tables
    b, p, d = u.shape
    q, pch = SSM_CHUNK, SSM_GROUP
    g = d // pch
    nc = p // q
    width = q * pch
    ns2 = 2 * SSM_STATE
    gb = _pick(g, SSM_GROUP_BLOCK, 8)
    ut = u.reshape(b, nc, q, g, pch).transpose(3, 0, 1, 2, 4).reshape(g, b * nc, width)

    y_in, s_r, s_i = pl.pallas_call(
        _s5_in_kernel,
        out_shape=(jax.ShapeDtypeStruct((g, b * nc, width), F32),
                   jax.ShapeDtypeStruct((b, g, nc, ns2), F32),
                   jax.ShapeDtypeStruct((b, g, nc, ns2), F32)),
        grid=(g // gb, b),
        in_specs=[pl.BlockSpec((gb, nc, width), lambda gi, bb: (gi, bb, 0)),
                  pl.BlockSpec((gb, width, width + 2 * ns2), lambda gi, bb: (gi, 0, 0))],
        out_specs=(pl.BlockSpec((gb, nc, width), lambda gi, bb: (gi, bb, 0)),
                   pl.BlockSpec((1, gb, nc, ns2), lambda gi, bb: (bb, gi, 0, 0)),
                   pl.BlockSpec((1, gb, nc, ns2), lambda gi, bb: (bb, gi, 0, 0))),
        compiler_params=_cparams(("arbitrary", "arbitrary")),
        name="s5_chunk_in",
    )(ut, w1)

    flat = lambda gi, bb: (bb * (g // gb) + gi, 0)
    h_r, h_i = pl.pallas_call(
        functools.partial(_s5_scan_kernel, n_chunks=nc, ctx_chunks=ctx_len // q),
        out_shape=(jax.ShapeDtypeStruct((b * g * nc, ns2), F32),) * 2,
        grid=(g // gb, b),
        in_specs=[pl.BlockSpec((gb * nc, ns2), flat), pl.BlockSpec((gb * nc, ns2), flat),
                  pl.BlockSpec((gb, ns2), lambda gi, bb: (gi, 0)),
                  pl.BlockSpec((gb, ns2), lambda gi, bb: (gi, 0))],
        out_specs=(pl.BlockSpec((gb * nc, ns2), flat),) * 2,
        scratch_shapes=[pltpu.VMEM((gb * nc, ns2), F32)] * 4,
        compiler_params=_cparams(("arbitrary", "arbitrary")),
        name="s5_chunk_scan",
    )(s_r.reshape(b * g * nc, ns2), s_i.reshape(b * g * nc, ns2), a_r, a_i)

    yt = pl.pallas_call(
        _s5_out_kernel,
        out_shape=jax.ShapeDtypeStruct((g, b * nc, width), BF16),
        grid=(g // gb, b),
        in_specs=[pl.BlockSpec((gb, nc, width), lambda gi, bb: (gi, bb, 0)),
                  pl.BlockSpec((1, gb, nc, ns2), lambda gi, bb: (bb, gi, 0, 0)),
                  pl.BlockSpec((1, gb, nc, ns2), lambda gi, bb: (bb, gi, 0, 0)),
                  pl.BlockSpec((gb, 2 * ns2, width), lambda gi, bb: (gi, 0, 0))],
        out_specs=pl.BlockSpec((gb, nc, width), lambda gi, bb: (gi, bb, 0)),
        compiler_params=_cparams(("arbitrary", "arbitrary")),
        name="s5_chunk_out",
    )(y_in, h_r.reshape(b, g, nc, ns2), h_i.reshape(b, g, nc, ns2), w3)
    return yt.reshape(g, b, nc, q, pch).transpose(1, 2, 3, 0, 4).reshape(b, p, d)


def _moe_schedule(idx, n_exp, tm, n_tiles):
    t = idx.shape[0]
    e_flat = idx.reshape(-1)
    n_asg = e_flat.shape[0]
    order = jnp.argsort(e_flat, stable=True).astype(jnp.int32)
    counts = jnp.sum((e_flat[:, None] == jnp.arange(n_exp)[None, :]).astype(jnp.int32), axis=0)
    padded = (counts + tm - 1) // tm * tm
    gstart = jnp.cumsum(padded) - padded
    cstart = jnp.cumsum(counts) - counts
    gend = gstart + padded
    n_active = (jnp.sum(padded) // tm).astype(jnp.int32)
    tile_start = jnp.arange(n_tiles, dtype=jnp.int32) * tm
    tile_e = jnp.minimum(jnp.searchsorted(gend, tile_start, side='right'), n_exp - 1).astype(jnp.int32)
    tile_e = jnp.where(jnp.arange(n_tiles) < n_active, tile_e, tile_e[jnp.maximum(n_active - 1, 0)])
    row_e = jnp.repeat(tile_e, tm)
    j = jnp.arange(n_tiles * tm, dtype=jnp.int32) - gstart[row_e].astype(jnp.int32)
    valid = (j < counts[row_e]) & (jnp.arange(n_tiles * tm) < n_active * tm)
    sidx = jnp.clip(cstart[row_e] + j, 0, n_asg - 1)
    row_token = jnp.where(valid, order[sidx] // TOP_K, 0).astype(jnp.int32)
    sorted_e = e_flat[order]
    dest = (gstart[sorted_e] + jnp.arange(n_asg) - cstart[sorted_e]).astype(jnp.int32)
    inv = jnp.argsort(order).astype(jnp.int32)
    pos = dest[inv].reshape(t, TOP_K)
    return tile_e, n_active.reshape(1), row_token, pos


def _moe_prep_kernel(w_ref, sel_ref, o_ref):
    sel = sel_ref[...]
    blk = sel.shape[0]
    for c in range(w_ref.shape[2] // blk):
        w = w_ref[0, :, c * blk:(c + 1) * blk].astype(BF16)
        o_ref[0, :, c * blk:(c + 1) * blk] = jnp.dot(w, sel, preferred_element_type=F32).astype(BF16)


def _moe_prep(w_gu):
    n_exp, d, n = w_gu.shape
    blk = 2 * LANES
    tk = _pick(d, 512, 8)
    src = jnp.arange(blk)
    sel = (jnp.arange(blk)[None, :] == (src // 2 + (src % 2) * LANES)[:, None]).astype(BF16)
    return pl.pallas_call(
        _moe_prep_kernel,
        out_shape=jax.ShapeDtypeStruct((n_exp, d, n), BF16),
        grid=(n_exp, d // tk),
        in_specs=[pl.BlockSpec((1, tk, n), lambda e, k: (e, k, 0)),
                  pl.BlockSpec((blk, blk), lambda e, k: (0, 0))],
        out_specs=pl.BlockSpec((1, tk, n), lambda e, k: (e, k, 0)),
        compiler_params=_cparams(("arbitrary", "arbitrary")),
        name="moe_weight_prep",
    )(w_gu, sel)


def _split_blocks(v):
    lead = v.shape[:-1]
    return v.reshape(lead + (-1, LANES, 2)).swapaxes(-1, -2).reshape(lead + (-1,))


def _moe_kernel(te_ref, na_ref, tok_ref, tokn_ref, x_hbm, w_ref, b_ref, wd_ref, bd_ref, o_ref, xbuf, sem,
                *, tm):
    i = pl.program_id(0)
    n_act = na_ref[0]

    def issue(t_ref, slot):
        def body(r, c):
            tok = t_ref[0, 0, r]
            pltpu.make_async_copy(x_hbm.at[pl.ds(tok, 1)], xbuf.at[slot, pl.ds(r, 1)], sem.at[slot]).start()
            return c
        lax.fori_loop(0, tm, body, 0, unroll=8)

    @pl.when(i == 0)
    def _():
        issue(tok_ref, 0)

    @pl.when(i + 1 < n_act)
    def _():
        issue(tokn_ref, (i + 1) % 2)

    @pl.when(i < n_act)
    def _():
        slot = i % 2

        def wbody(r, c):
            pltpu.make_async_copy(x_hbm.at[pl.ds(0, 1)], xbuf.at[slot, pl.ds(0, 1)], sem.at[slot]).wait()
            return c
        lax.fori_loop(0, tm, wbody, 0, unroll=8)

        lo, hi = _unpack_halves(xbuf[slot])
        half = lo.shape[1]
        h = (jnp.dot(lo.astype(BF16), w_ref[0, :half], preferred_element_type=F32)
             + jnp.dot(hi.astype(BF16), w_ref[0, half:], preferred_element_type=F32) + b_ref[0])
        acts = []
        for c in range(h.shape[1] // (2 * LANES)):
            gate = jnp.minimum(h[:, 2 * c * LANES:(2 * c + 1) * LANES], SWIGLU_LIMIT)
            up = jnp.clip(h[:, (2 * c + 1) * LANES:(2 * c + 2) * LANES], -SWIGLU_LIMIT, SWIGLU_LIMIT)
            acts.append(((up + 1.0) * (gate * jax.nn.sigmoid(SWIGLU_ALPHA * gate))).astype(BF16))
        act = jnp.concatenate(acts, axis=1)
        y = jnp.dot(act, wd_ref[0], preferred_element_type=F32) + bd_ref[0]
        o_ref[...] = _pack_halves(y)

    @pl.when(i >= n_act)
    def _():
        o_ref[...] = jnp.zeros(o_ref.shape, o_ref.dtype)


def _moe_experts(hp2d, row_token, tile_e, n_active, wp, bp, wd, bd, tm):
    half = hp2d.shape[1]
    n_exp, d, n = wp.shape
    n_tiles = row_token.shape[0] // tm
    tok3 = row_token.reshape(n_tiles, 1, tm)
    exp3 = lambda i, te, na: (te[i], 0, 0)
    return pl.pallas_call(
        functools.partial(_moe_kernel, tm=tm),
        out_shape=jax.ShapeDtypeStruct((n_tiles * tm, half), jnp.uint32),
        grid_spec=pltpu.PrefetchScalarGridSpec(
            num_scalar_prefetch=2,
            grid=(n_tiles,),
            in_specs=[pl.BlockSpec((1, 1, tm), lambda i, te, na: (i, 0, 0), memory_space=pltpu.SMEM),
                      pl.BlockSpec((1, 1, tm), lambda i, te, na: (jnp.minimum(i + 1, n_tiles - 1), 0, 0),
                                   memory_space=pltpu.SMEM),
                      pl.BlockSpec(memory_space=pl.ANY),
                      pl.BlockSpec((1, d, n), exp3), pl.BlockSpec((1, 1, n), exp3),
                      pl.BlockSpec((1, n // 2, d), exp3), pl.BlockSpec((1, 1, d), exp3)],
            out_specs=pl.BlockSpec((tm, half), lambda i, te, na: (i, 0)),
            scratch_shapes=[pltpu.VMEM((2, tm, half), jnp.uint32), pltpu.SemaphoreType.DMA((2,))]),
        compiler_params=_cparams(("arbitrary",)),
        name="moe_experts",
    )(tile_e, n_active, tok3, tok3, hp2d, wp, bp, wd, bd)


def _combine_kernel(pos_ref, posn_ref, gw_ref, x_ref, g_ref, y_hbm, o_ref, buf, sem, *, rows):
    i = pl.program_id(0)
    n = pl.num_programs(0)

    def issue(p_ref, slot):
        def body(r, c):
            for k in range(TOP_K):
                src = p_ref[0, 0, r * TOP_K + k]
                pltpu.make_async_copy(y_hbm.at[pl.ds(src, 1)], buf.at[slot, k, pl.ds(r, 1)],
                                      sem.at[slot]).start()
            return c
        lax.fori_loop(0, rows, body, 0, unroll=2)

    @pl.when(i == 0)
    def _():
        issue(pos_ref, 0)

    @pl.when(i + 1 < n)
    def _():
        issue(posn_ref, (i + 1) % 2)

    slot = i % 2

    def wbody(r, c):
        pltpu.make_async_copy(y_hbm.at[pl.ds(0, 1)], buf.at[slot, 0, pl.ds(0, 1)], sem.at[slot]).wait()
        return c
    lax.fori_loop(0, rows * TOP_K, wbody, 0, unroll=8)

    gw = gw_ref[...]
    half = buf.shape[-1]
    acc_lo = acc_hi = None
    for k in range(TOP_K):
        lo, hi = _unpack_halves(buf[slot, k])
        w = gw[:, k:k + 1]
        acc_lo = w * lo if k == 0 else acc_lo + w * lo
        acc_hi = w * hi if k == 0 else acc_hi + w * hi
    g = g_ref[0]
    o_ref[:, :half] = x_ref[:, :half] + g[:, :half] * acc_lo
    o_ref[:, half:] = x_ref[:, half:] + g[:, half:] * acc_hi


def _moe_combine(y_sorted, pos, gw2d, x2d, modt, tm_row, ctx_tiles, tiles_per_batch):
    t, d = x2d.shape
    rows = _pick(tm_row, COMBINE_ROWS, 8)
    n_steps = t // rows
    sub = tm_row // rows
    pos3 = pos.reshape(n_steps, 1, rows * TOP_K)

    def gmap(i):
        tile = i // sub
        bb = tile // tiles_per_batch
        seg = jnp.where(tile % tiles_per_batch >= ctx_tiles, 1, 0)
        return (bb * 2 + seg, 0, 5)

    return pl.pallas_call(
        functools.partial(_combine_kernel, rows=rows),
        out_shape=jax.ShapeDtypeStruct((t, d), F32),
        grid=(n_steps,),
        in_specs=[pl.BlockSpec((1, 1, rows * TOP_K), lambda i: (i, 0, 0), memory_space=pltpu.SMEM),
                  pl.BlockSpec((1, 1, rows * TOP_K), lambda i: (jnp.minimum(i + 1, n_steps - 1), 0, 0),
                               memory_space=pltpu.SMEM),
                  pl.BlockSpec((rows, LANES), lambda i: (i, 0)),
                  pl.BlockSpec((rows, d), lambda i: (i, 0)),
                  pl.BlockSpec((1, 1, d), gmap),
                  pl.BlockSpec(memory_space=pl.ANY)],
        out_specs=pl.BlockSpec((rows, d), lambda i: (i, 0)),
        scratch_shapes=[pltpu.VMEM((2, TOP_K, rows, d // 2), jnp.uint32), pltpu.SemaphoreType.DMA((2,))],
        compiler_params=_cparams(("arbitrary",)),
        name="moe_combine",
    )(pos3, pos3, gw2d, x2d, modt, y_sorted)


def _rope_tables(seq, ctx_len):
    n_rows = seq // GRID_W
    axis_rot = HEAD_DIM // 2
    rows = jnp.repeat(jnp.arange(n_rows, dtype=F32), GRID_W)
    cols = jnp.tile(jnp.arange(GRID_W, dtype=F32), n_rows)
    inv_freq = ROPE_BASE ** (-jnp.arange(0, axis_rot, 2, dtype=F32) / axis_rot)
    ang_r = rows[:, None] * inv_freq
    ang_c = cols[:, None] * inv_freq
    ang = jnp.concatenate([ang_r, ang_r, ang_c, ang_c], axis=-1)
    sign = jnp.tile(jnp.concatenate([-jnp.ones(axis_rot // 2, F32), jnp.ones(axis_rot // 2, F32)]), 2)
    cos = jnp.concatenate([jnp.ones((ctx_len, HEAD_DIM), F32), jnp.cos(ang)], axis=0)
    sin = jnp.concatenate([jnp.zeros((ctx_len, HEAD_DIM), F32), jnp.sin(ang) * sign], axis=0)
    return cos, sin


def kernel(x, c, ctx, c_ctx, w_mod, b_mod, norm_mix, norm_ffn, w_router, b_router, w_gate_up, b_gate_up, w_down, b_down, attn_w_qkv, attn_w_o, attn_q_gain, attn_k_gain, attn_sinks, ssm_a_re, ssm_a_im, ssm_log_dt, ssm_b_re, ssm_b_im, ssm_c_re, ssm_c_im, ssm_d, ssm_w_glu, ssm_b_glu):
    b, seq, d = x.shape
    ctx_len = ctx.shape[1]
    depth = w_mod.shape[0]
    p = ctx_len + seq
    t = b * p
    n_exp = w_router.shape[-1]
    n_q = d // HEAD_DIM
    n_kv = (attn_w_qkv.shape[-1] // HEAD_DIM - n_q) // 2
    assert ctx_len % ATT_BLOCK == 0 and seq % ATT_BLOCK == 0 and seq % GRID_W == 0
    assert d % (2 * LANES) == 0 and ctx_len % SSM_CHUNK == 0 and seq % SSM_CHUNK == 0

    tm = _pick(math.gcd(p, ctx_len), ROW_TILE)
    mm_tm = _pick(math.gcd(p, ctx_len), MM_TILE_M)
    tiles_pb = p // mm_tm
    ctx_tiles = ctx_len // mm_tm

    pad = (-(b + 1)) % 8
    cvec = jnp.concatenate([c, c_ctx[None, :], jnp.zeros((pad, d), F32)], axis=0)
    mod_all = _mod_all(cvec, w_mod, b_mod)

    cos_t, sin_t = _rope_tables(seq, ctx_len)
    xs = jnp.concatenate([ctx, x], axis=1)

    tm_moe = min(MOE_TILE, _pick(t * TOP_K, MOE_TILE, 8))
    n_tiles = (t * TOP_K) // tm_moe + n_exp
    scale = HEAD_DIM ** -0.5

    for i in range(depth):
        j = i // 2
        lat = mod_all[i, :b]
        cx = jnp.broadcast_to(mod_all[i, b][None], lat.shape)
        modt = jnp.stack([cx, lat], axis=1).reshape(b * 2, 1, 6 * d)

        u = _norm_mod(xs, norm_mix[i], modt, 0, ctx_len)
        x2d = xs.reshape(t, d)
        if i % 2 == 0:
            gain = jnp.concatenate([jnp.tile(attn_q_gain[j] * scale, n_q), jnp.tile(attn_k_gain[j], n_kv),
                                    jnp.ones((n_kv * HEAD_DIM,), F32)]).reshape(1, -1)
            qkv = _qkv_proj(u.reshape(t, d), attn_w_qkv[j].astype(BF16), gain, cos_t, sin_t,
                            (n_q + n_kv) * HEAD_DIM, (mm_tm, tiles_pb))
            o = _attention(qkv.reshape(b, p, -1), attn_sinks[j].astype(F32), n_q, n_kv, ctx_len)
            x2d = _proj_residual(o.reshape(t, d), attn_w_o[j].astype(BF16), x2d, modt, 2,
                                 mm_tm, ctx_tiles, tiles_pb)
        else:
            tables = _s5_tables(ssm_a_re[j], ssm_a_im[j], ssm_log_dt[j], ssm_b_re[j], ssm_b_im[j],
                                ssm_c_re[j], ssm_c_im[j], ssm_d[j])
            gy = _s5_mix(u, tables, ctx_len)
            x2d = _glu_residual(gy.reshape(t, d), ssm_w_glu[j].astype(BF16), ssm_b_glu[j], x2d, modt, 2,
                                mm_tm, ctx_tiles, tiles_pb)

        hp, idx, gw = _norm_router(x2d.reshape(b, p, d), norm_ffn[i], modt, w_router[i], b_router[i], ctx_len)
        idx2d = idx.reshape(t, LANES)[:, :TOP_K]
        tile_e, n_active, row_token, pos = _moe_schedule(idx2d, n_exp, tm_moe, n_tiles)
        y_sorted = _moe_experts(hp.reshape(t, d // 2), row_token, tile_e, n_active,
                                _moe_prep(w_gate_up[i]), _split_blocks(b_gate_up[i]).reshape(n_exp, 1, -1),
                                w_down[i].astype(BF16), b_down[i].reshape(n_exp, 1, d), tm_moe)
        x2d = _moe_combine(y_sorted, pos, gw.reshape(t, LANES), x2d, modt, tm, ctx_len // tm, p // tm)
        xs = x2d.reshape(b, p, d)

    return xs[:, ctx_len:]
```

```python
import functools
import math

import jax
import jax.numpy as jnp
from jax import lax
from jax.experimental import pallas as pl
from jax.experimental.pallas import tpu as pltpu

F32 = jnp.float32
BF16 = jnp.bfloat16

HEAD_DIM = 128
ATT_BLOCK = 128
GRID_W = 64
ROPE_BASE = 10000.0
SSM_GROUP = 16
SSM_STATE = 64
SSM_CHUNK = 16
TOP_K = 4
SWIGLU_LIMIT = 7.0
SWIGLU_ALPHA = 1.702
EPS = 1e-6
NEG_INF = -1e30
DT_FLOOR_RE = -1e-4

LANES = 128
VMEM_LIMIT = 56 * 1024 * 1024

ROW_TILE = 256
MM_TILE_M = 512
PROJ_TILE_N = 2048
GLU_TILE_N = 1024
MOE_TILE = 512
COMBINE_ROWS = 128
SSM_GROUP_BLOCK = 8


def _cparams(sem):
    return pltpu.CompilerParams(dimension_semantics=sem, vmem_limit_bytes=VMEM_LIMIT)


def _pick(n, pref, mult=8):
    t = min(n, pref)
    while n % t or t % mult:
        t -= 1
    return t


def _mod_kernel(c_ref, w_ref, b_ref, o_ref):
    c = c_ref[...]
    s = c * jax.nn.sigmoid(c)
    o_ref[0] = jnp.dot(s.astype(BF16), w_ref[0].astype(BF16), preferred_element_type=F32) + b_ref[0]


def _mod_all(cvec, w_mod, b_mod):
    depth, d, n = w_mod.shape
    tn = _pick(n, 1024, LANES)
    return pl.pallas_call(
        _mod_kernel,
        out_shape=jax.ShapeDtypeStruct((depth, cvec.shape[0], n), F32),
        grid=(depth, n // tn),
        in_specs=[pl.BlockSpec(cvec.shape, lambda l, j: (0, 0)),
                  pl.BlockSpec((1, d, tn), lambda l, j: (l, 0, j)),
                  pl.BlockSpec((1, 1, tn), lambda l, j: (l, 0, j))],
        out_specs=pl.BlockSpec((1, cvec.shape[0], tn), lambda l, j: (l, 0, j)),
        compiler_params=_cparams(("arbitrary", "arbitrary")),
        name="adaln_rows",
    )(cvec, w_mod, b_mod.reshape(depth, 1, n))


def _normed(x, w, sh, sc):
    r = lax.rsqrt(jnp.mean(x * x, axis=-1, keepdims=True) + EPS)
    return (x * r * w) * (1.0 + sc) + sh


def _norm_mod_kernel(x_ref, w_ref, sh_ref, sc_ref, o_ref):
    o_ref[0] = _normed(x_ref[0], w_ref[...], sh_ref[0], sc_ref[0]).astype(o_ref.dtype)


def _mod_spec(d, which, ctx_tiles):
    return pl.BlockSpec((1, 1, d), lambda b, i: (b * 2 + jnp.where(i >= ctx_tiles, 1, 0), 0, which))


def _norm_mod(x, w, modt, which_shift, ctx_len):
    b, p, d = x.shape
    tm = _pick(math.gcd(p, ctx_len), ROW_TILE)
    ct = ctx_len // tm
    return pl.pallas_call(
        _norm_mod_kernel,
        out_shape=jax.ShapeDtypeStruct((b, p, d), BF16),
        grid=(b, p // tm),
        in_specs=[pl.BlockSpec((1, tm, d), lambda bb, i: (bb, i, 0)),
                  pl.BlockSpec((1, d), lambda bb, i: (0, 0)),
                  _mod_spec(d, which_shift, ct), _mod_spec(d, which_shift + 1, ct)],
        out_specs=pl.BlockSpec((1, tm, d), lambda bb, i: (bb, i, 0)),
        compiler_params=_cparams(("arbitrary", "arbitrary")),
        name="norm_mod",
    )(x, w.reshape(1, d), modt, modt)


def _pack_halves(h):
    half = h.shape[-1] // 2
    bits = lax.bitcast_convert_type(h.astype(BF16).astype(F32), jnp.uint32)
    return (bits[:, :half] >> 16) | (bits[:, half:] & jnp.uint32(0xFFFF0000))


def _unpack_halves(xp):
    lo = lax.bitcast_convert_type(xp << 16, F32)
    hi = lax.bitcast_convert_type(xp & jnp.uint32(0xFFFF0000), F32)
    return lo, hi


def _norm_router_kernel(x_ref, w_ref, sh_ref, sc_ref, wr_ref, br_ref, hp_ref, idx_ref, gw_ref):
    h = _normed(x_ref[0], w_ref[...], sh_ref[0], sc_ref[0])
    hp_ref[0] = _pack_halves(h)
    logits = jnp.dot(h, wr_ref[...], precision=lax.Precision.HIGHEST,
                     preferred_element_type=F32) + br_ref[...]
    n_exp = logits.shape[-1]
    lane = lax.broadcasted_iota(jnp.int32, logits.shape, 1)
    out_lane = lax.broadcasted_iota(jnp.int32, idx_ref.shape[1:], 1)
    vals, idxs = [], []
    rest = logits
    for _ in range(TOP_K):
        m = jnp.max(rest, axis=-1, keepdims=True)
        idx = jnp.min(jnp.where(rest == m, lane, n_exp), axis=-1, keepdims=True)
        vals.append(m)
        idxs.append(idx)
        rest = jnp.where(lane == idx, -jnp.inf, rest)
    exps = [jnp.exp(v - vals[0]) for v in vals]
    tot = exps[0]
    for e in exps[1:]:
        tot = tot + e
    idx_out = jnp.zeros(idx_ref.shape[1:], jnp.int32)
    gw_out = jnp.zeros(gw_ref.shape[1:], F32)
    for k in range(TOP_K):
        idx_out = jnp.where(out_lane == k, idxs[k], idx_out)
        gw_out = jnp.where(out_lane == k, exps[k] / tot, gw_out)
    idx_ref[0] = idx_out
    gw_ref[0] = gw_out


def _norm_router(x, w, modt, w_router, b_router, ctx_len):
    b, p, d = x.shape
    n_exp = w_router.shape[-1]
    tm = _pick(math.gcd(p, ctx_len), ROW_TILE)
    ct = ctx_len // tm
    row = lambda bb, i: (bb, i, 0)
    return pl.pallas_call(
        _norm_router_kernel,
        out_shape=(jax.ShapeDtypeStruct((b, p, d // 2), jnp.uint32),
                   jax.ShapeDtypeStruct((b, p, LANES), jnp.int32),
                   jax.ShapeDtypeStruct((b, p, LANES), F32)),
        grid=(b, p // tm),
        in_specs=[pl.BlockSpec((1, tm, d), row),
                  pl.BlockSpec((1, d), lambda bb, i: (0, 0)),
                  _mod_spec(d, 3, ct), _mod_spec(d, 4, ct),
                  pl.BlockSpec((d, n_exp), lambda bb, i: (0, 0)),
                  pl.BlockSpec((1, n_exp), lambda bb, i: (0, 0))],
        out_specs=(pl.BlockSpec((1, tm, d // 2), row),
                   pl.BlockSpec((1, tm, LANES), row),
                   pl.BlockSpec((1, tm, LANES), row)),
        compiler_params=_cparams(("arbitrary", "arbitrary")),
        name="norm_router",
    )(x, w.reshape(1, d), modt, modt, w_router, b_router.reshape(1, n_exp))


def _qkv_kernel(x_ref, nw_ref, sh_ref, sc_ref, w_ref, g_ref, cos_ref, sin_ref, o_ref, *, n_norm_heads):
    u = _normed(x_ref[...], nw_ref[...], sh_ref[0], sc_ref[0]).astype(BF16)
    acc = jnp.dot(u, w_ref[...], preferred_element_type=F32)
    cos = cos_ref[...]
    sin = sin_ref[...]
    lane = lax.broadcasted_iota(jnp.int32, cos.shape, 1)
    first = (lane // (HEAD_DIM // 4)) % 2 == 0
    for hh in range(acc.shape[1] // HEAD_DIM):
        sl = slice(hh * HEAD_DIM, (hh + 1) * HEAD_DIM)
        xh = acc[:, sl]
        if hh < n_norm_heads:
            r = lax.rsqrt(jnp.mean(xh * xh, axis=-1, keepdims=True) + EPS)
            y = xh * r * g_ref[:, sl]
            partner = jnp.where(first, pltpu.roll(y, HEAD_DIM - HEAD_DIM // 4, 1),
                                pltpu.roll(y, HEAD_DIM // 4, 1))
            xh = y * cos + partner * sin
        o_ref[:, sl] = xh.astype(o_ref.dtype)


def _row_mod_spec(d, which, ctx_tiles, tiles_per_batch):
    def imap(i):
        seg = jnp.where(i % tiles_per_batch >= ctx_tiles, 1, 0)
        return ((i // tiles_per_batch) * 2 + seg, 0, which)

    return pl.BlockSpec((1, 1, d), imap)


def _qkv_proj(x2d, norm_w, modt, w_bf16, gain_row, cos_t, sin_t, n_norm_heads, tm, ctx_tiles, tiles_per_batch):
    m, k = x2d.shape
    n = w_bf16.shape[1]
    pos = lambda i: (i % tiles_per_batch, 0)
    return pl.pallas_call(
        functools.partial(_qkv_kernel, n_norm_heads=n_norm_heads),
        out_shape=jax.ShapeDtypeStruct((m, n), BF16),
        grid=(m // tm,),
        in_specs=[pl.BlockSpec((tm, k), lambda i: (i, 0)),
                  pl.BlockSpec((1, k), lambda i: (0, 0)),
                  _row_mod_spec(k, 0, ctx_tiles, tiles_per_batch),
                  _row_mod_spec(k, 1, ctx_tiles, tiles_per_batch),
                  pl.BlockSpec((k, n), lambda i: (0, 0)),
                  pl.BlockSpec((1, n), lambda i: (0, 0)),
                  pl.BlockSpec((tm, HEAD_DIM), pos),
                  pl.BlockSpec((tm, HEAD_DIM), pos)],
        out_specs=pl.BlockSpec((tm, n), lambda i: (i, 0)),
        compiler_params=_cparams(("arbitrary",)),
        name="qkv_proj",
    )(x2d, norm_w.reshape(1, k), modt, modt, w_bf16, gain_row, cos_t, sin_t)


def _attn_kernel(sink_ref, q_ref, kc_ref, vc_ref, kp_ref, ks_ref, kn_ref, vp_ref, vs_ref, vn_ref,
                 o_ref, *, q_per_kv, ctx_blocks, n_blocks):
    h = pl.program_id(1)
    i = pl.program_id(2)
    blk = ATT_BLOCK
    q = q_ref[0]
    qs = jnp.concatenate([q[:, g * HEAD_DIM:(g + 1) * HEAD_DIM] for g in range(q_per_kv)], axis=0)
    rows = qs.shape[0]
    dn = (((1,), (1,)), ((), ()))
    k_loc = jnp.concatenate([kp_ref[0], ks_ref[0], kn_ref[0]], axis=0)
    v_loc = jnp.concatenate([vp_ref[0], vs_ref[0], vn_ref[0]], axis=0)
    s_loc = lax.dot_general(qs, k_loc, dn, preferred_element_type=F32)
    s_ctx = lax.dot_general(qs, kc_ref[0], dn, preferred_element_type=F32)

    qi = lax.broadcasted_iota(jnp.int32, s_loc.shape, 0) % blk
    kj = lax.broadcasted_iota(jnp.int32, s_loc.shape, 1)
    is_lat = i >= ctx_blocks
    ok_prev = (kj < blk) & (kj >= qi) & (i > ctx_blocks)
    ok_self = (kj >= blk) & (kj < 2 * blk)
    ok_next = (kj >= 2 * blk) & (kj - 2 * blk <= qi) & (i < n_blocks - 1)
    valid = (ok_prev | ok_self | ok_next) & is_lat
    s_loc = jnp.where(valid, s_loc, NEG_INF)

    row_head = lax.broadcasted_iota(jnp.int32, (rows, 1), 0) // blk
    sink = jnp.zeros((rows, 1), F32)
    for g in range(q_per_kv):
        sink = jnp.where(row_head == g, sink_ref[h * q_per_kv + g], sink)

    m = jnp.maximum(jnp.maximum(jnp.max(s_loc, axis=-1, keepdims=True),
                                jnp.max(s_ctx, axis=-1, keepdims=True)), sink)
    p_loc = jnp.exp(s_loc - m)
    p_ctx = jnp.exp(s_ctx - m)
    denom = (jnp.sum(p_loc, axis=-1, keepdims=True) + jnp.sum(p_ctx, axis=-1, keepdims=True)
             + jnp.exp(sink - m))
    o = (jnp.dot(p_loc.astype(BF16), v_loc, preferred_element_type=F32)
         + jnp.dot(p_ctx.astype(BF16), vc_ref[0], preferred_element_type=F32)) / denom
    o = o.astype(o_ref.dtype)
    for g in range(q_per_kv):
        o_ref[0, :, g * HEAD_DIM:(g + 1) * HEAD_DIM] = o[g * blk:(g + 1) * blk]


def _attention(qkv, sinks, n_q, n_kv, ctx_len):
    b, p, _ = qkv.shape
    blk = ATT_BLOCK
    nb = p // blk
    cb = ctx_len // blk
    qpk = n_q // n_kv
    qw = qpk * HEAD_DIM
    kcol = n_q
    vcol = n_q + n_kv

    def loc(col0, off):
        return pl.BlockSpec(
            (1, blk, HEAD_DIM),
            lambda bb, h, i, s: (bb, jnp.clip(i + off, cb, nb - 1), col0 + h))

    return pl.pallas_call(
        functools.partial(_attn_kernel, q_per_kv=qpk, ctx_blocks=cb, n_blocks=nb),
        out_shape=jax.ShapeDtypeStruct((b, p, n_q * HEAD_DIM), BF16),
        grid_spec=pltpu.PrefetchScalarGridSpec(
            num_scalar_prefetch=1,
            grid=(b, n_kv, nb),
            in_specs=[pl.BlockSpec((1, blk, qw), lambda bb, h, i, s: (bb, i, h)),
                      pl.BlockSpec((1, ctx_len, HEAD_DIM), lambda bb, h, i, s: (bb, 0, kcol + h)),
                      pl.BlockSpec((1, ctx_len, HEAD_DIM), lambda bb, h, i, s: (bb, 0, vcol + h)),
                      loc(kcol, -1), loc(kcol, 0), loc(kcol, 1),
                      loc(vcol, -1), loc(vcol, 0), loc(vcol, 1)],
            out_specs=pl.BlockSpec((1, blk, qw), lambda bb, h, i, s: (bb, i, h))),
        compiler_params=_cparams(("arbitrary", "arbitrary", "arbitrary")),
        name="window_attn",
    )(sinks, qkv, qkv, qkv, qkv, qkv, qkv, qkv, qkv, qkv)


def _gate_spec(d, which, ctx_tiles, tiles_per_batch, tn):
    per = d // tn

    def imap(j, i):
        bb = i // tiles_per_batch
        seg = jnp.where(i % tiles_per_batch >= ctx_tiles, 1, 0)
        return (bb * 2 + seg, 0, which * per + j)

    return pl.BlockSpec((1, 1, tn), imap)


def _proj_res_kernel(a_ref, w_ref, x_ref, g_ref, o_ref):
    y = jnp.dot(a_ref[...], w_ref[...], preferred_element_type=F32)
    o_ref[...] = x_ref[...] + g_ref[0] * y


def _proj_residual(a2d, w_bf16, x2d, modt, which_gate, tm, ctx_tiles, tiles_per_batch):
    m, k = a2d.shape
    n = w_bf16.shape[1]
    tn = _pick(n, PROJ_TILE_N, LANES)
    return pl.pallas_call(
        _proj_res_kernel,
        out_shape=jax.ShapeDtypeStruct((m, n), F32),
        grid=(n // tn, m // tm),
        in_specs=[pl.BlockSpec((tm, k), lambda j, i: (i, 0)),
                  pl.BlockSpec((k, tn), lambda j, i: (0, j)),
                  pl.BlockSpec((tm, tn), lambda j, i: (i, j)),
                  _gate_spec(n, which_gate, ctx_tiles, tiles_per_batch, tn)],
        out_specs=pl.BlockSpec((tm, tn), lambda j, i: (i, j)),
        compiler_params=_cparams(("arbitrary", "arbitrary")),
        name="proj_residual",
    )(a2d, w_bf16, x2d, modt)


def _glu_res_kernel(a_ref, w1_ref, w2_ref, b1_ref, b2_ref, x_ref, g_ref, o_ref):
    a = a_ref[...]
    z1 = jnp.dot(a, w1_ref[...], preferred_element_type=F32) + b1_ref[...]
    z2 = jnp.dot(a, w2_ref[...], preferred_element_type=F32) + b2_ref[...]
    o_ref[...] = x_ref[...] + g_ref[0] * (z1 * jax.nn.sigmoid(z2))


def _glu_residual(a2d, w_bf16, b_glu, x2d, modt, which_gate, tm, ctx_tiles, tiles_per_batch):
    m, k = a2d.shape
    d = w_bf16.shape[1] // 2
    tn = _pick(d, GLU_TILE_N, LANES)
    nj = d // tn
    b2 = b_glu.reshape(1, 2 * d)
    return pl.pallas_call(
        _glu_res_kernel,
        out_shape=jax.ShapeDtypeStruct((m, d), F32),
        grid=(nj, m // tm),
        in_specs=[pl.BlockSpec((tm, k), lambda j, i: (i, 0)),
                  pl.BlockSpec((k, tn), lambda j, i: (0, j)),
                  pl.BlockSpec((k, tn), lambda j, i: (0, j + nj)),
                  pl.BlockSpec((1, tn), lambda j, i: (0, j)),
                  pl.BlockSpec((1, tn), lambda j, i: (0, j + nj)),
                  pl.BlockSpec((tm, tn), lambda j, i: (i, j)),
                  _gate_spec(d, which_gate, ctx_tiles, tiles_per_batch, tn)],
        out_specs=pl.BlockSpec((tm, tn), lambda j, i: (i, j)),
        compiler_params=_cparams(("arbitrary", "arbitrary")),
        name="glu_residual",
    )(a2d, w_bf16, w_bf16, b2, b2, x2d, modt)


def _s5_tables(a_re, a_im, log_dt, b_re, b_im, c_re, c_im, d_skip):
    q, n_state, pch = SSM_CHUNK, a_re.shape[-1], b_re.shape[-1]
    g = a_re.shape[1]
    hp = lax.Precision.HIGHEST
    lam = lax.complex(jnp.minimum(a_re.astype(F32), DT_FLOOR_RE), a_im.astype(F32))
    lam_dt = lam * jnp.exp(log_dt.astype(F32))[..., None]
    b_bar = ((jnp.exp(lam_dt) - 1.0) / lam)[..., None] * lax.complex(b_re.astype(F32), b_im.astype(F32))
    c_mat = lax.complex(c_re.astype(F32), c_im.astype(F32))
    tau = jnp.arange(q + 1, dtype=F32)
    pw = jnp.exp(lam_dt[:, :, None, :] * tau[None, None, :, None])
    kern = jnp.einsum('dgpn,dgtn,dgnr->dgtpr', c_mat, pw[:, :, :q], b_bar, precision=hp).real
    t_i = jnp.arange(q)[:, None]
    s_i = jnp.arange(q)[None, :]
    kf = jnp.where((t_i >= s_i)[None, :, :, None, None], kern[0][:, jnp.clip(t_i - s_i, 0, q - 1)], 0.0)
    kb = jnp.where((s_i >= t_i)[None, :, :, None, None], kern[1][:, jnp.clip(s_i - t_i, 0, q - 1)], 0.0)
    dsk = d_skip.astype(F32).reshape(g, pch)
    diag = (jnp.eye(q)[None, :, :, None, None] * jnp.eye(pch)[None, None, None] * dsk[:, None, None, :, None])
    mt = (kf + kb + diag).transpose(0, 2, 4, 1, 3).reshape(g, q * pch, q * pch)
    wsf = pw[0][:, ::-1][:, 1:, :, None] * b_bar[0][:, None]
    wsb = pw[1][:, :q, :, None] * b_bar[1][:, None]
    wsf = wsf.transpose(0, 1, 3, 2).reshape(g, q * pch, n_state)
    wsb = wsb.transpose(0, 1, 3, 2).reshape(g, q * pch, n_state)
    w1 = jnp.concatenate([mt, wsf.real, wsb.real, wsf.imag, wsb.imag], axis=-1)
    cf = c_mat[0][:, None] * pw[0][:, 1:, None, :]
    cb = c_mat[1][:, None] * pw[1][:, ::-1][:, :q, None, :]
    cf = cf.transpose(0, 3, 1, 2).reshape(g, n_state, q * pch)
    cb = cb.transpose(0, 3, 1, 2).reshape(g, n_state, q * pch)
    w3 = jnp.concatenate([cf.real, cb.real, -cf.imag, -cb.imag], axis=1)
    aq = pw[:, :, q]
    a_r = jnp.concatenate([aq[0].real, aq[1].real], axis=-1)
    a_i = jnp.concatenate([aq[0].imag, aq[1].imag], axis=-1)
    return w1.astype(BF16), w3.astype(BF16), a_r, a_i


def _s5_in_kernel(u_ref, w_ref, y_ref, sr_ref, si_ref):
    width = u_ref.shape[-1]
    ns2 = sr_ref.shape[-1]
    for g in range(u_ref.shape[0]):
        r = jnp.dot(u_ref[g], w_ref[g], preferred_element_type=F32)
        y_ref[g] = r[:, :width]
        sr_ref[0, g] = r[:, width:width + ns2]
        si_ref[0, g] = r[:, width + ns2:]


def _s5_scan_kernel(sr_ref, si_ref, ar_ref, ai_ref, hr_ref, hi_ref, fr, fi, br, bi, *, n_chunks, ctx_chunks):
    gb = ar_ref.shape[0]
    a_r = ar_ref[...]
    a_i = ai_ref[...]
    lane = lax.broadcasted_iota(jnp.int32, a_r.shape, 1)
    fwd = lane < (a_r.shape[1] // 2)

    def rows(ref, c):
        return ref[pl.ds(c, gb, stride=n_chunks), :]

    def body(i, carry):
        h_r, h_i = carry
        cf = i
        cb = jnp.where(i < ctx_chunks, ctx_chunks - 1 - i, n_chunks + ctx_chunks - 1 - i)
        fr[pl.ds(cf, gb, stride=n_chunks), :] = h_r
        fi[pl.ds(cf, gb, stride=n_chunks), :] = h_i
        br[pl.ds(cb, gb, stride=n_chunks), :] = h_r
        bi[pl.ds(cb, gb, stride=n_chunks), :] = h_i
        s_r = jnp.where(fwd, rows(sr_ref, cf), rows(sr_ref, cb))
        s_i = jnp.where(fwd, rows(si_ref, cf), rows(si_ref, cb))
        return (a_r * h_r - a_i * h_i + s_r, a_r * h_i + a_i * h_r + s_i)

    zero = jnp.zeros(a_r.shape, F32)
    lax.fori_loop(0, n_chunks, body, (zero, zero))
    full = lax.broadcasted_iota(jnp.int32, hr_ref.shape, 1) < (a_r.shape[1] // 2)
    hr_ref[...] = jnp.where(full, fr[...], br[...])
    hi_ref[...] = jnp.where(full, fi[...], bi[...])


def _s5_out_kernel(y_ref, hr_ref, hi_ref, w_ref, o_ref):
    ns2 = hr_ref.shape[-1]
    for g in range(y_ref.shape[0]):
        y = (y_ref[g]
             + jnp.dot(hr_ref[0, g].astype(BF16), w_ref[g, :ns2], preferred_element_type=F32)
             + jnp.dot(hi_ref[0, g].astype(BF16), w_ref[g, ns2:], preferred_element_type=F32))
        o_ref[g] = jax.nn.gelu(y).astype(o_ref.dtype)


def _s5_mix(u, tables, ctx_len):
    w1, w3, a_r, a_i = tables
    b, p, d = u.shape
    q, pch = SSM_CHUNK, SSM_GROUP
    g = d // pch
    nc = p // q
    width = q * pch
    ns2 = 2 * SSM_STATE
    gb = _pick(g, SSM_GROUP_BLOCK, 8)
    ut = u.reshape(b, nc, q, g, pch).transpose(3, 0, 1, 2, 4).reshape(g, b * nc, width)

    y_in, s_r, s_i = pl.pallas_call(
        _s5_in_kernel,
        out_shape=(jax.ShapeDtypeStruct((g, b * nc, width), F32),
                   jax.ShapeDtypeStruct((b, g, nc, ns2), F32),
                   jax.ShapeDtypeStruct((b, g, nc, ns2), F32)),
        grid=(g // gb, b),
        in_specs=[pl.BlockSpec((gb, nc, width), lambda gi, bb: (gi, bb, 0)),
                  pl.BlockSpec((gb, width, width + 2 * ns2), lambda gi, bb: (gi, 0, 0))],
        out_specs=(pl.BlockSpec((gb, nc, width), lambda gi, bb: (gi, bb, 0)),
                   pl.BlockSpec((1, gb, nc, ns2), lambda gi, bb: (bb, gi, 0, 0)),
                   pl.BlockSpec((1, gb, nc, ns2), lambda gi, bb: (bb, gi, 0, 0))),
        compiler_params=_cparams(("arbitrary", "arbitrary")),
        name="s5_chunk_in",
    )(ut, w1)

    flat = lambda gi, bb: (bb * (g // gb) + gi, 0)
    h_r, h_i = pl.pallas_call(
        functools.partial(_s5_scan_kernel, n_chunks=nc, ctx_chunks=ctx_len // q),
        out_shape=(jax.ShapeDtypeStruct((b * g * nc, ns2), F32),) * 2,
        grid=(g // gb, b),
        in_specs=[pl.BlockSpec((gb * nc, ns2), flat), pl.BlockSpec((gb * nc, ns2), flat),
                  pl.BlockSpec((gb, ns2), lambda gi, bb: (gi, 0)),
                  pl.BlockSpec((gb, ns2), lambda gi, bb: (gi, 0))],
        out_specs=(pl.BlockSpec((gb * nc, ns2), flat),) * 2,
        scratch_shapes=[pltpu.VMEM((gb * nc, ns2), F32)] * 4,
        compiler_params=_cparams(("arbitrary", "arbitrary")),
        name="s5_chunk_scan",
    )(s_r.reshape(b * g * nc, ns2), s_i.reshape(b * g * nc, ns2), a_r, a_i)

    yt = pl.pallas_call(
        _s5_out_kernel,
        out_shape=jax.ShapeDtypeStruct((g, b * nc, width), BF16),
        grid=(g // gb, b),
        in_specs=[pl.BlockSpec((gb, nc, width), lambda gi, bb: (gi, bb, 0)),
                  pl.BlockSpec((1, gb, nc, ns2), lambda gi, bb: (bb, gi, 0, 0)),
                  pl.BlockSpec((1, gb, nc, ns2), lambda gi, bb: (bb, gi, 0, 0)),
                  pl.BlockSpec((gb, 2 * ns2, width), lambda gi, bb: (gi, 0, 0))],
        out_specs=pl.BlockSpec((gb, nc, width), lambda gi, bb: (gi, bb, 0)),
        compiler_params=_cparams(("arbitrary", "arbitrary")),
        name="s5_chunk_out",
    )(y_in, h_r.reshape(b, g, nc, ns2), h_i.reshape(b, g, nc, ns2), w3)
    return yt.reshape(g, b, nc, q, pch).transpose(1, 2, 3, 0, 4).reshape(b, p, d)


def _moe_schedule(idx, n_exp, tm, n_tiles):
    t = idx.shape[0]
    e_flat = idx.reshape(-1)
    n_asg = e_flat.shape[0]
    order = jnp.argsort(e_flat, stable=True).astype(jnp.int32)
    counts = jnp.sum((e_flat[:, None] == jnp.arange(n_exp)[None, :]).astype(jnp.int32), axis=0)
    padded = (counts + tm - 1) // tm * tm
    gstart = jnp.cumsum(padded) - padded
    cstart = jnp.cumsum(counts) - counts
    gend = gstart + padded
    n_active = (jnp.sum(padded) // tm).astype(jnp.int32)
    tile_start = jnp.arange(n_tiles, dtype=jnp.int32) * tm
    tile_e = jnp.sum((gend[None, :] <= tile_start[:, None]).astype(jnp.int32), axis=1)
    tile_e = jnp.minimum(tile_e, n_exp - 1)
    tile_e = jnp.where(jnp.arange(n_tiles) < n_active, tile_e, tile_e[jnp.maximum(n_active - 1, 0)])
    row_e = jnp.repeat(tile_e, tm)
    j = jnp.arange(n_tiles * tm, dtype=jnp.int32) - gstart[row_e].astype(jnp.int32)
    valid = (j < counts[row_e]) & (jnp.arange(n_tiles * tm) < n_active * tm)
    sidx = jnp.clip(cstart[row_e] + j, 0, n_asg - 1)
    row_token = jnp.where(valid, order[sidx] // TOP_K, 0).astype(jnp.int32)
    sorted_e = e_flat[order]
    dest = (gstart[sorted_e] + jnp.arange(n_asg) - cstart[sorted_e]).astype(jnp.int32)
    inv = jnp.argsort(order).astype(jnp.int32)
    pos = dest[inv].reshape(t, TOP_K)
    return tile_e, n_active.reshape(1), row_token, pos


def _moe_prep_kernel(w_ref, sel_ref, o_ref):
    sel = sel_ref[...]
    blk = sel.shape[0]
    for c in range(w_ref.shape[3] // blk):
        w = w_ref[0, 0, :, c * blk:(c + 1) * blk].astype(BF16)
        o_ref[0, :, c * blk:(c + 1) * blk] = jnp.dot(w, sel, preferred_element_type=F32).astype(BF16)


def _moe_prep(w_gate_up, layer):
    _, n_exp, d, n = w_gate_up.shape
    blk = 2 * LANES
    tk = _pick(d, 512, 8)
    src = jnp.arange(blk)
    sel = (jnp.arange(blk)[None, :] == (src // 2 + (src % 2) * LANES)[:, None]).astype(BF16)
    return pl.pallas_call(
        _moe_prep_kernel,
        out_shape=jax.ShapeDtypeStruct((n_exp, d, n), BF16),
        grid=(n_exp, d // tk),
        in_specs=[pl.BlockSpec((1, 1, tk, n), lambda e, k: (layer, e, k, 0)),
                  pl.BlockSpec((blk, blk), lambda e, k: (0, 0))],
        out_specs=pl.BlockSpec((1, tk, n), lambda e, k: (e, k, 0)),
        compiler_params=_cparams(("arbitrary", "arbitrary")),
        name="moe_weight_prep",
    )(w_gate_up, sel)


def _cast_kernel(w_ref, o_ref):
    o_ref[0] = w_ref[0, 0].astype(o_ref.dtype)


def _moe_cast(w_down, layer):
    _, n_exp, f, d = w_down.shape
    tk = _pick(f, 512, 8)
    return pl.pallas_call(
        _cast_kernel,
        out_shape=jax.ShapeDtypeStruct((n_exp, f, d), BF16),
        grid=(n_exp, f // tk),
        in_specs=[pl.BlockSpec((1, 1, tk, d), lambda e, k: (layer, e, k, 0))],
        out_specs=pl.BlockSpec((1, tk, d), lambda e, k: (e, k, 0)),
        compiler_params=_cparams(("arbitrary", "arbitrary")),
        name="moe_weight_cast",
    )(w_down)


def _split_blocks(v):
    lead = v.shape[:-1]
    return v.reshape(lead + (-1, LANES, 2)).swapaxes(-1, -2).reshape(lead + (-1,))


def _moe_kernel(te_ref, na_ref, tok_ref, tokn_ref, x_hbm, w_ref, b_ref, wd_ref, bd_ref, o_ref, xbuf, sem,
                *, tm):
    i = pl.program_id(0)
    n_act = na_ref[0]
    slot = i % 2

    def row_copy(tok, r, s):
        return pltpu.make_async_copy(x_hbm.at[tok], xbuf.at[s, pl.ds(r, 1)], sem.at[s])

    def wait_rows(s):
        def wbody(r, c):
            row_copy(0, 0, s).wait()
            return c
        lax.fori_loop(0, tm, wbody, 0, unroll=8)

    @pl.when(i == 0)
    def _():
        def body(r, c):
            row_copy(tok_ref[0, 0, r], r, 0).start()
            return c
        lax.fori_loop(0, tm, body, 0, unroll=8)

    @pl.when(i < n_act)
    def _():
        wait_rows(slot)
        lo, hi = _unpack_halves(xbuf[slot])
        for r in range(tm):
            row_copy(tokn_ref[0, 0, r], r, 1 - slot).start()
        half = lo.shape[1]
        h = (jnp.dot(lo.astype(BF16), w_ref[0, :half], preferred_element_type=F32)
             + jnp.dot(hi.astype(BF16), w_ref[0, half:], preferred_element_type=F32) + b_ref[0])
        acts = []
        for c in range(h.shape[1] // (2 * LANES)):
            gate = jnp.minimum(h[:, 2 * c * LANES:(2 * c + 1) * LANES], SWIGLU_LIMIT)
            up = jnp.clip(h[:, (2 * c + 1) * LANES:(2 * c + 2) * LANES], -SWIGLU_LIMIT, SWIGLU_LIMIT)
            acts.append(((up + 1.0) * (gate * jax.nn.sigmoid(SWIGLU_ALPHA * gate))).astype(BF16))
        act = jnp.concatenate(acts, axis=1)
        y = jnp.dot(act, wd_ref[0], preferred_element_type=F32) + bd_ref[0]
        o_ref[...] = _pack_halves(y)

    @pl.when(i == n_act)
    def _():
        wait_rows(slot)

    @pl.when(i >= n_act)
    def _():
        o_ref[...] = jnp.zeros(o_ref.shape, o_ref.dtype)


def _moe_experts(hp3d, row_token, tile_e, n_active, wp, bp, wd, bd, tm):
    half = hp3d.shape[-1]
    n_exp, d, n = wp.shape
    n_tiles = row_token.shape[0] // tm
    tok3 = row_token.reshape(n_tiles, 1, tm)
    exp3 = lambda i, te, na: (te[i], 0, 0)
    return pl.pallas_call(
        functools.partial(_moe_kernel, tm=tm),
        out_shape=jax.ShapeDtypeStruct((n_tiles * tm, half), jnp.uint32),
        grid_spec=pltpu.PrefetchScalarGridSpec(
            num_scalar_prefetch=2,
            grid=(n_tiles,),
            in_specs=[pl.BlockSpec((1, 1, tm), lambda i, te, na: (i, 0, 0), memory_space=pltpu.SMEM),
                      pl.BlockSpec((1, 1, tm), lambda i, te, na: (jnp.minimum(i + 1, n_tiles - 1), 0, 0),
                                   memory_space=pltpu.SMEM),
                      pl.BlockSpec(memory_space=pl.ANY),
                      pl.BlockSpec((1, d, n), exp3), pl.BlockSpec((1, 1, n), exp3),
                      pl.BlockSpec((1, n // 2, d), exp3), pl.BlockSpec((1, 1, d), exp3)],
            out_specs=pl.BlockSpec((tm, half), lambda i, te, na: (i, 0)),
            scratch_shapes=[pltpu.VMEM((2, tm, half), jnp.uint32), pltpu.SemaphoreType.DMA((2,))]),
        compiler_params=_cparams(("arbitrary",)),
        name="moe_experts",
    )(tile_e, n_active, tok3, tok3, hp3d, wp, bp, wd, bd)


def _combine_kernel(pos_ref, posn_ref, gw_ref, x_ref, g_ref, y_hbm, o_ref, buf, sem, *, rows):
    i = pl.program_id(0)
    n = pl.num_programs(0)

    slot = i % 2

    def row_copy(src, k, r, s):
        return pltpu.make_async_copy(y_hbm.at[src], buf.at[s, k, pl.ds(r, 1)], sem.at[s])

    @pl.when(i == 0)
    def _():
        def body(r, c):
            for k in range(TOP_K):
                row_copy(pos_ref[0, 0, r * TOP_K + k], k, r, 0).start()
            return c
        lax.fori_loop(0, rows, body, 0, unroll=2)

    def wbody(r, c):
        row_copy(0, 0, 0, slot).wait()
        return c
    lax.fori_loop(0, rows * TOP_K, wbody, 0, unroll=8)

    def combine():
        gw = gw_ref[...]
        half = buf.shape[-1]
        acc_lo = acc_hi = None
        for k in range(TOP_K):
            lo, hi = _unpack_halves(buf[slot, k])
            w = gw[:, k:k + 1]
            acc_lo = w * lo if k == 0 else acc_lo + w * lo
            acc_hi = w * hi if k == 0 else acc_hi + w * hi
        g = g_ref[0]
        o_ref[:, :half] = x_ref[:, :half] + g[:, :half] * acc_lo
        o_ref[:, half:] = x_ref[:, half:] + g[:, half:] * acc_hi

    @pl.when(i + 1 < n)
    def _():
        for r in range(rows):
            for k in range(TOP_K):
                row_copy(posn_ref[0, 0, r * TOP_K + k], k, r, 1 - slot).start()
        combine()

    @pl.when(i + 1 == n)
    def _():
        combine()


def _moe_combine(y_sorted, pos, gw2d, x2d, modt, tm_row, ctx_tiles, tiles_per_batch):
    t, d = x2d.shape
    rows = _pick(tm_row, COMBINE_ROWS, 8)
    n_steps = t // rows
    sub = tm_row // rows
    pos3 = pos.reshape(n_steps, 1, rows * TOP_K)

    def gmap(i):
        tile = i // sub
        bb = tile // tiles_per_batch
        seg = jnp.where(tile % tiles_per_batch >= ctx_tiles, 1, 0)
        return (bb * 2 + seg, 0, 5)

    return pl.pallas_call(
        functools.partial(_combine_kernel, rows=rows),
        out_shape=jax.ShapeDtypeStruct((t, d), F32),
        grid=(n_steps,),
        in_specs=[pl.BlockSpec((1, 1, rows * TOP_K), lambda i: (i, 0, 0), memory_space=pltpu.SMEM),
                  pl.BlockSpec((1, 1, rows * TOP_K), lambda i: (jnp.minimum(i + 1, n_steps - 1), 0, 0),
                               memory_space=pltpu.SMEM),
                  pl.BlockSpec((rows, LANES), lambda i: (i, 0)),
                  pl.BlockSpec((rows, d), lambda i: (i, 0)),
                  pl.BlockSpec((1, 1, d), gmap),
                  pl.BlockSpec(memory_space=pl.ANY)],
        out_specs=pl.BlockSpec((rows, d), lambda i: (i, 0)),
        scratch_shapes=[pltpu.VMEM((2, TOP_K, rows, d // 2), jnp.uint32), pltpu.SemaphoreType.DMA((2,))],
        compiler_params=_cparams(("arbitrary",)),
        name="moe_combine",
    )(pos3, pos3, gw2d, x2d, modt, y_sorted)


def _rope_tables(seq, ctx_len):
    n_rows = seq // GRID_W
    axis_rot = HEAD_DIM // 2
    rows = jnp.repeat(jnp.arange(n_rows, dtype=F32), GRID_W)
    cols = jnp.tile(jnp.arange(GRID_W, dtype=F32), n_rows)
    inv_freq = ROPE_BASE ** (-jnp.arange(0, axis_rot, 2, dtype=F32) / axis_rot)
    ang_r = rows[:, None] * inv_freq
    ang_c = cols[:, None] * inv_freq
    ang = jnp.concatenate([ang_r, ang_r, ang_c, ang_c], axis=-1)
    sign = jnp.tile(jnp.concatenate([-jnp.ones(axis_rot // 2, F32), jnp.ones(axis_rot // 2, F32)]), 2)
    cos = jnp.concatenate([jnp.ones((ctx_len, HEAD_DIM), F32), jnp.cos(ang)], axis=0)
    sin = jnp.concatenate([jnp.zeros((ctx_len, HEAD_DIM), F32), jnp.sin(ang) * sign], axis=0)
    return cos, sin


def kernel(x, c, ctx, c_ctx, w_mod, b_mod, norm_mix, norm_ffn, w_router, b_router, w_gate_up, b_gate_up, w_down, b_down, attn_w_qkv, attn_w_o, attn_q_gain, attn_k_gain, attn_sinks, ssm_a_re, ssm_a_im, ssm_log_dt, ssm_b_re, ssm_b_im, ssm_c_re, ssm_c_im, ssm_d, ssm_w_glu, ssm_b_glu):
    b, seq, d = x.shape
    ctx_len = ctx.shape[1]
    depth = w_mod.shape[0]
    p = ctx_len + seq
    t = b * p
    n_exp = w_router.shape[-1]
    n_q = d // HEAD_DIM
    n_kv = (attn_w_qkv.shape[-1] // HEAD_DIM - n_q) // 2
    assert ctx_len % ATT_BLOCK == 0 and seq % ATT_BLOCK == 0 and seq % GRID_W == 0
    assert d % (2 * LANES) == 0 and ctx_len % SSM_CHUNK == 0 and seq % SSM_CHUNK == 0

    tm = _pick(math.gcd(p, ctx_len), ROW_TILE)
    mm_tm = _pick(math.gcd(p, ctx_len), MM_TILE_M)
    tiles_pb = p // mm_tm
    ctx_tiles = ctx_len // mm_tm

    pad = (-(b + 1)) % 8
    cvec = jnp.concatenate([c, c_ctx[None, :], jnp.zeros((pad, d), F32)], axis=0)
    mod_all = _mod_all(cvec, w_mod, b_mod)

    cos_t, sin_t = _rope_tables(seq, ctx_len)
    xs = jnp.concatenate([ctx, x], axis=1)

    tm_moe = min(MOE_TILE, _pick(t * TOP_K, MOE_TILE, 8))
    n_tiles = (t * TOP_K) // tm_moe + n_exp
    scale = HEAD_DIM ** -0.5

    for i in range(depth):
        j = i // 2
        lat = mod_all[i, :b]
        cx = jnp.broadcast_to(mod_all[i, b][None], lat.shape)
        modt = jnp.stack([cx, lat], axis=1).reshape(b * 2, 1, 6 * d)

        x2d = xs.reshape(t, d)
        if i % 2 == 0:
            gain = jnp.concatenate([jnp.tile(attn_q_gain[j] * scale, n_q), jnp.tile(attn_k_gain[j], n_kv),
                                    jnp.ones((n_kv * HEAD_DIM,), F32)]).reshape(1, -1)
            qkv = _qkv_proj(x2d, norm_mix[i], modt, attn_w_qkv[j].astype(BF16), gain, cos_t, sin_t,
                            n_q + n_kv, mm_tm, ctx_tiles, tiles_pb)
            o = _attention(qkv.reshape(b, p, -1), attn_sinks[j].astype(F32), n_q, n_kv, ctx_len)
            x2d = _proj_residual(o.reshape(t, d), attn_w_o[j].astype(BF16), x2d, modt, 2,
                                 mm_tm, ctx_tiles, tiles_pb)
        else:
            u = _norm_mod(xs, norm_mix[i], modt, 0, ctx_len)
            tables = _s5_tables(ssm_a_re[j], ssm_a_im[j], ssm_log_dt[j], ssm_b_re[j], ssm_b_im[j],
                                ssm_c_re[j], ssm_c_im[j], ssm_d[j])
            gy = _s5_mix(u, tables, ctx_len)
            x2d = _glu_residual(gy.reshape(t, d), ssm_w_glu[j].astype(BF16), ssm_b_glu[j], x2d, modt, 2,
                                mm_tm, ctx_tiles, tiles_pb)

        hp, idx, gw = _norm_router(x2d.reshape(b, p, d), norm_ffn[i], modt, w_router[i], b_router[i], ctx_len)
        idx2d = idx.reshape(t, LANES)[:, :TOP_K]
        tile_e, n_active, row_token, pos = _moe_schedule(idx2d, n_exp, tm_moe, n_tiles)
        y_sorted = _moe_experts(hp.reshape(t, 1, d // 2), row_token, tile_e, n_active,
                                _moe_prep(w_gate_up, i), _split_blocks(b_gate_up[i]).reshape(n_exp, 1, -1),
                                _moe_cast(w_down, i), b_down[i].reshape(n_exp, 1, d), tm_moe)
        x2d = _moe_combine(y_sorted.reshape(-1, 1, d // 2), pos, gw.reshape(t, LANES), x2d, modt,
                           tm, ctx_len // tm, p // tm)
        xs = x2d.reshape(b, p, d)

    return xs[:, ctx_len:]
```

```python
import functools
import math

import jax
import jax.numpy as jnp
from jax import lax
from jax.experimental import pallas as pl
from jax.experimental.pallas import tpu as pltpu

F32 = jnp.float32
BF16 = jnp.bfloat16

HEAD_DIM = 128
ATT_BLOCK = 128
GRID_W = 64
ROPE_BASE = 10000.0
SSM_GROUP = 16
SSM_STATE = 64
SSM_CHUNK = 16
TOP_K = 4
SWIGLU_LIMIT = 7.0
SWIGLU_ALPHA = 1.702
EPS = 1e-6
NEG_INF = -1e30
DT_FLOOR_RE = -1e-4

LANES = 128
VMEM_LIMIT = 56 * 1024 * 1024

ROW_TILE = 256
MM_TILE_M = 512
PROJ_TILE_N = 2048
GLU_TILE_N = 1024
MOE_TILE = 512
COMBINE_ROWS = 128
SSM_GROUP_BLOCK = 16


def _cparams(sem):
    return pltpu.CompilerParams(dimension_semantics=sem, vmem_limit_bytes=VMEM_LIMIT)


def _pick(n, pref, mult=8):
    t = min(n, pref)
    while n % t or t % mult:
        t -= 1
    return t


def _mod_kernel(c_ref, w_ref, b_ref, o_ref):
    c = c_ref[...]
    s = c * jax.nn.sigmoid(c)
    o_ref[0] = jnp.dot(s.astype(BF16), w_ref[0].astype(BF16), preferred_element_type=F32) + b_ref[0]


def _mod_all(cvec, w_mod, b_mod):
    depth, d, n = w_mod.shape
    tn = _pick(n, 1024, LANES)
    return pl.pallas_call(
        _mod_kernel,
        out_shape=jax.ShapeDtypeStruct((depth, cvec.shape[0], n), F32),
        grid=(depth, n // tn),
        in_specs=[pl.BlockSpec(cvec.shape, lambda l, j: (0, 0)),
                  pl.BlockSpec((1, d, tn), lambda l, j: (l, 0, j)),
                  pl.BlockSpec((1, 1, tn), lambda l, j: (l, 0, j))],
        out_specs=pl.BlockSpec((1, cvec.shape[0], tn), lambda l, j: (l, 0, j)),
        compiler_params=_cparams(("arbitrary", "arbitrary")),
        name="adaln_rows",
    )(cvec, w_mod, b_mod.reshape(depth, 1, n))


def _normed(x, w, sh, sc):
    r = lax.rsqrt(jnp.mean(x * x, axis=-1, keepdims=True) + EPS)
    return (x * r * w) * (1.0 + sc) + sh


def _norm_mod_kernel(x_ref, w_ref, sh_ref, sc_ref, o_ref):
    o_ref[0] = _normed(x_ref[0], w_ref[...], sh_ref[0], sc_ref[0]).astype(o_ref.dtype)


def _mod_spec(d, which, ctx_tiles):
    return pl.BlockSpec((1, 1, d), lambda b, i: (b * 2 + jnp.where(i >= ctx_tiles, 1, 0), 0, which))


def _norm_mod(x, w, modt, which_shift, ctx_len):
    b, p, d = x.shape
    tm = _pick(math.gcd(p, ctx_len), ROW_TILE)
    ct = ctx_len // tm
    return pl.pallas_call(
        _norm_mod_kernel,
        out_shape=jax.ShapeDtypeStruct((b, p, d), BF16),
        grid=(b, p // tm),
        in_specs=[pl.BlockSpec((1, tm, d), lambda bb, i: (bb, i, 0)),
                  pl.BlockSpec((1, d), lambda bb, i: (0, 0)),
                  _mod_spec(d, which_shift, ct), _mod_spec(d, which_shift + 1, ct)],
        out_specs=pl.BlockSpec((1, tm, d), lambda bb, i: (bb, i, 0)),
        compiler_params=_cparams(("arbitrary", "arbitrary")),
        name="norm_mod",
    )(x, w.reshape(1, d), modt, modt)


def _pack_halves(h):
    half = h.shape[-1] // 2
    bits = lax.bitcast_convert_type(h.astype(BF16).astype(F32), jnp.uint32)
    return (bits[:, :half] >> 16) | (bits[:, half:] & jnp.uint32(0xFFFF0000))


def _unpack_halves(xp):
    lo = lax.bitcast_convert_type(xp << 16, F32)
    hi = lax.bitcast_convert_type(xp & jnp.uint32(0xFFFF0000), F32)
    return lo, hi


def _norm_router_kernel(x_ref, w_ref, sh_ref, sc_ref, wr_ref, br_ref, hp_ref, idx_ref, gw_ref):
    h = _normed(x_ref[0], w_ref[...], sh_ref[0], sc_ref[0])
    hp_ref[0] = _pack_halves(h)
    logits = jnp.dot(h, wr_ref[...], precision=lax.Precision.HIGHEST,
                     preferred_element_type=F32) + br_ref[...]
    n_exp = logits.shape[-1]
    lane = lax.broadcasted_iota(jnp.int32, logits.shape, 1)
    out_lane = lax.broadcasted_iota(jnp.int32, idx_ref.shape[1:], 1)
    vals, idxs = [], []
    rest = logits
    for _ in range(TOP_K):
        m = jnp.max(rest, axis=-1, keepdims=True)
        idx = jnp.min(jnp.where(rest == m, lane, n_exp), axis=-1, keepdims=True)
        vals.append(m)
        idxs.append(idx)
        rest = jnp.where(lane == idx, -jnp.inf, rest)
    exps = [jnp.exp(v - vals[0]) for v in vals]
    tot = exps[0]
    for e in exps[1:]:
        tot = tot + e
    idx_out = jnp.zeros(idx_ref.shape[1:], jnp.int32)
    gw_out = jnp.zeros(gw_ref.shape[1:], F32)
    for k in range(TOP_K):
        idx_out = jnp.where(out_lane == k, idxs[k], idx_out)
        gw_out = jnp.where(out_lane == k, exps[k] / tot, gw_out)
    idx_ref[0] = idx_out
    gw_ref[0] = gw_out


def _norm_router(x, w, modt, w_router, b_router, ctx_len):
    b, p, d = x.shape
    n_exp = w_router.shape[-1]
    tm = _pick(math.gcd(p, ctx_len), ROW_TILE)
    ct = ctx_len // tm
    row = lambda bb, i: (bb, i, 0)
    return pl.pallas_call(
        _norm_router_kernel,
        out_shape=(jax.ShapeDtypeStruct((b, p, d // 2), jnp.uint32),
                   jax.ShapeDtypeStruct((b, p, LANES), jnp.int32),
                   jax.ShapeDtypeStruct((b, p, LANES), F32)),
        grid=(b, p // tm),
        in_specs=[pl.BlockSpec((1, tm, d), row),
                  pl.BlockSpec((1, d), lambda bb, i: (0, 0)),
                  _mod_spec(d, 3, ct), _mod_spec(d, 4, ct),
                  pl.BlockSpec((d, n_exp), lambda bb, i: (0, 0)),
                  pl.BlockSpec((1, n_exp), lambda bb, i: (0, 0))],
        out_specs=(pl.BlockSpec((1, tm, d // 2), row),
                   pl.BlockSpec((1, tm, LANES), row),
                   pl.BlockSpec((1, tm, LANES), row)),
        compiler_params=_cparams(("arbitrary", "arbitrary")),
        name="norm_router",
    )(x, w.reshape(1, d), modt, modt, w_router, b_router.reshape(1, n_exp))


def _qkv_kernel(x_ref, nw_ref, sh_ref, sc_ref, w_ref, g_ref, cos_ref, sin_ref, o_ref, *, n_norm_heads):
    u = _normed(x_ref[...], nw_ref[...], sh_ref[0], sc_ref[0]).astype(BF16)
    acc = jnp.dot(u, w_ref[...], preferred_element_type=F32)
    cos = cos_ref[...]
    sin = sin_ref[...]
    lane = lax.broadcasted_iota(jnp.int32, cos.shape, 1)
    first = (lane // (HEAD_DIM // 4)) % 2 == 0
    for hh in range(acc.shape[1] // HEAD_DIM):
        sl = slice(hh * HEAD_DIM, (hh + 1) * HEAD_DIM)
        xh = acc[:, sl]
        if hh < n_norm_heads:
            r = lax.rsqrt(jnp.mean(xh * xh, axis=-1, keepdims=True) + EPS)
            y = xh * r * g_ref[:, sl]
            partner = jnp.where(first, pltpu.roll(y, HEAD_DIM - HEAD_DIM // 4, 1),
                                pltpu.roll(y, HEAD_DIM // 4, 1))
            xh = y * cos + partner * sin
        o_ref[:, sl] = xh.astype(o_ref.dtype)


def _row_mod_spec(d, which, ctx_tiles, tiles_per_batch):
    def imap(i):
        seg = jnp.where(i % tiles_per_batch >= ctx_tiles, 1, 0)
        return ((i // tiles_per_batch) * 2 + seg, 0, which)

    return pl.BlockSpec((1, 1, d), imap)


def _qkv_proj(x2d, norm_w, modt, w_bf16, gain_row, cos_t, sin_t, n_norm_heads, tm, ctx_tiles, tiles_per_batch):
    m, k = x2d.shape
    n = w_bf16.shape[1]
    pos = lambda i: (i % tiles_per_batch, 0)
    return pl.pallas_call(
        functools.partial(_qkv_kernel, n_norm_heads=n_norm_heads),
        out_shape=jax.ShapeDtypeStruct((m, n), BF16),
        grid=(m // tm,),
        in_specs=[pl.BlockSpec((tm, k), lambda i: (i, 0)),
                  pl.BlockSpec((1, k), lambda i: (0, 0)),
                  _row_mod_spec(k, 0, ctx_tiles, tiles_per_batch),
                  _row_mod_spec(k, 1, ctx_tiles, tiles_per_batch),
                  pl.BlockSpec((k, n), lambda i: (0, 0)),
                  pl.BlockSpec((1, n), lambda i: (0, 0)),
                  pl.BlockSpec((tm, HEAD_DIM), pos),
                  pl.BlockSpec((tm, HEAD_DIM), pos)],
        out_specs=pl.BlockSpec((tm, n), lambda i: (i, 0)),
        compiler_params=_cparams(("arbitrary",)),
        name="qkv_proj",
    )(x2d, norm_w.reshape(1, k), modt, modt, w_bf16, gain_row, cos_t, sin_t)


def _attn_kernel(sink_ref, q_ref, kc_ref, vc_ref, kp_ref, ks_ref, kn_ref, vp_ref, vs_ref, vn_ref,
                 o_ref, *, q_per_kv, ctx_blocks, n_blocks):
    h = pl.program_id(1)
    i = pl.program_id(2)
    blk = ATT_BLOCK
    q = q_ref[0]
    qs = jnp.concatenate([q[:, g * HEAD_DIM:(g + 1) * HEAD_DIM] for g in range(q_per_kv)], axis=0)
    rows = qs.shape[0]
    dn = (((1,), (1,)), ((), ()))
    k_loc = jnp.concatenate([kp_ref[0], ks_ref[0], kn_ref[0]], axis=0)
    v_loc = jnp.concatenate([vp_ref[0], vs_ref[0], vn_ref[0]], axis=0)
    s_loc = lax.dot_general(qs, k_loc, dn, preferred_element_type=F32)
    s_ctx = lax.dot_general(qs, kc_ref[0], dn, preferred_element_type=F32)

    qi = lax.broadcasted_iota(jnp.int32, s_loc.shape, 0) % blk
    kj = lax.broadcasted_iota(jnp.int32, s_loc.shape, 1)
    is_lat = i >= ctx_blocks
    ok_prev = (kj < blk) & (kj >= qi) & (i > ctx_blocks)
    ok_self = (kj >= blk) & (kj < 2 * blk)
    ok_next = (kj >= 2 * blk) & (kj - 2 * blk <= qi) & (i < n_blocks - 1)
    valid = (ok_prev | ok_self | ok_next) & is_lat
    s_loc = jnp.where(valid, s_loc, NEG_INF)

    row_head = lax.broadcasted_iota(jnp.int32, (rows, 1), 0) // blk
    sink = jnp.zeros((rows, 1), F32)
    for g in range(q_per_kv):
        sink = jnp.where(row_head == g, sink_ref[h * q_per_kv + g], sink)

    m = jnp.maximum(jnp.maximum(jnp.max(s_loc, axis=-1, keepdims=True),
                                jnp.max(s_ctx, axis=-1, keepdims=True)), sink)
    p_loc = jnp.exp(s_loc - m)
    p_ctx = jnp.exp(s_ctx - m)
    denom = (jnp.sum(p_loc, axis=-1, keepdims=True) + jnp.sum(p_ctx, axis=-1, keepdims=True)
             + jnp.exp(sink - m))
    o = (jnp.dot(p_loc.astype(BF16), v_loc, preferred_element_type=F32)
         + jnp.dot(p_ctx.astype(BF16), vc_ref[0], preferred_element_type=F32)) / denom
    o = o.astype(o_ref.dtype)
    for g in range(q_per_kv):
        o_ref[0, :, g * HEAD_DIM:(g + 1) * HEAD_DIM] = o[g * blk:(g + 1) * blk]


def _attention(qkv, sinks, n_q, n_kv, ctx_len):
    b, p, _ = qkv.shape
    blk = ATT_BLOCK
    nb = p // blk
    cb = ctx_len // blk
    qpk = n_q // n_kv
    qw = qpk * HEAD_DIM
    kcol = n_q
    vcol = n_q + n_kv

    def loc(col0, off):
        return pl.BlockSpec(
            (1, blk, HEAD_DIM),
            lambda bb, h, i, s: (bb, jnp.clip(i + off, cb, nb - 1), col0 + h))

    return pl.pallas_call(
        functools.partial(_attn_kernel, q_per_kv=qpk, ctx_blocks=cb, n_blocks=nb),
        out_shape=jax.ShapeDtypeStruct((b, p, n_q * HEAD_DIM), BF16),
        grid_spec=pltpu.PrefetchScalarGridSpec(
            num_scalar_prefetch=1,
            grid=(b, n_kv, nb),
            in_specs=[pl.BlockSpec((1, blk, qw), lambda bb, h, i, s: (bb, i, h)),
                      pl.BlockSpec((1, ctx_len, HEAD_DIM), lambda bb, h, i, s: (bb, 0, kcol + h)),
                      pl.BlockSpec((1, ctx_len, HEAD_DIM), lambda bb, h, i, s: (bb, 0, vcol + h)),
                      loc(kcol, -1), loc(kcol, 0), loc(kcol, 1),
                      loc(vcol, -1), loc(vcol, 0), loc(vcol, 1)],
            out_specs=pl.BlockSpec((1, blk, qw), lambda bb, h, i, s: (bb, i, h))),
        compiler_params=_cparams(("arbitrary", "arbitrary", "arbitrary")),
        name="window_attn",
    )(sinks, qkv, qkv, qkv, qkv, qkv, qkv, qkv, qkv, qkv)


def _gate_spec(d, which, ctx_tiles, tiles_per_batch, tn):
    per = d // tn

    def imap(j, i):
        bb = i // tiles_per_batch
        seg = jnp.where(i % tiles_per_batch >= ctx_tiles, 1, 0)
        return (bb * 2 + seg, 0, which * per + j)

    return pl.BlockSpec((1, 1, tn), imap)


def _proj_res_kernel(a_ref, w_ref, x_ref, g_ref, o_ref):
    y = jnp.dot(a_ref[...], w_ref[...], preferred_element_type=F32)
    o_ref[...] = x_ref[...] + g_ref[0] * y


def _proj_residual(a2d, w_bf16, x2d, modt, which_gate, tm, ctx_tiles, tiles_per_batch):
    m, k = a2d.shape
    n = w_bf16.shape[1]
    tn = _pick(n, PROJ_TILE_N, LANES)
    return pl.pallas_call(
        _proj_res_kernel,
        out_shape=jax.ShapeDtypeStruct((m, n), F32),
        grid=(n // tn, m // tm),
        in_specs=[pl.BlockSpec((tm, k), lambda j, i: (i, 0)),
                  pl.BlockSpec((k, tn), lambda j, i: (0, j)),
                  pl.BlockSpec((tm, tn), lambda j, i: (i, j)),
                  _gate_spec(n, which_gate, ctx_tiles, tiles_per_batch, tn)],
        out_specs=pl.BlockSpec((tm, tn), lambda j, i: (i, j)),
        compiler_params=_cparams(("arbitrary", "arbitrary")),
        name="proj_residual",
    )(a2d, w_bf16, x2d, modt)


def _glu_res_kernel(a_ref, w1_ref, w2_ref, b1_ref, b2_ref, x_ref, g_ref, o_ref):
    a = a_ref[...]
    z1 = jnp.dot(a, w1_ref[...], preferred_element_type=F32) + b1_ref[...]
    z2 = jnp.dot(a, w2_ref[...], preferred_element_type=F32) + b2_ref[...]
    o_ref[...] = x_ref[...] + g_ref[0] * (z1 * jax.nn.sigmoid(z2))


def _glu_residual(a2d, w_bf16, b_glu, x2d, modt, which_gate, tm, ctx_tiles, tiles_per_batch):
    m, k = a2d.shape
    d = w_bf16.shape[1] // 2
    tn = _pick(d, GLU_TILE_N, LANES)
    nj = d // tn
    b2 = b_glu.reshape(1, 2 * d)
    return pl.pallas_call(
        _glu_res_kernel,
        out_shape=jax.ShapeDtypeStruct((m, d), F32),
        grid=(nj, m // tm),
        in_specs=[pl.BlockSpec((tm, k), lambda j, i: (i, 0)),
                  pl.BlockSpec((k, tn), lambda j, i: (0, j)),
                  pl.BlockSpec((k, tn), lambda j, i: (0, j + nj)),
                  pl.BlockSpec((1, tn), lambda j, i: (0, j)),
                  pl.BlockSpec((1, tn), lambda j, i: (0, j + nj)),
                  pl.BlockSpec((tm, tn), lambda j, i: (i, j)),
                  _gate_spec(d, which_gate, ctx_tiles, tiles_per_batch, tn)],
        out_specs=pl.BlockSpec((tm, tn), lambda j, i: (i, j)),
        compiler_params=_cparams(("arbitrary", "arbitrary")),
        name="glu_residual",
    )(a2d, w_bf16, w_bf16, b2, b2, x2d, modt)


def _s5_tables(a_re, a_im, log_dt, b_re, b_im, c_re, c_im, d_skip):
    q, n_state, pch = SSM_CHUNK, a_re.shape[-1], b_re.shape[-1]
    g = a_re.shape[1]
    hp = lax.Precision.HIGHEST
    lam = lax.complex(jnp.minimum(a_re.astype(F32), DT_FLOOR_RE), a_im.astype(F32))
    lam_dt = lam * jnp.exp(log_dt.astype(F32))[..., None]
    b_bar = ((jnp.exp(lam_dt) - 1.0) / lam)[..., None] * lax.complex(b_re.astype(F32), b_im.astype(F32))
    c_mat = lax.complex(c_re.astype(F32), c_im.astype(F32))
    tau = jnp.arange(q + 1, dtype=F32)
    pw = jnp.exp(lam_dt[:, :, None, :] * tau[None, None, :, None])
    kern = jnp.einsum('dgpn,dgtn,dgnr->dgtpr', c_mat, pw[:, :, :q], b_bar, precision=hp).real
    t_i = jnp.arange(q)[:, None]
    s_i = jnp.arange(q)[None, :]
    kf = jnp.where((t_i >= s_i)[None, :, :, None, None], kern[0][:, jnp.clip(t_i - s_i, 0, q - 1)], 0.0)
    kb = jnp.where((s_i >= t_i)[None, :, :, None, None], kern[1][:, jnp.clip(s_i - t_i, 0, q - 1)], 0.0)
    dsk = d_skip.astype(F32).reshape(g, pch)
    diag = (jnp.eye(q)[None, :, :, None, None] * jnp.eye(pch)[None, None, None] * dsk[:, None, None, :, None])
    mt = (kf + kb + diag).transpose(0, 2, 4, 1, 3).reshape(g, q * pch, q * pch)
    wsf = pw[0][:, ::-1][:, 1:, :, None] * b_bar[0][:, None]
    wsb = pw[1][:, :q, :, None] * b_bar[1][:, None]
    wsf = wsf.transpose(0, 1, 3, 2).reshape(g, q * pch, n_state)
    wsb = wsb.transpose(0, 1, 3, 2).reshape(g, q * pch, n_state)
    w1 = jnp.concatenate([mt, wsf.real, wsb.real, wsf.imag, wsb.imag], axis=-1)
    cf = c_mat[0][:, None] * pw[0][:, 1:, None, :]
    cb = c_mat[1][:, None] * pw[1][:, ::-1][:, :q, None, :]
    cf = cf.transpose(0, 3, 1, 2).reshape(g, n_state, q * pch)
    cb = cb.transpose(0, 3, 1, 2).reshape(g, n_state, q * pch)
    w3 = jnp.concatenate([cf.real, cb.real, -cf.imag, -cb.imag], axis=1)
    aq = pw[:, :, q]
    a_r = jnp.concatenate([aq[0].real, aq[1].real], axis=-1)
    a_i = jnp.concatenate([aq[0].imag, aq[1].imag], axis=-1)
    return w1.astype(BF16), w3.astype(BF16), a_r, a_i


def _s5_in_kernel(u_ref, w_ref, y_ref, sr_ref, si_ref):
    width = u_ref.shape[-1]
    ns2 = sr_ref.shape[-1]
    for g in range(u_ref.shape[0]):
        r = jnp.dot(u_ref[g], w_ref[g], preferred_element_type=F32)
        y_ref[g] = r[:, :width]
        sr_ref[0, g] = r[:, width:width + ns2]
        si_ref[0, g] = r[:, width + ns2:]


def _s5_scan_kernel(sr_ref, si_ref, ar_ref, ai_ref, hr_ref, hi_ref, fr, fi, br, bi, *, n_chunks, ctx_chunks):
    gb = ar_ref.shape[0]
    a_r = ar_ref[...]
    a_i = ai_ref[...]
    lane = lax.broadcasted_iota(jnp.int32, a_r.shape, 1)
    fwd = lane < (a_r.shape[1] // 2)

    def rows(ref, c):
        return ref[pl.ds(c, gb, stride=n_chunks), :]

    def body(i, carry):
        h_r, h_i = carry
        cf = i
        cb = jnp.where(i < ctx_chunks, ctx_chunks - 1 - i, n_chunks + ctx_chunks - 1 - i)
        fr[pl.ds(cf, gb, stride=n_chunks), :] = h_r
        fi[pl.ds(cf, gb, stride=n_chunks), :] = h_i
        br[pl.ds(cb, gb, stride=n_chunks), :] = h_r
        bi[pl.ds(cb, gb, stride=n_chunks), :] = h_i
        s_r = jnp.where(fwd, rows(sr_ref, cf), rows(sr_ref, cb))
        s_i = jnp.where(fwd, rows(si_ref, cf), rows(si_ref, cb))
        return (a_r * h_r - a_i * h_i + s_r, a_r * h_i + a_i * h_r + s_i)

    zero = jnp.zeros(a_r.shape, F32)
    lax.fori_loop(0, n_chunks, body, (zero, zero))
    full = lax.broadcasted_iota(jnp.int32, hr_ref.shape, 1) < (a_r.shape[1] // 2)
    hr_ref[...] = jnp.where(full, fr[...], br[...])
    hi_ref[...] = jnp.where(full, fi[...], bi[...])


def _s5_out_kernel(y_ref, hr_ref, hi_ref, w_ref, o_ref):
    ns2 = hr_ref.shape[-1]
    for g in range(y_ref.shape[0]):
        y = (y_ref[g]
             + jnp.dot(hr_ref[0, g].astype(BF16), w_ref[g, :ns2], preferred_element_type=F32)
             + jnp.dot(hi_ref[0, g].astype(BF16), w_ref[g, ns2:], preferred_element_type=F32))
        o_ref[g] = jax.nn.gelu(y).astype(o_ref.dtype)


def _s5_mix(u, tables, ctx_len):
    w1, w3, a_r, a_i = tables
    b, p, d = u.shape
    q, pch = SSM_CHUNK, SSM_GROUP
    g = d // pch
    nc = p // q
    width = q * pch
    ns2 = 2 * SSM_STATE
    gb = _pick(g, SSM_GROUP_BLOCK, 8)
    ut = u.reshape(b, nc, q, g, pch).transpose(3, 0, 1, 2, 4).reshape(g, b * nc, width)

    y_in, s_r, s_i = pl.pallas_call(
        _s5_in_kernel,
        out_shape=(jax.ShapeDtypeStruct((g, b * nc, width), F32),
                   jax.ShapeDtypeStruct((b, g, nc, ns2), F32),
                   jax.ShapeDtypeStruct((b, g, nc, ns2), F32)),
        grid=(g // gb, b),
        in_specs=[pl.BlockSpec((gb, nc, width), lambda gi, bb: (gi, bb, 0)),
                  pl.BlockSpec((gb, width, width + 2 * ns2), lambda gi, bb: (gi, 0, 0))],
        out_specs=(pl.BlockSpec((gb, nc, width), lambda gi, bb: (gi, bb, 0)),
                   pl.BlockSpec((1, gb, nc, ns2), lambda gi, bb: (bb, gi, 0, 0)),
                   pl.BlockSpec((1, gb, nc, ns2), lambda gi, bb: (bb, gi, 0, 0))),
        compiler_params=_cparams(("arbitrary", "arbitrary")),
        name="s5_chunk_in",
    )(ut, w1)

    flat = lambda gi, bb: (bb * (g // gb) + gi, 0)
    h_r, h_i = pl.pallas_call(
        functools.partial(_s5_scan_kernel, n_chunks=nc, ctx_chunks=ctx_len // q),
        out_shape=(jax.ShapeDtypeStruct((b * g * nc, ns2), F32),) * 2,
        grid=(g // gb, b),
        in_specs=[pl.BlockSpec((gb * nc, ns2), flat), pl.BlockSpec((gb * nc, ns2), flat),
                  pl.BlockSpec((gb, ns2), lambda gi, bb: (gi, 0)),
                  pl.BlockSpec((gb, ns2), lambda gi, bb: (gi, 0))],
        out_specs=(pl.BlockSpec((gb * nc, ns2), flat),) * 2,
        scratch_shapes=[pltpu.VMEM((gb * nc, ns2), F32)] * 4,
        compiler_params=_cparams(("arbitrary", "arbitrary")),
        name="s5_chunk_scan",
    )(s_r.reshape(b * g * nc, ns2), s_i.reshape(b * g * nc, ns2), a_r, a_i)

    yt = pl.pallas_call(
        _s5_out_kernel,
        out_shape=jax.ShapeDtypeStruct((g, b * nc, width), BF16),
        grid=(g // gb, b),
        in_specs=[pl.BlockSpec((gb, nc, width), lambda gi, bb: (gi, bb, 0)),
                  pl.BlockSpec((1, gb, nc, ns2), lambda gi, bb: (bb, gi, 0, 0)),
                  pl.BlockSpec((1, gb, nc, ns2), lambda gi, bb: (bb, gi, 0, 0)),
                  pl.BlockSpec((gb, 2 * ns2, width), lambda gi, bb: (gi, 0, 0))],
        out_specs=pl.BlockSpec((gb, nc, width), lambda gi, bb: (gi, bb, 0)),
        compiler_params=_cparams(("arbitrary", "arbitrary")),
        name="s5_chunk_out",
    )(y_in, h_r.reshape(b, g, nc, ns2), h_i.reshape(b, g, nc, ns2), w3)
    return yt.reshape(g, b, nc, q, pch).transpose(1, 2, 3, 0, 4).reshape(b, p, d)


def _moe_schedule(idx, n_exp, tm, n_tiles):
    t = idx.shape[0]
    e_flat = idx.reshape(-1)
    n_asg = e_flat.shape[0]
    order = jnp.argsort(e_flat, stable=True).astype(jnp.int32)
    counts = jnp.sum((e_flat[:, None] == jnp.arange(n_exp)[None, :]).astype(jnp.int32), axis=0)
    padded = (counts + tm - 1) // tm * tm
    gstart = jnp.cumsum(padded) - padded
    cstart = jnp.cumsum(counts) - counts
    gend = gstart + padded
    n_active = (jnp.sum(padded) // tm).astype(jnp.int32)
    tile_start = jnp.arange(n_tiles, dtype=jnp.int32) * tm
    tile_e = jnp.sum((gend[None, :] <= tile_start[:, None]).astype(jnp.int32), axis=1)
    tile_e = jnp.minimum(tile_e, n_exp - 1)
    tile_e = jnp.where(jnp.arange(n_tiles) < n_active, tile_e, tile_e[jnp.maximum(n_active - 1, 0)])
    row_e = jnp.repeat(tile_e, tm)
    j = jnp.arange(n_tiles * tm, dtype=jnp.int32) - gstart[row_e].astype(jnp.int32)
    valid = (j < counts[row_e]) & (jnp.arange(n_tiles * tm) < n_active * tm)
    sidx = jnp.clip(cstart[row_e] + j, 0, n_asg - 1)
    row_token = jnp.where(valid, order[sidx] // TOP_K, 0).astype(jnp.int32)
    sorted_e = e_flat[order]
    dest = (gstart[sorted_e] + jnp.arange(n_asg) - cstart[sorted_e]).astype(jnp.int32)
    inv = jnp.argsort(order).astype(jnp.int32)
    pos = dest[inv].reshape(t, TOP_K)
    return tile_e, n_active.reshape(1), row_token, pos


def _moe_prep_kernel(w_ref, sel_ref, o_ref):
    sel = sel_ref[...]
    blk = sel.shape[0]
    per = o_ref.shape[3] // blk
    for c in range(w_ref.shape[3] // blk):
        w = w_ref[0, 0, :, c * blk:(c + 1) * blk].astype(BF16)
        o_ref[0, c // per, :, (c % per) * blk:(c % per + 1) * blk] = jnp.dot(
            w, sel, preferred_element_type=F32).astype(BF16)


def _moe_chunk_width(n):
    return 4 * LANES if n % (4 * LANES) == 0 else 2 * LANES


def _moe_prep(w_gate_up, layer):
    _, n_exp, d, n = w_gate_up.shape
    blk = 2 * LANES
    cw = _moe_chunk_width(n)
    tk = _pick(d, 512, 8)
    src = jnp.arange(blk)
    sel = (jnp.arange(blk)[None, :] == (src // 2 + (src % 2) * LANES)[:, None]).astype(BF16)
    return pl.pallas_call(
        _moe_prep_kernel,
        out_shape=jax.ShapeDtypeStruct((n_exp, n // cw, d, cw), BF16),
        grid=(n_exp, d // tk),
        in_specs=[pl.BlockSpec((1, 1, tk, n), lambda e, k: (layer, e, k, 0)),
                  pl.BlockSpec((blk, blk), lambda e, k: (0, 0))],
        out_specs=pl.BlockSpec((1, n // cw, tk, cw), lambda e, k: (e, 0, k, 0)),
        compiler_params=_cparams(("arbitrary", "arbitrary")),
        name="moe_weight_prep",
    )(w_gate_up, sel)


def _cast_kernel(w_ref, o_ref):
    o_ref[0] = w_ref[0, 0].astype(o_ref.dtype)


def _moe_cast(w_down, layer):
    _, n_exp, f, d = w_down.shape
    tk = _pick(f, 512, 8)
    return pl.pallas_call(
        _cast_kernel,
        out_shape=jax.ShapeDtypeStruct((n_exp, f, d), BF16),
        grid=(n_exp, f // tk),
        in_specs=[pl.BlockSpec((1, 1, tk, d), lambda e, k: (layer, e, k, 0))],
        out_specs=pl.BlockSpec((1, tk, d), lambda e, k: (e, k, 0)),
        compiler_params=_cparams(("arbitrary", "arbitrary")),
        name="moe_weight_cast",
    )(w_down)


def _split_blocks(v):
    lead = v.shape[:-1]
    return v.reshape(lead + (-1, LANES, 2)).swapaxes(-1, -2).reshape(lead + (-1,))


def _moe_kernel(te_ref, na_ref, tok_ref, tokn_ref, x_hbm, w_ref, b_ref, wd_ref, bd_ref, o_ref,
                xbuf, xb, act, sem, *, tm):
    i = pl.program_id(0)
    n_act = na_ref[0]
    slot = i % 2
    n_chunks = w_ref.shape[1]
    rows_per = tm // n_chunks

    def row_copy(tok, r, s):
        return pltpu.make_async_copy(x_hbm.at[pl.ds(tok, 1)], xbuf.at[s, pl.ds(r, 1)], sem.at[s])

    def wait_rows(s):
        def wbody(r, c):
            row_copy(0, 0, s).wait()
            return c
        lax.fori_loop(0, tm, wbody, 0, unroll=8)

    @pl.when(i == 0)
    def _():
        def body(r, c):
            row_copy(tok_ref[0, 0, r], r, 0).start()
            return c
        lax.fori_loop(0, tm, body, 0, unroll=8)

    @pl.when(i < n_act)
    def _():
        wait_rows(slot)
        lo, hi = _unpack_halves(xbuf[slot])
        half = lo.shape[1]
        xb[:, :half] = lo.astype(BF16)
        xb[:, half:] = hi.astype(BF16)

        def chunk(c, carry):
            for rr in range(rows_per):
                r = c * rows_per + rr
                row_copy(tokn_ref[0, 0, r], r, 1 - slot).start()
            h = jnp.dot(xb[...], w_ref[0, c], preferred_element_type=F32) + b_ref[0, c]
            acts = []
            for q in range(h.shape[1] // (2 * LANES)):
                gate = jnp.minimum(h[:, 2 * q * LANES:(2 * q + 1) * LANES], SWIGLU_LIMIT)
                up = jnp.clip(h[:, (2 * q + 1) * LANES:(2 * q + 2) * LANES], -SWIGLU_LIMIT, SWIGLU_LIMIT)
                acts.append(((up + 1.0) * (gate * jax.nn.sigmoid(SWIGLU_ALPHA * gate))).astype(BF16))
            act[c] = acts[0] if len(acts) == 1 else jnp.concatenate(acts, axis=1)
            return carry
        lax.fori_loop(0, n_chunks, chunk, 0)

        a = act[0] if n_chunks == 1 else jnp.concatenate([act[c] for c in range(n_chunks)], axis=1)
        y = jnp.dot(a, wd_ref[0], preferred_element_type=F32) + bd_ref[0]
        o_ref[...] = _pack_halves(y)

    @pl.when(i == n_act)
    def _():
        wait_rows(slot)

    @pl.when(i >= n_act)
    def _():
        o_ref[...] = jnp.zeros(o_ref.shape, o_ref.dtype)


def _moe_experts(hp2d, row_token, tile_e, n_active, wp, bp, wd, bd, tm):
    half = hp2d.shape[-1]
    n_exp, n_chunks, d, cw = wp.shape
    f = wd.shape[1]
    n_tiles = row_token.shape[0] // tm
    tok3 = row_token.reshape(n_tiles, 1, tm)
    exp3 = lambda i, te, na: (te[i], 0, 0)
    exp4 = lambda i, te, na: (te[i], 0, 0, 0)
    return pl.pallas_call(
        functools.partial(_moe_kernel, tm=tm),
        out_shape=jax.ShapeDtypeStruct((n_tiles * tm, half), jnp.uint32),
        grid_spec=pltpu.PrefetchScalarGridSpec(
            num_scalar_prefetch=2,
            grid=(n_tiles,),
            in_specs=[pl.BlockSpec((1, 1, tm), lambda i, te, na: (i, 0, 0), memory_space=pltpu.SMEM),
                      pl.BlockSpec((1, 1, tm), lambda i, te, na: (jnp.minimum(i + 1, n_tiles - 1), 0, 0),
                                   memory_space=pltpu.SMEM),
                      pl.BlockSpec(memory_space=pl.ANY),
                      pl.BlockSpec((1, n_chunks, d, cw), exp4), pl.BlockSpec((1, n_chunks, 1, cw), exp4),
                      pl.BlockSpec((1, f, d), exp3), pl.BlockSpec((1, 1, d), exp3)],
            out_specs=pl.BlockSpec((tm, half), lambda i, te, na: (i, 0)),
            scratch_shapes=[pltpu.VMEM((2, tm, half), jnp.uint32), pltpu.VMEM((tm, d), BF16),
                            pltpu.VMEM((n_chunks, tm, cw // 2), BF16), pltpu.SemaphoreType.DMA((2,))]),
        compiler_params=_cparams(("arbitrary",)),
        name="moe_experts",
    )(tile_e, n_active, tok3, tok3, hp2d, wp, bp, wd, bd)


def _combine_kernel(pos_ref, posn_ref, gw_ref, x_ref, g_ref, y_hbm, o_ref, buf, sem, *, rows):
    i = pl.program_id(0)
    n = pl.num_programs(0)

    slot = i % 2

    def row_copy(src, k, r, s):
        return pltpu.make_async_copy(y_hbm.at[pl.ds(src, 1)], buf.at[s, k, pl.ds(r, 1)], sem.at[s])

    @pl.when(i == 0)
    def _():
        def body(r, c):
            for k in range(TOP_K):
                row_copy(pos_ref[0, 0, r * TOP_K + k], k, r, 0).start()
            return c
        lax.fori_loop(0, rows, body, 0, unroll=2)

    def wbody(r, c):
        row_copy(0, 0, 0, slot).wait()
        return c
    lax.fori_loop(0, rows * TOP_K, wbody, 0, unroll=8)

    def combine():
        gw = gw_ref[...]
        half = buf.shape[-1]
        acc_lo = acc_hi = None
        for k in range(TOP_K):
            lo, hi = _unpack_halves(buf[slot, k])
            w = gw[:, k:k + 1]
            acc_lo = w * lo if k == 0 else acc_lo + w * lo
            acc_hi = w * hi if k == 0 else acc_hi + w * hi
        g = g_ref[0]
        o_ref[:, :half] = x_ref[:, :half] + g[:, :half] * acc_lo
        o_ref[:, half:] = x_ref[:, half:] + g[:, half:] * acc_hi

    @pl.when(i + 1 < n)
    def _():
        for r in range(rows):
            for k in range(TOP_K):
                row_copy(posn_ref[0, 0, r * TOP_K + k], k, r, 1 - slot).start()
        combine()

    @pl.when(i + 1 == n)
    def _():
        combine()


def _moe_combine(y_sorted, pos, gw2d, x2d, modt, tm_row, ctx_tiles, tiles_per_batch):
    t, d = x2d.shape
    rows = _pick(tm_row, COMBINE_ROWS, 8)
    n_steps = t // rows
    sub = tm_row // rows
    pos3 = pos.reshape(n_steps, 1, rows * TOP_K)

    def gmap(i):
        tile = i // sub
        bb = tile // tiles_per_batch
        seg = jnp.where(tile % tiles_per_batch >= ctx_tiles, 1, 0)
        return (bb * 2 + seg, 0, 5)

    return pl.pallas_call(
        functools.partial(_combine_kernel, rows=rows),
        out_shape=jax.ShapeDtypeStruct((t, d), F32),
        grid=(n_steps,),
        in_specs=[pl.BlockSpec((1, 1, rows * TOP_K), lambda i: (i, 0, 0), memory_space=pltpu.SMEM),
                  pl.BlockSpec((1, 1, rows * TOP_K), lambda i: (jnp.minimum(i + 1, n_steps - 1), 0, 0),
                               memory_space=pltpu.SMEM),
                  pl.BlockSpec((rows, LANES), lambda i: (i, 0)),
                  pl.BlockSpec((rows, d), lambda i: (i, 0)),
                  pl.BlockSpec((1, 1, d), gmap),
                  pl.BlockSpec(memory_space=pl.ANY)],
        out_specs=pl.BlockSpec((rows, d), lambda i: (i, 0)),
        scratch_shapes=[pltpu.VMEM((2, TOP_K, rows, d // 2), jnp.uint32), pltpu.SemaphoreType.DMA((2,))],
        compiler_params=_cparams(("arbitrary",)),
        name="moe_combine",
    )(pos3, pos3, gw2d, x2d, modt, y_sorted)


def _rope_tables(seq, ctx_len):
    n_rows = seq // GRID_W
    axis_rot = HEAD_DIM // 2
    rows = jnp.repeat(jnp.arange(n_rows, dtype=F32), GRID_W)
    cols = jnp.tile(jnp.arange(GRID_W, dtype=F32), n_rows)
    inv_freq = ROPE_BASE ** (-jnp.arange(0, axis_rot, 2, dtype=F32) / axis_rot)
    ang_r = rows[:, None] * inv_freq
    ang_c = cols[:, None] * inv_freq
    ang = jnp.concatenate([ang_r, ang_r, ang_c, ang_c], axis=-1)
    sign = jnp.tile(jnp.concatenate([-jnp.ones(axis_rot // 2, F32), jnp.ones(axis_rot // 2, F32)]), 2)
    cos = jnp.concatenate([jnp.ones((ctx_len, HEAD_DIM), F32), jnp.cos(ang)], axis=0)
    sin = jnp.concatenate([jnp.zeros((ctx_len, HEAD_DIM), F32), jnp.sin(ang) * sign], axis=0)
    return cos, sin


def kernel(x, c, ctx, c_ctx, w_mod, b_mod, norm_mix, norm_ffn, w_router, b_router, w_gate_up, b_gate_up, w_down, b_down, attn_w_qkv, attn_w_o, attn_q_gain, attn_k_gain, attn_sinks, ssm_a_re, ssm_a_im, ssm_log_dt, ssm_b_re, ssm_b_im, ssm_c_re, ssm_c_im, ssm_d, ssm_w_glu, ssm_b_glu):
    b, seq, d = x.shape
    ctx_len = ctx.shape[1]
    depth = w_mod.shape[0]
    p = ctx_len + seq
    t = b * p
    n_exp = w_router.shape[-1]
    n_q = d // HEAD_DIM
    n_kv = (attn_w_qkv.shape[-1] // HEAD_DIM - n_q) // 2
    assert ctx_len % ATT_BLOCK == 0 and seq % ATT_BLOCK == 0 and seq % GRID_W == 0
    assert d % (2 * LANES) == 0 and ctx_len % SSM_CHUNK == 0 and seq % SSM_CHUNK == 0

    tm = _pick(math.gcd(p, ctx_len), ROW_TILE)
    mm_tm = _pick(math.gcd(p, ctx_len), MM_TILE_M)
    tiles_pb = p // mm_tm
    ctx_tiles = ctx_len // mm_tm

    pad = (-(b + 1)) % 8
    cvec = jnp.concatenate([c, c_ctx[None, :], jnp.zeros((pad, d), F32)], axis=0)
    mod_all = _mod_all(cvec, w_mod, b_mod)

    cos_t, sin_t = _rope_tables(seq, ctx_len)
    xs = jnp.concatenate([ctx, x], axis=1)

    tm_moe = min(MOE_TILE, _pick(t * TOP_K, MOE_TILE, 8))
    n_tiles = (t * TOP_K) // tm_moe + n_exp
    scale = HEAD_DIM ** -0.5

    for i in range(depth):
        j = i // 2
        lat = mod_all[i, :b]
        cx = jnp.broadcast_to(mod_all[i, b][None], lat.shape)
        modt = jnp.stack([cx, lat], axis=1).reshape(b * 2, 1, 6 * d)

        x2d = xs.reshape(t, d)
        if i % 2 == 0:
            gain = jnp.concatenate([jnp.tile(attn_q_gain[j] * scale, n_q), jnp.tile(attn_k_gain[j], n_kv),
                                    jnp.ones((n_kv * HEAD_DIM,), F32)]).reshape(1, -1)
            qkv = _qkv_proj(x2d, norm_mix[i], modt, attn_w_qkv[j].astype(BF16), gain, cos_t, sin_t,
                            n_q + n_kv, mm_tm, ctx_tiles, tiles_pb)
            o = _attention(qkv.reshape(b, p, -1), attn_sinks[j].astype(F32), n_q, n_kv, ctx_len)
            x2d = _proj_residual(o.reshape(t, d), attn_w_o[j].astype(BF16), x2d, modt, 2,
                                 mm_tm, ctx_tiles, tiles_pb)
        else:
            u = _norm_mod(xs, norm_mix[i], modt, 0, ctx_len)
            tables = _s5_tables(ssm_a_re[j], ssm_a_im[j], ssm_log_dt[j], ssm_b_re[j], ssm_b_im[j],
                                ssm_c_re[j], ssm_c_im[j], ssm_d[j])
            gy = _s5_mix(u, tables, ctx_len)
            x2d = _glu_residual(gy.reshape(t, d), ssm_w_glu[j].astype(BF16), ssm_b_glu[j], x2d, modt, 2,
                                mm_tm, ctx_tiles, tiles_pb)

        hp, idx, gw = _norm_router(x2d.reshape(b, p, d), norm_ffn[i], modt, w_router[i], b_router[i], ctx_len)
        idx2d = idx.reshape(t, LANES)[:, :TOP_K]
        tile_e, n_active, row_token, pos = _moe_schedule(idx2d, n_exp, tm_moe, n_tiles)
        wp = _moe_prep(w_gate_up, i)
        bp = _split_blocks(b_gate_up[i]).reshape(n_exp, wp.shape[1], 1, wp.shape[3])
        y_sorted = _moe_experts(hp.reshape(t, d // 2), row_token, tile_e, n_active, wp, bp,
                                _moe_cast(w_down, i), b_down[i].reshape(n_exp, 1, d), tm_moe)
        x2d = _moe_combine(y_sorted, pos, gw.reshape(t, LANES), x2d, modt, tm, ctx_len // tm, p // tm)
        xs = x2d.reshape(b, p, d)

    return xs[:, ctx_len:]
```

```python
import functools
import math

import jax
import jax.numpy as jnp
from jax import lax
from jax.experimental import pallas as pl
from jax.experimental.pallas import tpu as pltpu

F32 = jnp.float32
BF16 = jnp.bfloat16

HEAD_DIM = 128
ATT_BLOCK = 128
GRID_W = 64
ROPE_BASE = 10000.0
SSM_GROUP = 16
SSM_STATE = 64
SSM_CHUNK = 16
TOP_K = 4
SWIGLU_LIMIT = 7.0
SWIGLU_ALPHA = 1.702
EPS = 1e-6
NEG_INF = -1e30
DT_FLOOR_RE = -1e-4

LANES = 128
VMEM_LIMIT = 56 * 1024 * 1024

ROW_TILE = 256
MM_TILE_M = 512
PROJ_TILE_N = 2048
GLU_TILE_N = 1024
MOE_TILE = 512
COMBINE_ROWS = 128
SSM_GROUP_BLOCK = 16


def _cparams(sem):
    return pltpu.CompilerParams(dimension_semantics=sem, vmem_limit_bytes=VMEM_LIMIT)


def _pick(n, pref, mult=8):
    t = min(n, pref)
    while n % t or t % mult:
        t -= 1
    return t


def _mod_kernel(c_ref, w_ref, b_ref, o_ref):
    c = c_ref[...]
    s = c * jax.nn.sigmoid(c)
    o_ref[0] = jnp.dot(s.astype(BF16), w_ref[0].astype(BF16), preferred_element_type=F32) + b_ref[0]


def _mod_all(cvec, w_mod, b_mod):
    depth, d, n = w_mod.shape
    tn = _pick(n, 1024, LANES)
    return pl.pallas_call(
        _mod_kernel,
        out_shape=jax.ShapeDtypeStruct((depth, cvec.shape[0], n), F32),
        grid=(depth, n // tn),
        in_specs=[pl.BlockSpec(cvec.shape, lambda l, j: (0, 0)),
                  pl.BlockSpec((1, d, tn), lambda l, j: (l, 0, j)),
                  pl.BlockSpec((1, 1, tn), lambda l, j: (l, 0, j))],
        out_specs=pl.BlockSpec((1, cvec.shape[0], tn), lambda l, j: (l, 0, j)),
        compiler_params=_cparams(("arbitrary", "arbitrary")),
        name="adaln_rows",
    )(cvec, w_mod, b_mod.reshape(depth, 1, n))


def _normed(x, w, sh, sc):
    r = lax.rsqrt(jnp.mean(x * x, axis=-1, keepdims=True) + EPS)
    return (x * r * w) * (1.0 + sc) + sh


def _norm_mod_kernel(x_ref, w_ref, sh_ref, sc_ref, o_ref):
    o_ref[0] = _normed(x_ref[0], w_ref[...], sh_ref[0], sc_ref[0]).astype(o_ref.dtype)


def _mod_spec(d, which, ctx_tiles):
    return pl.BlockSpec((1, 1, d), lambda b, i: (b * 2 + jnp.where(i >= ctx_tiles, 1, 0), 0, which))


def _norm_mod(x, w, modt, which_shift, ctx_len):
    b, p, d = x.shape
    tm = _pick(math.gcd(p, ctx_len), ROW_TILE)
    ct = ctx_len // tm
    return pl.pallas_call(
        _norm_mod_kernel,
        out_shape=jax.ShapeDtypeStruct((b, p, d), BF16),
        grid=(b, p // tm),
        in_specs=[pl.BlockSpec((1, tm, d), lambda bb, i: (bb, i, 0)),
                  pl.BlockSpec((1, d), lambda bb, i: (0, 0)),
                  _mod_spec(d, which_shift, ct), _mod_spec(d, which_shift + 1, ct)],
        out_specs=pl.BlockSpec((1, tm, d), lambda bb, i: (bb, i, 0)),
        compiler_params=_cparams(("arbitrary", "arbitrary")),
        name="norm_mod",
    )(x, w.reshape(1, d), modt, modt)


def _pack_halves(h):
    half = h.shape[-1] // 2
    bits = lax.bitcast_convert_type(h.astype(BF16).astype(F32), jnp.uint32)
    return (bits[:, :half] >> 16) | (bits[:, half:] & jnp.uint32(0xFFFF0000))


def _unpack_halves(xp):
    lo = lax.bitcast_convert_type(xp << 16, F32)
    hi = lax.bitcast_convert_type(xp & jnp.uint32(0xFFFF0000), F32)
    return lo, hi


def _norm_router_kernel(x_ref, w_ref, sh_ref, sc_ref, wr_ref, br_ref, hp_ref, idx_ref, gw_ref):
    h = _normed(x_ref[0], w_ref[...], sh_ref[0], sc_ref[0])
    hp_ref[0, :, 0, :] = _pack_halves(h)
    logits = jnp.dot(h, wr_ref[...], precision=lax.Precision.HIGHEST,
                     preferred_element_type=F32) + br_ref[...]
    n_exp = logits.shape[-1]
    lane = lax.broadcasted_iota(jnp.int32, logits.shape, 1)
    out_lane = lax.broadcasted_iota(jnp.int32, idx_ref.shape[1:], 1)
    vals, idxs = [], []
    rest = logits
    for _ in range(TOP_K):
        m = jnp.max(rest, axis=-1, keepdims=True)
        idx = jnp.min(jnp.where(rest == m, lane, n_exp), axis=-1, keepdims=True)
        vals.append(m)
        idxs.append(idx)
        rest = jnp.where(lane == idx, -jnp.inf, rest)
    exps = [jnp.exp(v - vals[0]) for v in vals]
    tot = exps[0]
    for e in exps[1:]:
        tot = tot + e
    idx_out = jnp.zeros(idx_ref.shape[1:], jnp.int32)
    gw_out = jnp.zeros(gw_ref.shape[1:], F32)
    for k in range(TOP_K):
        idx_out = jnp.where(out_lane == k, idxs[k], idx_out)
        gw_out = jnp.where(out_lane == k, exps[k] / tot, gw_out)
    idx_ref[0] = idx_out
    gw_ref[0] = gw_out


def _norm_router(x, w, modt, w_router, b_router, ctx_len):
    b, p, d = x.shape
    n_exp = w_router.shape[-1]
    tm = _pick(math.gcd(p, ctx_len), ROW_TILE)
    ct = ctx_len // tm
    row = lambda bb, i: (bb, i, 0)
    return pl.pallas_call(
        _norm_router_kernel,
        out_shape=(jax.ShapeDtypeStruct((b, p, 1, d // 2), jnp.uint32),
                   jax.ShapeDtypeStruct((b, p, LANES), jnp.int32),
                   jax.ShapeDtypeStruct((b, p, LANES), F32)),
        grid=(b, p // tm),
        in_specs=[pl.BlockSpec((1, tm, d), row),
                  pl.BlockSpec((1, d), lambda bb, i: (0, 0)),
                  _mod_spec(d, 3, ct), _mod_spec(d, 4, ct),
                  pl.BlockSpec((d, n_exp), lambda bb, i: (0, 0)),
                  pl.BlockSpec((1, n_exp), lambda bb, i: (0, 0))],
        out_specs=(pl.BlockSpec((1, tm, 1, d // 2), lambda bb, i: (bb, i, 0, 0)),
                   pl.BlockSpec((1, tm, LANES), row),
                   pl.BlockSpec((1, tm, LANES), row)),
        compiler_params=_cparams(("arbitrary", "arbitrary")),
        name="norm_router",
    )(x, w.reshape(1, d), modt, modt, w_router, b_router.reshape(1, n_exp))


def _qkv_kernel(x_ref, nw_ref, sh_ref, sc_ref, w_ref, g_ref, cos_ref, sin_ref, o_ref, *, n_norm_heads):
    u = _normed(x_ref[...], nw_ref[...], sh_ref[0], sc_ref[0]).astype(BF16)
    acc = jnp.dot(u, w_ref[...], preferred_element_type=F32)
    cos = cos_ref[...]
    sin = sin_ref[...]
    lane = lax.broadcasted_iota(jnp.int32, cos.shape, 1)
    first = (lane // (HEAD_DIM // 4)) % 2 == 0
    for hh in range(acc.shape[1] // HEAD_DIM):
        sl = slice(hh * HEAD_DIM, (hh + 1) * HEAD_DIM)
        xh = acc[:, sl]
        if hh < n_norm_heads:
            r = lax.rsqrt(jnp.mean(xh * xh, axis=-1, keepdims=True) + EPS)
            y = xh * r * g_ref[:, sl]
            partner = jnp.where(first, pltpu.roll(y, HEAD_DIM - HEAD_DIM // 4, 1),
                                pltpu.roll(y, HEAD_DIM // 4, 1))
            xh = y * cos + partner * sin
        o_ref[:, sl] = xh.astype(o_ref.dtype)


def _row_mod_spec(d, which, ctx_tiles, tiles_per_batch):
    def imap(i):
        seg = jnp.where(i % tiles_per_batch >= ctx_tiles, 1, 0)
        return ((i // tiles_per_batch) * 2 + seg, 0, which)

    return pl.BlockSpec((1, 1, d), imap)


def _qkv_proj(x2d, norm_w, modt, w_bf16, gain_row, cos_t, sin_t, n_norm_heads, tm, ctx_tiles, tiles_per_batch):
    m, k = x2d.shape
    n = w_bf16.shape[1]
    pos = lambda i: (i % tiles_per_batch, 0)
    return pl.pallas_call(
        functools.partial(_qkv_kernel, n_norm_heads=n_norm_heads),
        out_shape=jax.ShapeDtypeStruct((m, n), BF16),
        grid=(m // tm,),
        in_specs=[pl.BlockSpec((tm, k), lambda i: (i, 0)),
                  pl.BlockSpec((1, k), lambda i: (0, 0)),
                  _row_mod_spec(k, 0, ctx_tiles, tiles_per_batch),
                  _row_mod_spec(k, 1, ctx_tiles, tiles_per_batch),
                  pl.BlockSpec((k, n), lambda i: (0, 0)),
                  pl.BlockSpec((1, n), lambda i: (0, 0)),
                  pl.BlockSpec((tm, HEAD_DIM), pos),
                  pl.BlockSpec((tm, HEAD_DIM), pos)],
        out_specs=pl.BlockSpec((tm, n), lambda i: (i, 0)),
        compiler_params=_cparams(("arbitrary",)),
        name="qkv_proj",
    )(x2d, norm_w.reshape(1, k), modt, modt, w_bf16, gain_row, cos_t, sin_t)


def _attn_kernel(sink_ref, q_ref, kc_ref, vc_ref, kp_ref, ks_ref, kn_ref, vp_ref, vs_ref, vn_ref,
                 o_ref, *, q_per_kv, ctx_blocks, n_blocks):
    h = pl.program_id(1)
    i = pl.program_id(2)
    blk = ATT_BLOCK
    q = q_ref[0]
    qs = jnp.concatenate([q[:, g * HEAD_DIM:(g + 1) * HEAD_DIM] for g in range(q_per_kv)], axis=0)
    rows = qs.shape[0]
    dn = (((1,), (1,)), ((), ()))
    k_loc = jnp.concatenate([kp_ref[0], ks_ref[0], kn_ref[0]], axis=0)
    v_loc = jnp.concatenate([vp_ref[0], vs_ref[0], vn_ref[0]], axis=0)
    s_loc = lax.dot_general(qs, k_loc, dn, preferred_element_type=F32)
    s_ctx = lax.dot_general(qs, kc_ref[0], dn, preferred_element_type=F32)

    qi = lax.broadcasted_iota(jnp.int32, s_loc.shape, 0) % blk
    kj = lax.broadcasted_iota(jnp.int32, s_loc.shape, 1)
    is_lat = i >= ctx_blocks
    ok_prev = (kj < blk) & (kj >= qi) & (i > ctx_blocks)
    ok_self = (kj >= blk) & (kj < 2 * blk)
    ok_next = (kj >= 2 * blk) & (kj - 2 * blk <= qi) & (i < n_blocks - 1)
    valid = (ok_prev | ok_self | ok_next) & is_lat
    s_loc = jnp.where(valid, s_loc, NEG_INF)

    row_head = lax.broadcasted_iota(jnp.int32, (rows, 1), 0) // blk
    sink = jnp.zeros((rows, 1), F32)
    for g in range(q_per_kv):
        sink = jnp.where(row_head == g, sink_ref[h * q_per_kv + g], sink)

    m = jnp.maximum(jnp.maximum(jnp.max(s_loc, axis=-1, keepdims=True),
                                jnp.max(s_ctx, axis=-1, keepdims=True)), sink)
    p_loc = jnp.exp(s_loc - m)
    p_ctx = jnp.exp(s_ctx - m)
    denom = (jnp.sum(p_loc, axis=-1, keepdims=True) + jnp.sum(p_ctx, axis=-1, keepdims=True)
             + jnp.exp(sink - m))
    o = (jnp.dot(p_loc.astype(BF16), v_loc, preferred_element_type=F32)
         + jnp.dot(p_ctx.astype(BF16), vc_ref[0], preferred_element_type=F32)) / denom
    o = o.astype(o_ref.dtype)
    for g in range(q_per_kv):
        o_ref[0, :, g * HEAD_DIM:(g + 1) * HEAD_DIM] = o[g * blk:(g + 1) * blk]


def _attention(qkv, sinks, n_q, n_kv, ctx_len):
    b, p, _ = qkv.shape
    blk = ATT_BLOCK
    nb = p // blk
    cb = ctx_len // blk
    qpk = n_q // n_kv
    qw = qpk * HEAD_DIM
    kcol = n_q
    vcol = n_q + n_kv

    def loc(col0, off):
        return pl.BlockSpec(
            (1, blk, HEAD_DIM),
            lambda bb, h, i, s: (bb, jnp.clip(i + off, cb, nb - 1), col0 + h))

    return pl.pallas_call(
        functools.partial(_attn_kernel, q_per_kv=qpk, ctx_blocks=cb, n_blocks=nb),
        out_shape=jax.ShapeDtypeStruct((b, p, n_q * HEAD_DIM), BF16),
        grid_spec=pltpu.PrefetchScalarGridSpec(
            num_scalar_prefetch=1,
            grid=(b, n_kv, nb),
            in_specs=[pl.BlockSpec((1, blk, qw), lambda bb, h, i, s: (bb, i, h)),
                      pl.BlockSpec((1, ctx_len, HEAD_DIM), lambda bb, h, i, s: (bb, 0, kcol + h)),
                      pl.BlockSpec((1, ctx_len, HEAD_DIM), lambda bb, h, i, s: (bb, 0, vcol + h)),
                      loc(kcol, -1), loc(kcol, 0), loc(kcol, 1),
                      loc(vcol, -1), loc(vcol, 0), loc(vcol, 1)],
            out_specs=pl.BlockSpec((1, blk, qw), lambda bb, h, i, s: (bb, i, h))),
        compiler_params=_cparams(("arbitrary", "arbitrary", "arbitrary")),
        name="window_attn",
    )(sinks, qkv, qkv, qkv, qkv, qkv, qkv, qkv, qkv, qkv)


def _gate_spec(d, which, ctx_tiles, tiles_per_batch, tn):
    per = d // tn

    def imap(j, i):
        bb = i // tiles_per_batch
        seg = jnp.where(i % tiles_per_batch >= ctx_tiles, 1, 0)
        return (bb * 2 + seg, 0, which * per + j)

    return pl.BlockSpec((1, 1, tn), imap)


def _proj_res_kernel(a_ref, w_ref, x_ref, g_ref, o_ref):
    y = jnp.dot(a_ref[...], w_ref[...], preferred_element_type=F32)
    o_ref[...] = x_ref[...] + g_ref[0] * y


def _proj_residual(a2d, w_bf16, x2d, modt, which_gate, tm, ctx_tiles, tiles_per_batch):
    m, k = a2d.shape
    n = w_bf16.shape[1]
    tn = _pick(n, PROJ_TILE_N, LANES)
    return pl.pallas_call(
        _proj_res_kernel,
        out_shape=jax.ShapeDtypeStruct((m, n), F32),
        grid=(n // tn, m // tm),
        in_specs=[pl.BlockSpec((tm, k), lambda j, i: (i, 0)),
                  pl.BlockSpec((k, tn), lambda j, i: (0, j)),
                  pl.BlockSpec((tm, tn), lambda j, i: (i, j)),
                  _gate_spec(n, which_gate, ctx_tiles, tiles_per_batch, tn)],
        out_specs=pl.BlockSpec((tm, tn), lambda j, i: (i, j)),
        compiler_params=_cparams(("arbitrary", "arbitrary")),
        name="proj_residual",
    )(a2d, w_bf16, x2d, modt)


def _glu_res_kernel(a_ref, w1_ref, w2_ref, b1_ref, b2_ref, x_ref, g_ref, o_ref):
    a = a_ref[...]
    z1 = jnp.dot(a, w1_ref[...], preferred_element_type=F32) + b1_ref[...]
    z2 = jnp.dot(a, w2_ref[...], preferred_element_type=F32) + b2_ref[...]
    o_ref[...] = x_ref[...] + g_ref[0] * (z1 * jax.nn.sigmoid(z2))


def _glu_residual(a2d, w_bf16, b_glu, x2d, modt, which_gate, tm, ctx_tiles, tiles_per_batch):
    m, k = a2d.shape
    d = w_bf16.shape[1] // 2
    tn = _pick(d, GLU_TILE_N, LANES)
    nj = d // tn
    b2 = b_glu.reshape(1, 2 * d)
    return pl.pallas_call(
        _glu_res_kernel,
        out_shape=jax.ShapeDtypeStruct((m, d), F32),
        grid=(nj, m // tm),
        in_specs=[pl.BlockSpec((tm, k), lambda j, i: (i, 0)),
                  pl.BlockSpec((k, tn), lambda j, i: (0, j)),
                  pl.BlockSpec((k, tn), lambda j, i: (0, j + nj)),
                  pl.BlockSpec((1, tn), lambda j, i: (0, j)),
                  pl.BlockSpec((1, tn), lambda j, i: (0, j + nj)),
                  pl.BlockSpec((tm, tn), lambda j, i: (i, j)),
                  _gate_spec(d, which_gate, ctx_tiles, tiles_per_batch, tn)],
        out_specs=pl.BlockSpec((tm, tn), lambda j, i: (i, j)),
        compiler_params=_cparams(("arbitrary", "arbitrary")),
        name="glu_residual",
    )(a2d, w_bf16, w_bf16, b2, b2, x2d, modt)


def _s5_tables(a_re, a_im, log_dt, b_re, b_im, c_re, c_im, d_skip):
    q, n_state, pch = SSM_CHUNK, a_re.shape[-1], b_re.shape[-1]
    g = a_re.shape[1]
    hp = lax.Precision.HIGHEST
    lam = lax.complex(jnp.minimum(a_re.astype(F32), DT_FLOOR_RE), a_im.astype(F32))
    lam_dt = lam * jnp.exp(log_dt.astype(F32))[..., None]
    b_bar = ((jnp.exp(lam_dt) - 1.0) / lam)[..., None] * lax.complex(b_re.astype(F32), b_im.astype(F32))
    c_mat = lax.complex(c_re.astype(F32), c_im.astype(F32))
    tau = jnp.arange(q + 1, dtype=F32)
    pw = jnp.exp(lam_dt[:, :, None, :] * tau[None, None, :, None])
    kern = jnp.einsum('dgpn,dgtn,dgnr->dgtpr', c_mat, pw[:, :, :q], b_bar, precision=hp).real
    t_i = jnp.arange(q)[:, None]
    s_i = jnp.arange(q)[None, :]
    kf = jnp.where((t_i >= s_i)[None, :, :, None, None], kern[0][:, jnp.clip(t_i - s_i, 0, q - 1)], 0.0)
    kb = jnp.where((s_i >= t_i)[None, :, :, None, None], kern[1][:, jnp.clip(s_i - t_i, 0, q - 1)], 0.0)
    dsk = d_skip.astype(F32).reshape(g, pch)
    diag = (jnp.eye(q)[None, :, :, None, None] * jnp.eye(pch)[None, None, None] * dsk[:, None, None, :, None])
    mt = (kf + kb + diag).transpose(0, 2, 4, 1, 3).reshape(g, q * pch, q * pch)
    wsf = pw[0][:, ::-1][:, 1:, :, None] * b_bar[0][:, None]
    wsb = pw[1][:, :q, :, None] * b_bar[1][:, None]
    wsf = wsf.transpose(0, 1, 3, 2).reshape(g, q * pch, n_state)
    wsb = wsb.transpose(0, 1, 3, 2).reshape(g, q * pch, n_state)
    w1 = jnp.concatenate([mt, wsf.real, wsb.real, wsf.imag, wsb.imag], axis=-1)
    cf = c_mat[0][:, None] * pw[0][:, 1:, None, :]
    cb = c_mat[1][:, None] * pw[1][:, ::-1][:, :q, None, :]
    cf = cf.transpose(0, 3, 1, 2).reshape(g, n_state, q * pch)
    cb = cb.transpose(0, 3, 1, 2).reshape(g, n_state, q * pch)
    w3 = jnp.concatenate([cf.real, cb.real, -cf.imag, -cb.imag], axis=1)
    aq = pw[:, :, q]
    a_r = jnp.concatenate([aq[0].real, aq[1].real], axis=-1)
    a_i = jnp.concatenate([aq[0].imag, aq[1].imag], axis=-1)
    return w1.astype(BF16), w3.astype(BF16), a_r, a_i


def _s5_in_kernel(u_ref, w_ref, y_ref, sr_ref, si_ref):
    width = u_ref.shape[-1]
    ns2 = sr_ref.shape[-1]
    for g in range(u_ref.shape[0]):
        r = jnp.dot(u_ref[g], w_ref[g], preferred_element_type=F32)
        y_ref[g] = r[:, :width]
        sr_ref[0, g] = r[:, width:width + ns2]
        si_ref[0, g] = r[:, width + ns2:]


def _s5_scan_kernel(sr_ref, si_ref, ar_ref, ai_ref, hr_ref, hi_ref, fr, fi, br, bi, *, n_chunks, ctx_chunks):
    gb = ar_ref.shape[0]
    a_r = ar_ref[...]
    a_i = ai_ref[...]
    lane = lax.broadcasted_iota(jnp.int32, a_r.shape, 1)
    fwd = lane < (a_r.shape[1] // 2)

    def rows(ref, c):
        return ref[pl.ds(c, gb, stride=n_chunks), :]

    def body(i, carry):
        h_r, h_i = carry
        cf = i
        cb = jnp.where(i < ctx_chunks, ctx_chunks - 1 - i, n_chunks + ctx_chunks - 1 - i)
        fr[pl.ds(cf, gb, stride=n_chunks), :] = h_r
        fi[pl.ds(cf, gb, stride=n_chunks), :] = h_i
        br[pl.ds(cb, gb, stride=n_chunks), :] = h_r
        bi[pl.ds(cb, gb, stride=n_chunks), :] = h_i
        s_r = jnp.where(fwd, rows(sr_ref, cf), rows(sr_ref, cb))
        s_i = jnp.where(fwd, rows(si_ref, cf), rows(si_ref, cb))
        return (a_r * h_r - a_i * h_i + s_r, a_r * h_i + a_i * h_r + s_i)

    zero = jnp.zeros(a_r.shape, F32)
    lax.fori_loop(0, n_chunks, body, (zero, zero))
    full = lax.broadcasted_iota(jnp.int32, hr_ref.shape, 1) < (a_r.shape[1] // 2)
    hr_ref[...] = jnp.where(full, fr[...], br[...])
    hi_ref[...] = jnp.where(full, fi[...], bi[...])


def _s5_out_kernel(y_ref, hr_ref, hi_ref, w_ref, o_ref):
    ns2 = hr_ref.shape[-1]
    for g in range(y_ref.shape[0]):
        y = (y_ref[g]
             + jnp.dot(hr_ref[0, g].astype(BF16), w_ref[g, :ns2], preferred_element_type=F32)
             + jnp.dot(hi_ref[0, g].astype(BF16), w_ref[g, ns2:], preferred_element_type=F32))
        o_ref[g] = jax.nn.gelu(y).astype(o_ref.dtype)


def _s5_mix(u, tables, ctx_len):
    w1, w3, a_r, a_i = tables
    b, p, d = u.shape
    q, pch = SSM_CHUNK, SSM_GROUP
    g = d // pch
    nc = p // q
    width = q * pch
    ns2 = 2 * SSM_STATE
    gb = _pick(g, SSM_GROUP_BLOCK, 8)
    ut = u.reshape(b, nc, q, g, pch).transpose(3, 0, 1, 2, 4).reshape(g, b * nc, width)

    y_in, s_r, s_i = pl.pallas_call(
        _s5_in_kernel,
        out_shape=(jax.ShapeDtypeStruct((g, b * nc, width), F32),
                   jax.ShapeDtypeStruct((b, g, nc, ns2), F32),
                   jax.ShapeDtypeStruct((b, g, nc, ns2), F32)),
        grid=(g // gb, b),
        in_specs=[pl.BlockSpec((gb, nc, width), lambda gi, bb: (gi, bb, 0)),
                  pl.BlockSpec((gb, width, width + 2 * ns2), lambda gi, bb: (gi, 0, 0))],
        out_specs=(pl.BlockSpec((gb, nc, width), lambda gi, bb: (gi, bb, 0)),
                   pl.BlockSpec((1, gb, nc, ns2), lambda gi, bb: (bb, gi, 0, 0)),
                   pl.BlockSpec((1, gb, nc, ns2), lambda gi, bb: (bb, gi, 0, 0))),
        compiler_params=_cparams(("arbitrary", "arbitrary")),
        name="s5_chunk_in",
    )(ut, w1)

    flat = lambda gi, bb: (bb * (g // gb) + gi, 0)
    h_r, h_i = pl.pallas_call(
        functools.partial(_s5_scan_kernel, n_chunks=nc, ctx_chunks=ctx_len // q),
        out_shape=(jax.ShapeDtypeStruct((b * g * nc, ns2), F32),) * 2,
        grid=(g // gb, b),
        in_specs=[pl.BlockSpec((gb * nc, ns2), flat), pl.BlockSpec((gb * nc, ns2), flat),
                  pl.BlockSpec((gb, ns2), lambda gi, bb: (gi, 0)),
                  pl.BlockSpec((gb, ns2), lambda gi, bb: (gi, 0))],
        out_specs=(pl.BlockSpec((gb * nc, ns2), flat),) * 2,
        scratch_shapes=[pltpu.VMEM((gb * nc, ns2), F32)] * 4,
        compiler_params=_cparams(("arbitrary", "arbitrary")),
        name="s5_chunk_scan",
    )(s_r.reshape(b * g * nc, ns2), s_i.reshape(b * g * nc, ns2), a_r, a_i)

    yt = pl.pallas_call(
        _s5_out_kernel,
        out_shape=jax.ShapeDtypeStruct((g, b * nc, width), BF16),
        grid=(g // gb, b),
        in_specs=[pl.BlockSpec((gb, nc, width), lambda gi, bb: (gi, bb, 0)),
                  pl.BlockSpec((1, gb, nc, ns2), lambda gi, bb: (bb, gi, 0, 0)),
                  pl.BlockSpec((1, gb, nc, ns2), lambda gi, bb: (bb, gi, 0, 0)),
                  pl.BlockSpec((gb, 2 * ns2, width), lambda gi, bb: (gi, 0, 0))],
        out_specs=pl.BlockSpec((gb, nc, width), lambda gi, bb: (gi, bb, 0)),
        compiler_params=_cparams(("arbitrary", "arbitrary")),
        name="s5_chunk_out",
    )(y_in, h_r.reshape(b, g, nc, ns2), h_i.reshape(b, g, nc, ns2), w3)
    return yt.reshape(g, b, nc, q, pch).transpose(1, 2, 3, 0, 4).reshape(b, p, d)


def _moe_schedule(idx, n_exp, tm, n_tiles):
    t = idx.shape[0]
    e_flat = idx.reshape(-1)
    n_asg = e_flat.shape[0]
    order = jnp.argsort(e_flat, stable=True).astype(jnp.int32)
    counts = jnp.sum((e_flat[:, None] == jnp.arange(n_exp)[None, :]).astype(jnp.int32), axis=0)
    padded = (counts + tm - 1) // tm * tm
    gstart = jnp.cumsum(padded) - padded
    cstart = jnp.cumsum(counts) - counts
    gend = gstart + padded
    n_active = (jnp.sum(padded) // tm).astype(jnp.int32)
    tile_start = jnp.arange(n_tiles, dtype=jnp.int32) * tm
    tile_e = jnp.sum((gend[None, :] <= tile_start[:, None]).astype(jnp.int32), axis=1)
    tile_e = jnp.minimum(tile_e, n_exp - 1)
    tile_e = jnp.where(jnp.arange(n_tiles) < n_active, tile_e, tile_e[jnp.maximum(n_active - 1, 0)])
    row_e = jnp.repeat(tile_e, tm)
    j = jnp.arange(n_tiles * tm, dtype=jnp.int32) - gstart[row_e].astype(jnp.int32)
    valid = (j < counts[row_e]) & (jnp.arange(n_tiles * tm) < n_active * tm)
    sidx = jnp.clip(cstart[row_e] + j, 0, n_asg - 1)
    row_token = jnp.where(valid, order[sidx] // TOP_K, 0).astype(jnp.int32)
    sorted_e = e_flat[order]
    dest = (gstart[sorted_e] + jnp.arange(n_asg) - cstart[sorted_e]).astype(jnp.int32)
    inv = jnp.argsort(order).astype(jnp.int32)
    pos = dest[inv].reshape(t, TOP_K)
    return tile_e, n_active.reshape(1), row_token, pos


def _moe_prep_kernel(w_ref, sel_ref, o_ref):
    sel = sel_ref[...]
    blk = sel.shape[0]
    per = o_ref.shape[3] // blk
    for c in range(w_ref.shape[3] // blk):
        w = w_ref[0, 0, :, c * blk:(c + 1) * blk].astype(BF16)
        o_ref[0, c // per, :, (c % per) * blk:(c % per + 1) * blk] = jnp.dot(
            w, sel, preferred_element_type=F32).astype(BF16)


def _moe_chunk_width(n):
    return 4 * LANES if n % (4 * LANES) == 0 else 2 * LANES


def _moe_prep(w_gate_up, layer):
    _, n_exp, d, n = w_gate_up.shape
    blk = 2 * LANES
    cw = _moe_chunk_width(n)
    tk = _pick(d, 512, 8)
    src = jnp.arange(blk)
    sel = (jnp.arange(blk)[None, :] == (src // 2 + (src % 2) * LANES)[:, None]).astype(BF16)
    return pl.pallas_call(
        _moe_prep_kernel,
        out_shape=jax.ShapeDtypeStruct((n_exp, n // cw, d, cw), BF16),
        grid=(n_exp, d // tk),
        in_specs=[pl.BlockSpec((1, 1, tk, n), lambda e, k: (layer, e, k, 0)),
                  pl.BlockSpec((blk, blk), lambda e, k: (0, 0))],
        out_specs=pl.BlockSpec((1, n // cw, tk, cw), lambda e, k: (e, 0, k, 0)),
        compiler_params=_cparams(("arbitrary", "arbitrary")),
        name="moe_weight_prep",
    )(w_gate_up, sel)


def _cast_kernel(w_ref, o_ref):
    o_ref[0] = w_ref[0, 0].astype(o_ref.dtype)


def _moe_cast(w_down, layer):
    _, n_exp, f, d = w_down.shape
    tk = _pick(f, 512, 8)
    return pl.pallas_call(
        _cast_kernel,
        out_shape=jax.ShapeDtypeStruct((n_exp, f, d), BF16),
        grid=(n_exp, f // tk),
        in_specs=[pl.BlockSpec((1, 1, tk, d), lambda e, k: (layer, e, k, 0))],
        out_specs=pl.BlockSpec((1, tk, d), lambda e, k: (e, k, 0)),
        compiler_params=_cparams(("arbitrary", "arbitrary")),
        name="moe_weight_cast",
    )(w_down)


def _split_blocks(v):
    lead = v.shape[:-1]
    return v.reshape(lead + (-1, LANES, 2)).swapaxes(-1, -2).reshape(lead + (-1,))


def _moe_kernel(te_ref, na_ref, tok_ref, tokn_ref, x_hbm, w_ref, b_ref, wd_ref, bd_ref, o_ref,
                xbuf, xb, act, sem, *, tm):
    i = pl.program_id(0)
    n_act = na_ref[0]
    slot = i % 2
    n_chunks = w_ref.shape[1]
    rows_per = tm // n_chunks

    def row_copy(tok, r, s):
        return pltpu.make_async_copy(x_hbm.at[tok], xbuf.at[s, pl.ds(r, 1)], sem.at[s])

    def wait_rows(s):
        pltpu.make_async_copy(xbuf.at[1 - s], xbuf.at[s], sem.at[s]).wait()

    @pl.when(i == 0)
    def _():
        def body(r, c):
            row_copy(tok_ref[0, 0, r], r, 0).start()
            return c
        lax.fori_loop(0, tm, body, 0, unroll=8)

    @pl.when(i < n_act)
    def _():
        wait_rows(slot)
        lo, hi = _unpack_halves(xbuf[slot])
        half = lo.shape[1]
        xb[:, :half] = lo.astype(BF16)
        xb[:, half:] = hi.astype(BF16)

        def chunk(c, carry):
            for rr in range(rows_per):
                r = c * rows_per + rr
                row_copy(tokn_ref[0, 0, r], r, 1 - slot).start()
            h = jnp.dot(xb[...], w_ref[0, c], preferred_element_type=F32) + b_ref[0, c]
            acts = []
            for q in range(h.shape[1] // (2 * LANES)):
                gate = jnp.minimum(h[:, 2 * q * LANES:(2 * q + 1) * LANES], SWIGLU_LIMIT)
                up = jnp.clip(h[:, (2 * q + 1) * LANES:(2 * q + 2) * LANES], -SWIGLU_LIMIT, SWIGLU_LIMIT)
                acts.append(((up + 1.0) * (gate * jax.nn.sigmoid(SWIGLU_ALPHA * gate))).astype(BF16))
            act[c] = acts[0] if len(acts) == 1 else jnp.concatenate(acts, axis=1)
            return carry
        lax.fori_loop(0, n_chunks, chunk, 0)

        a = act[0] if n_chunks == 1 else jnp.concatenate([act[c] for c in range(n_chunks)], axis=1)
        y = jnp.dot(a, wd_ref[0], preferred_element_type=F32) + bd_ref[0]
        o_ref[:, 0, :] = _pack_halves(y)

    @pl.when(i == n_act)
    def _():
        wait_rows(slot)

    @pl.when(i >= n_act)
    def _():
        o_ref[...] = jnp.zeros(o_ref.shape, o_ref.dtype)


def _moe_experts(hp3d, row_token, tile_e, n_active, wp, bp, wd, bd, tm):
    half = hp3d.shape[-1]
    n_exp, n_chunks, d, cw = wp.shape
    f = wd.shape[1]
    n_tiles = row_token.shape[0] // tm
    tok3 = row_token.reshape(n_tiles, 1, tm)
    exp3 = lambda i, te, na: (te[i], 0, 0)
    exp4 = lambda i, te, na: (te[i], 0, 0, 0)
    return pl.pallas_call(
        functools.partial(_moe_kernel, tm=tm),
        out_shape=jax.ShapeDtypeStruct((n_tiles * tm, 1, half), jnp.uint32),
        grid_spec=pltpu.PrefetchScalarGridSpec(
            num_scalar_prefetch=2,
            grid=(n_tiles,),
            in_specs=[pl.BlockSpec((1, 1, tm), lambda i, te, na: (i, 0, 0), memory_space=pltpu.SMEM),
                      pl.BlockSpec((1, 1, tm), lambda i, te, na: (jnp.minimum(i + 1, n_tiles - 1), 0, 0),
                                   memory_space=pltpu.SMEM),
                      pl.BlockSpec(memory_space=pl.ANY),
                      pl.BlockSpec((1, n_chunks, d, cw), exp4), pl.BlockSpec((1, n_chunks, 1, cw), exp4),
                      pl.BlockSpec((1, f, d), exp3), pl.BlockSpec((1, 1, d), exp3)],
            out_specs=pl.BlockSpec((tm, 1, half), lambda i, te, na: (i, 0, 0)),
            scratch_shapes=[pltpu.VMEM((2, tm, half), jnp.uint32), pltpu.VMEM((tm, d), BF16),
                            pltpu.VMEM((n_chunks, tm, cw // 2), BF16), pltpu.SemaphoreType.DMA((2,))]),
        compiler_params=_cparams(("arbitrary",)),
        name="moe_experts",
    )(tile_e, n_active, tok3, tok3, hp3d, wp, bp, wd, bd)


def _combine_kernel(pos_ref, posn_ref, gw_ref, x_ref, g_ref, y_hbm, o_ref, buf, sem, *, rows):
    i = pl.program_id(0)
    n = pl.num_programs(0)

    slot = i % 2

    def row_copy(src, k, r, s):
        return pltpu.make_async_copy(y_hbm.at[src], buf.at[s, k, pl.ds(r, 1)], sem.at[s])

    @pl.when(i == 0)
    def _():
        def body(r, c):
            for k in range(TOP_K):
                row_copy(pos_ref[0, 0, r * TOP_K + k], k, r, 0).start()
            return c
        lax.fori_loop(0, rows, body, 0, unroll=2)

    pltpu.make_async_copy(buf.at[1 - slot], buf.at[slot], sem.at[slot]).wait()

    def combine():
        gw = gw_ref[...]
        half = buf.shape[-1]
        acc_lo = acc_hi = None
        for k in range(TOP_K):
            lo, hi = _unpack_halves(buf[slot, k])
            w = gw[:, k:k + 1]
            acc_lo = w * lo if k == 0 else acc_lo + w * lo
            acc_hi = w * hi if k == 0 else acc_hi + w * hi
        g = g_ref[0]
        o_ref[:, :half] = x_ref[:, :half] + g[:, :half] * acc_lo
        o_ref[:, half:] = x_ref[:, half:] + g[:, half:] * acc_hi

    @pl.when(i + 1 < n)
    def _():
        for r in range(rows):
            for k in range(TOP_K):
                row_copy(posn_ref[0, 0, r * TOP_K + k], k, r, 1 - slot).start()
        combine()

    @pl.when(i + 1 == n)
    def _():
        combine()


def _moe_combine(y_sorted, pos, gw2d, x2d, modt, tm_row, ctx_tiles, tiles_per_batch):
    t, d = x2d.shape
    rows = _pick(tm_row, COMBINE_ROWS, 8)
    n_steps = t // rows
    sub = tm_row // rows
    pos3 = pos.reshape(n_steps, 1, rows * TOP_K)

    def gmap(i):
        tile = i // sub
        bb = tile // tiles_per_batch
        seg = jnp.where(tile % tiles_per_batch >= ctx_tiles, 1, 0)
        return (bb * 2 + seg, 0, 5)

    return pl.pallas_call(
        functools.partial(_combine_kernel, rows=rows),
        out_shape=jax.ShapeDtypeStruct((t, d), F32),
        grid=(n_steps,),
        in_specs=[pl.BlockSpec((1, 1, rows * TOP_K), lambda i: (i, 0, 0), memory_space=pltpu.SMEM),
                  pl.BlockSpec((1, 1, rows * TOP_K), lambda i: (jnp.minimum(i + 1, n_steps - 1), 0, 0),
                               memory_space=pltpu.SMEM),
                  pl.BlockSpec((rows, LANES), lambda i: (i, 0)),
                  pl.BlockSpec((rows, d), lambda i: (i, 0)),
                  pl.BlockSpec((1, 1, d), gmap),
                  pl.BlockSpec(memory_space=pl.ANY)],
        out_specs=pl.BlockSpec((rows, d), lambda i: (i, 0)),
        scratch_shapes=[pltpu.VMEM((2, TOP_K, rows, d // 2), jnp.uint32), pltpu.SemaphoreType.DMA((2,))],
        compiler_params=_cparams(("arbitrary",)),
        name="moe_combine",
    )(pos3, pos3, gw2d, x2d, modt, y_sorted)


def _rope_tables(seq, ctx_len):
    n_rows = seq // GRID_W
    axis_rot = HEAD_DIM // 2
    rows = jnp.repeat(jnp.arange(n_rows, dtype=F32), GRID_W)
    cols = jnp.tile(jnp.arange(GRID_W, dtype=F32), n_rows)
    inv_freq = ROPE_BASE ** (-jnp.arange(0, axis_rot, 2, dtype=F32) / axis_rot)
    ang_r = rows[:, None] * inv_freq
    ang_c = cols[:, None] * inv_freq
    ang = jnp.concatenate([ang_r, ang_r, ang_c, ang_c], axis=-1)
    sign = jnp.tile(jnp.concatenate([-jnp.ones(axis_rot // 2, F32), jnp.ones(axis_rot // 2, F32)]), 2)
    cos = jnp.concatenate([jnp.ones((ctx_len, HEAD_DIM), F32), jnp.cos(ang)], axis=0)
    sin = jnp.concatenate([jnp.zeros((ctx_len, HEAD_DIM), F32), jnp.sin(ang) * sign], axis=0)
    return cos, sin


def kernel(x, c, ctx, c_ctx, w_mod, b_mod, norm_mix, norm_ffn, w_router, b_router, w_gate_up, b_gate_up, w_down, b_down, attn_w_qkv, attn_w_o, attn_q_gain, attn_k_gain, attn_sinks, ssm_a_re, ssm_a_im, ssm_log_dt, ssm_b_re, ssm_b_im, ssm_c_re, ssm_c_im, ssm_d, ssm_w_glu, ssm_b_glu):
    b, seq, d = x.shape
    ctx_len = ctx.shape[1]
    depth = w_mod.shape[0]
    p = ctx_len + seq
    t = b * p
    n_exp = w_router.shape[-1]
    n_q = d // HEAD_DIM
    n_kv = (attn_w_qkv.shape[-1] // HEAD_DIM - n_q) // 2
    assert ctx_len % ATT_BLOCK == 0 and seq % ATT_BLOCK == 0 and seq % GRID_W == 0
    assert d % (2 * LANES) == 0 and ctx_len % SSM_CHUNK == 0 and seq % SSM_CHUNK == 0

    tm = _pick(math.gcd(p, ctx_len), ROW_TILE)
    mm_tm = _pick(math.gcd(p, ctx_len), MM_TILE_M)
    tiles_pb = p // mm_tm
    ctx_tiles = ctx_len // mm_tm

    pad = (-(b + 1)) % 8
    cvec = jnp.concatenate([c, c_ctx[None, :], jnp.zeros((pad, d), F32)], axis=0)
    mod_all = _mod_all(cvec, w_mod, b_mod)

    cos_t, sin_t = _rope_tables(seq, ctx_len)
    xs = jnp.concatenate([ctx, x], axis=1)

    tm_moe = min(MOE_TILE, _pick(t * TOP_K, MOE_TILE, 8))
    n_tiles = (t * TOP_K) // tm_moe + n_exp
    scale = HEAD_DIM ** -0.5

    for i in range(depth):
        j = i // 2
        lat = mod_all[i, :b]
        cx = jnp.broadcast_to(mod_all[i, b][None], lat.shape)
        modt = jnp.stack([cx, lat], axis=1).reshape(b * 2, 1, 6 * d)

        x2d = xs.reshape(t, d)
        if i % 2 == 0:
            gain = jnp.concatenate([jnp.tile(attn_q_gain[j] * scale, n_q), jnp.tile(attn_k_gain[j], n_kv),
                                    jnp.ones((n_kv * HEAD_DIM,), F32)]).reshape(1, -1)
            qkv = _qkv_proj(x2d, norm_mix[i], modt, attn_w_qkv[j].astype(BF16), gain, cos_t, sin_t,
                            n_q + n_kv, mm_tm, ctx_tiles, tiles_pb)
            o = _attention(qkv.reshape(b, p, -1), attn_sinks[j].astype(F32), n_q, n_kv, ctx_len)
            x2d = _proj_residual(o.reshape(t, d), attn_w_o[j].astype(BF16), x2d, modt, 2,
                                 mm_tm, ctx_tiles, tiles_pb)
        else:
            u = _norm_mod(xs, norm_mix[i], modt, 0, ctx_len)
            tables = _s5_tables(ssm_a_re[j], ssm_a_im[j], ssm_log_dt[j], ssm_b_re[j], ssm_b_im[j],
                                ssm_c_re[j], ssm_c_im[j], ssm_d[j])
            gy = _s5_mix(u, tables, ctx_len)
            x2d = _glu_residual(gy.reshape(t, d), ssm_w_glu[j].astype(BF16), ssm_b_glu[j], x2d, modt, 2,
                                mm_tm, ctx_tiles, tiles_pb)

        hp, idx, gw = _norm_router(x2d.reshape(b, p, d), norm_ffn[i], modt, w_router[i], b_router[i], ctx_len)
        idx2d = idx.reshape(t, LANES)[:, :TOP_K]
        tile_e, n_active, row_token, pos = _moe_schedule(idx2d, n_exp, tm_moe, n_tiles)
        wp = _moe_prep(w_gate_up, i)
        bp = _split_blocks(b_gate_up[i]).reshape(n_exp, wp.shape[1], 1, wp.shape[3])
        y_sorted = _moe_experts(hp.reshape(t, 1, d // 2), row_token, tile_e, n_active, wp, bp,
                                _moe_cast(w_down, i), b_down[i].reshape(n_exp, 1, d), tm_moe)
        x2d = _moe_combine(y_sorted, pos, gw.reshape(t, LANES), x2d, modt, tm, ctx_len // tm, p // tm)
        xs = x2d.reshape(b, p, d)

    return xs[:, ctx_len:]
```

```python
import functools
import math

import jax
import jax.numpy as jnp
from jax import lax
from jax.experimental import pallas as pl
from jax.experimental.pallas import tpu as pltpu

F32 = jnp.float32
BF16 = jnp.bfloat16

HEAD_DIM = 128
ATT_BLOCK = 128
GRID_W = 64
ROPE_BASE = 10000.0
SSM_GROUP = 16
SSM_STATE = 64
SSM_CHUNK = 16
TOP_K = 4
SWIGLU_LIMIT = 7.0
SWIGLU_ALPHA = 1.702
EPS = 1e-6
NEG_INF = -1e30
DT_FLOOR_RE = -1e-4

LANES = 128
VMEM_LIMIT = 56 * 1024 * 1024

ROW_TILE = 256
MM_TILE_M = 512
PROJ_TILE_N = 2048
GLU_TILE_N = 1024
MOE_TILE = 512
COMBINE_ROWS = 128
SSM_GROUP_BLOCK = 16


def _cparams(sem):
    return pltpu.CompilerParams(dimension_semantics=sem, vmem_limit_bytes=VMEM_LIMIT)


def _pick(n, pref, mult=8):
    t = min(n, pref)
    while n % t or t % mult:
        t -= 1
    return t


def _mod_kernel(c_ref, w_ref, b_ref, o_ref):
    c = c_ref[...]
    s = c * jax.nn.sigmoid(c)
    o_ref[0] = jnp.dot(s.astype(BF16), w_ref[0].astype(BF16), preferred_element_type=F32) + b_ref[0]


def _mod_all(cvec, w_mod, b_mod):
    depth, d, n = w_mod.shape
    tn = _pick(n, 1024, LANES)
    return pl.pallas_call(
        _mod_kernel,
        out_shape=jax.ShapeDtypeStruct((depth, cvec.shape[0], n), F32),
        grid=(depth, n // tn),
        in_specs=[pl.BlockSpec(cvec.shape, lambda l, j: (0, 0)),
                  pl.BlockSpec((1, d, tn), lambda l, j: (l, 0, j)),
                  pl.BlockSpec((1, 1, tn), lambda l, j: (l, 0, j))],
        out_specs=pl.BlockSpec((1, cvec.shape[0], tn), lambda l, j: (l, 0, j)),
        compiler_params=_cparams(("arbitrary", "arbitrary")),
        name="adaln_rows",
    )(cvec, w_mod, b_mod.reshape(depth, 1, n))


def _normed(x, w, sh, sc):
    r = lax.rsqrt(jnp.mean(x * x, axis=-1, keepdims=True) + EPS)
    return (x * r * w) * (1.0 + sc) + sh


def _norm_mod_kernel(x_ref, w_ref, sh_ref, sc_ref, o_ref):
    o_ref[0] = _normed(x_ref[0], w_ref[...], sh_ref[0], sc_ref[0]).astype(o_ref.dtype)


def _mod_spec(d, which, ctx_tiles):
    return pl.BlockSpec((1, 1, d), lambda b, i: (b * 2 + jnp.where(i >= ctx_tiles, 1, 0), 0, which))


def _norm_mod(x, w, modt, which_shift, ctx_len):
    b, p, d = x.shape
    tm = _pick(math.gcd(p, ctx_len), ROW_TILE)
    ct = ctx_len // tm
    return pl.pallas_call(
        _norm_mod_kernel,
        out_shape=jax.ShapeDtypeStruct((b, p, d), BF16),
        grid=(b, p // tm),
        in_specs=[pl.BlockSpec((1, tm, d), lambda bb, i: (bb, i, 0)),
                  pl.BlockSpec((1, d), lambda bb, i: (0, 0)),
                  _mod_spec(d, which_shift, ct), _mod_spec(d, which_shift + 1, ct)],
        out_specs=pl.BlockSpec((1, tm, d), lambda bb, i: (bb, i, 0)),
        compiler_params=_cparams(("arbitrary", "arbitrary")),
        name="norm_mod",
    )(x, w.reshape(1, d), modt, modt)


def _pack_halves(h):
    half = h.shape[-1] // 2
    bits = lax.bitcast_convert_type(h.astype(BF16).astype(F32), jnp.uint32)
    return (bits[:, :half] >> 16) | (bits[:, half:] & jnp.uint32(0xFFFF0000))


def _unpack_halves(xp):
    lo = lax.bitcast_convert_type(xp << 16, F32)
    hi = lax.bitcast_convert_type(xp & jnp.uint32(0xFFFF0000), F32)
    return lo, hi


def _norm_router_kernel(x_ref, w_ref, sh_ref, sc_ref, wr_ref, br_ref, hp_ref, idx_ref, gw_ref, rank_ref,
                        cnt_ref, base):
    @pl.when((pl.program_id(0) == 0) & (pl.program_id(1) == 0))
    def _():
        base[...] = jnp.zeros(base.shape, F32)

    h = _normed(x_ref[0], w_ref[...], sh_ref[0], sc_ref[0])
    hp_ref[0] = _pack_halves(h)
    logits = jnp.dot(h, wr_ref[...], precision=lax.Precision.HIGHEST,
                     preferred_element_type=F32) + br_ref[...]
    n_exp = logits.shape[-1]
    lane = lax.broadcasted_iota(jnp.int32, logits.shape, 1)
    out_lane = lax.broadcasted_iota(jnp.int32, idx_ref.shape[1:], 1)
    vals, idxs = [], []
    rest = logits
    for _ in range(TOP_K):
        m = jnp.max(rest, axis=-1, keepdims=True)
        idx = jnp.min(jnp.where(rest == m, lane, n_exp), axis=-1, keepdims=True)
        vals.append(m)
        idxs.append(idx)
        rest = jnp.where(lane == idx, -jnp.inf, rest)
    exps = [jnp.exp(v - vals[0]) for v in vals]
    tot = exps[0]
    for e in exps[1:]:
        tot = tot + e
    idx_out = jnp.zeros(idx_ref.shape[1:], jnp.int32)
    gw_out = jnp.zeros(gw_ref.shape[1:], F32)
    for k in range(TOP_K):
        idx_out = jnp.where(out_lane == k, idxs[k], idx_out)
        gw_out = jnp.where(out_lane == k, exps[k] / tot, gw_out)
    idx_ref[0] = idx_out
    gw_ref[0] = gw_out

    sel = jnp.zeros(logits.shape, F32)
    for k in range(TOP_K):
        sel = sel + (lane == idxs[k]).astype(F32)
    tm = sel.shape[0]
    earlier = (lax.broadcasted_iota(jnp.int32, (tm, tm), 1)
               < lax.broadcasted_iota(jnp.int32, (tm, tm), 0)).astype(BF16)
    before = jnp.dot(earlier, sel.astype(BF16), preferred_element_type=F32) + base[...]
    rank_out = jnp.zeros(rank_ref.shape[1:], jnp.int32)
    for k in range(TOP_K):
        rk = jnp.sum(jnp.where(lane == idxs[k], before, 0.0), axis=-1, keepdims=True)
        rank_out = jnp.where(out_lane == k, rk.astype(jnp.int32), rank_out)
    rank_ref[0] = rank_out
    base[...] = base[...] + jnp.sum(sel, axis=0, keepdims=True)
    cnt_ref[...] = base[...]


def _norm_router(x, w, modt, w_router, b_router, ctx_len):
    b, p, d = x.shape
    n_exp = w_router.shape[-1]
    tm = _pick(math.gcd(p, ctx_len), ROW_TILE)
    ct = ctx_len // tm
    row = lambda bb, i: (bb, i, 0)
    return pl.pallas_call(
        _norm_router_kernel,
        out_shape=(jax.ShapeDtypeStruct((b, p, d // 2), jnp.uint32),
                   jax.ShapeDtypeStruct((b, p, LANES), jnp.int32),
                   jax.ShapeDtypeStruct((b, p, LANES), F32),
                   jax.ShapeDtypeStruct((b, p, LANES), jnp.int32),
                   jax.ShapeDtypeStruct((1, n_exp), F32)),
        grid=(b, p // tm),
        in_specs=[pl.BlockSpec((1, tm, d), row),
                  pl.BlockSpec((1, d), lambda bb, i: (0, 0)),
                  _mod_spec(d, 3, ct), _mod_spec(d, 4, ct),
                  pl.BlockSpec((d, n_exp), lambda bb, i: (0, 0)),
                  pl.BlockSpec((1, n_exp), lambda bb, i: (0, 0))],
        out_specs=(pl.BlockSpec((1, tm, d // 2), row),
                   pl.BlockSpec((1, tm, LANES), row),
                   pl.BlockSpec((1, tm, LANES), row),
                   pl.BlockSpec((1, tm, LANES), row),
                   pl.BlockSpec((1, n_exp), lambda bb, i: (0, 0))),
        scratch_shapes=[pltpu.VMEM((1, n_exp), F32)],
        compiler_params=_cparams(("arbitrary", "arbitrary")),
        name="norm_router",
    )(x, w.reshape(1, d), modt, modt, w_router, b_router.reshape(1, n_exp))


def _qkv_kernel(x_ref, nw_ref, sh_ref, sc_ref, w_ref, g_ref, cos_ref, sin_ref, o_ref, *, n_norm_heads):
    u = _normed(x_ref[...], nw_ref[...], sh_ref[0], sc_ref[0]).astype(BF16)
    acc = jnp.dot(u, w_ref[...], preferred_element_type=F32)
    cos = cos_ref[...]
    sin = sin_ref[...]
    lane = lax.broadcasted_iota(jnp.int32, cos.shape, 1)
    first = (lane // (HEAD_DIM // 4)) % 2 == 0
    for hh in range(acc.shape[1] // HEAD_DIM):
        sl = slice(hh * HEAD_DIM, (hh + 1) * HEAD_DIM)
        xh = acc[:, sl]
        if hh < n_norm_heads:
            r = lax.rsqrt(jnp.mean(xh * xh, axis=-1, keepdims=True) + EPS)
            y = xh * r * g_ref[:, sl]
            partner = jnp.where(first, pltpu.roll(y, HEAD_DIM - HEAD_DIM // 4, 1),
                                pltpu.roll(y, HEAD_DIM // 4, 1))
            xh = y * cos + partner * sin
        o_ref[:, sl] = xh.astype(o_ref.dtype)


def _row_mod_spec(d, which, ctx_tiles, tiles_per_batch):
    def imap(i):
        seg = jnp.where(i % tiles_per_batch >= ctx_tiles, 1, 0)
        return ((i // tiles_per_batch) * 2 + seg, 0, which)

    return pl.BlockSpec((1, 1, d), imap)


def _qkv_proj(x2d, norm_w, modt, w_bf16, gain_row, cos_t, sin_t, n_norm_heads, tm, ctx_tiles, tiles_per_batch):
    m, k = x2d.shape
    n = w_bf16.shape[1]
    pos = lambda i: (i % tiles_per_batch, 0)
    return pl.pallas_call(
        functools.partial(_qkv_kernel, n_norm_heads=n_norm_heads),
        out_shape=jax.ShapeDtypeStruct((m, n), BF16),
        grid=(m // tm,),
        in_specs=[pl.BlockSpec((tm, k), lambda i: (i, 0)),
                  pl.BlockSpec((1, k), lambda i: (0, 0)),
                  _row_mod_spec(k, 0, ctx_tiles, tiles_per_batch),
                  _row_mod_spec(k, 1, ctx_tiles, tiles_per_batch),
                  pl.BlockSpec((k, n), lambda i: (0, 0)),
                  pl.BlockSpec((1, n), lambda i: (0, 0)),
                  pl.BlockSpec((tm, HEAD_DIM), pos),
                  pl.BlockSpec((tm, HEAD_DIM), pos)],
        out_specs=pl.BlockSpec((tm, n), lambda i: (i, 0)),
        compiler_params=_cparams(("arbitrary",)),
        name="qkv_proj",
    )(x2d, norm_w.reshape(1, k), modt, modt, w_bf16, gain_row, cos_t, sin_t)


def _attn_kernel(sink_ref, q_ref, kc_ref, vc_ref, kp_ref, ks_ref, kn_ref, vp_ref, vs_ref, vn_ref,
                 o_ref, *, q_per_kv, ctx_blocks, n_blocks):
    h = pl.program_id(1)
    i = pl.program_id(2)
    blk = ATT_BLOCK
    q = q_ref[0]
    qs = jnp.concatenate([q[:, g * HEAD_DIM:(g + 1) * HEAD_DIM] for g in range(q_per_kv)], axis=0)
    rows = qs.shape[0]
    dn = (((1,), (1,)), ((), ()))
    k_loc = jnp.concatenate([kp_ref[0], ks_ref[0], kn_ref[0]], axis=0)
    v_loc = jnp.concatenate([vp_ref[0], vs_ref[0], vn_ref[0]], axis=0)
    s_loc = lax.dot_general(qs, k_loc, dn, preferred_element_type=F32)
    s_ctx = lax.dot_general(qs, kc_ref[0], dn, preferred_element_type=F32)

    qi = lax.broadcasted_iota(jnp.int32, s_loc.shape, 0) % blk
    kj = lax.broadcasted_iota(jnp.int32, s_loc.shape, 1)
    is_lat = i >= ctx_blocks
    ok_prev = (kj < blk) & (kj >= qi) & (i > ctx_blocks)
    ok_self = (kj >= blk) & (kj < 2 * blk)
    ok_next = (kj >= 2 * blk) & (kj - 2 * blk <= qi) & (i < n_blocks - 1)
    valid = (ok_prev | ok_self | ok_next) & is_lat
    s_loc = jnp.where(valid, s_loc, NEG_INF)

    row_head = lax.broadcasted_iota(jnp.int32, (rows, 1), 0) // blk
    sink = jnp.zeros((rows, 1), F32)
    for g in range(q_per_kv):
        sink = jnp.where(row_head == g, sink_ref[h * q_per_kv + g], sink)

    m = jnp.maximum(jnp.maximum(jnp.max(s_loc, axis=-1, keepdims=True),
                                jnp.max(s_ctx, axis=-1, keepdims=True)), sink)
    p_loc = jnp.exp(s_loc - m)
    p_ctx = jnp.exp(s_ctx - m)
    denom = (jnp.sum(p_loc, axis=-1, keepdims=True) + jnp.sum(p_ctx, axis=-1, keepdims=True)
             + jnp.exp(sink - m))
    o = (jnp.dot(p_loc.astype(BF16), v_loc, preferred_element_type=F32)
         + jnp.dot(p_ctx.astype(BF16), vc_ref[0], preferred_element_type=F32)) / denom
    o = o.astype(o_ref.dtype)
    for g in range(q_per_kv):
        o_ref[0, :, g * HEAD_DIM:(g + 1) * HEAD_DIM] = o[g * blk:(g + 1) * blk]


def _attention(qkv, sinks, n_q, n_kv, ctx_len):
    b, p, _ = qkv.shape
    blk = ATT_BLOCK
    nb = p // blk
    cb = ctx_len // blk
    qpk = n_q // n_kv
    qw = qpk * HEAD_DIM
    kcol = n_q
    vcol = n_q + n_kv

    def loc(col0, off):
        return pl.BlockSpec(
            (1, blk, HEAD_DIM),
            lambda bb, h, i, s: (bb, jnp.clip(i + off, cb, nb - 1), col0 + h))

    return pl.pallas_call(
        functools.partial(_attn_kernel, q_per_kv=qpk, ctx_blocks=cb, n_blocks=nb),
        out_shape=jax.ShapeDtypeStruct((b, p, n_q * HEAD_DIM), BF16),
        grid_spec=pltpu.PrefetchScalarGridSpec(
            num_scalar_prefetch=1,
            grid=(b, n_kv, nb),
            in_specs=[pl.BlockSpec((1, blk, qw), lambda bb, h, i, s: (bb, i, h)),
                      pl.BlockSpec((1, ctx_len, HEAD_DIM), lambda bb, h, i, s: (bb, 0, kcol + h)),
                      pl.BlockSpec((1, ctx_len, HEAD_DIM), lambda bb, h, i, s: (bb, 0, vcol + h)),
                      loc(kcol, -1), loc(kcol, 0), loc(kcol, 1),
                      loc(vcol, -1), loc(vcol, 0), loc(vcol, 1)],
            out_specs=pl.BlockSpec((1, blk, qw), lambda bb, h, i, s: (bb, i, h))),
        compiler_params=_cparams(("arbitrary", "arbitrary", "arbitrary")),
        name="window_attn",
    )(sinks, qkv, qkv, qkv, qkv, qkv, qkv, qkv, qkv, qkv)


def _gate_spec(d, which, ctx_tiles, tiles_per_batch, tn):
    per = d // tn

    def imap(j, i):
        bb = i // tiles_per_batch
        seg = jnp.where(i % tiles_per_batch >= ctx_tiles, 1, 0)
        return (bb * 2 + seg, 0, which * per + j)

    return pl.BlockSpec((1, 1, tn), imap)


def _proj_res_kernel(a_ref, w_ref, x_ref, g_ref, o_ref):
    y = jnp.dot(a_ref[...], w_ref[...], preferred_element_type=F32)
    o_ref[...] = x_ref[...] + g_ref[0] * y


def _proj_residual(a2d, w_bf16, x2d, modt, which_gate, tm, ctx_tiles, tiles_per_batch):
    m, k = a2d.shape
    n = w_bf16.shape[1]
    tn = _pick(n, PROJ_TILE_N, LANES)
    return pl.pallas_call(
        _proj_res_kernel,
        out_shape=jax.ShapeDtypeStruct((m, n), F32),
        grid=(n // tn, m // tm),
        in_specs=[pl.BlockSpec((tm, k), lambda j, i: (i, 0)),
                  pl.BlockSpec((k, tn), lambda j, i: (0, j)),
                  pl.BlockSpec((tm, tn), lambda j, i: (i, j)),
                  _gate_spec(n, which_gate, ctx_tiles, tiles_per_batch, tn)],
        out_specs=pl.BlockSpec((tm, tn), lambda j, i: (i, j)),
        compiler_params=_cparams(("arbitrary", "arbitrary")),
        name="proj_residual",
    )(a2d, w_bf16, x2d, modt)


def _glu_res_kernel(a_ref, w1_ref, w2_ref, b1_ref, b2_ref, x_ref, g_ref, o_ref):
    a = a_ref[...]
    z1 = jnp.dot(a, w1_ref[...], preferred_element_type=F32) + b1_ref[...]
    z2 = jnp.dot(a, w2_ref[...], preferred_element_type=F32) + b2_ref[...]
    o_ref[...] = x_ref[...] + g_ref[0] * (z1 * jax.nn.sigmoid(z2))


def _glu_residual(a2d, w_bf16, b_glu, x2d, modt, which_gate, tm, ctx_tiles, tiles_per_batch):
    m, k = a2d.shape
    d = w_bf16.shape[1] // 2
    tn = _pick(d, GLU_TILE_N, LANES)
    nj = d // tn
    b2 = b_glu.reshape(1, 2 * d)
    return pl.pallas_call(
        _glu_res_kernel,
        out_shape=jax.ShapeDtypeStruct((m, d), F32),
        grid=(nj, m // tm),
        in_specs=[pl.BlockSpec((tm, k), lambda j, i: (i, 0)),
                  pl.BlockSpec((k, tn), lambda j, i: (0, j)),
                  pl.BlockSpec((k, tn), lambda j, i: (0, j + nj)),
                  pl.BlockSpec((1, tn), lambda j, i: (0, j)),
                  pl.BlockSpec((1, tn), lambda j, i: (0, j + nj)),
                  pl.BlockSpec((tm, tn), lambda j, i: (i, j)),
                  _gate_spec(d, which_gate, ctx_tiles, tiles_per_batch, tn)],
        out_specs=pl.BlockSpec((tm, tn), lambda j, i: (i, j)),
        compiler_params=_cparams(("arbitrary", "arbitrary")),
        name="glu_residual",
    )(a2d, w_bf16, w_bf16, b2, b2, x2d, modt)


def _s5_tables(a_re, a_im, log_dt, b_re, b_im, c_re, c_im, d_skip):
    q, n_state, pch = SSM_CHUNK, a_re.shape[-1], b_re.shape[-1]
    g = a_re.shape[1]
    hp = lax.Precision.HIGHEST
    lam = lax.complex(jnp.minimum(a_re.astype(F32), DT_FLOOR_RE), a_im.astype(F32))
    lam_dt = lam * jnp.exp(log_dt.astype(F32))[..., None]
    b_bar = ((jnp.exp(lam_dt) - 1.0) / lam)[..., None] * lax.complex(b_re.astype(F32), b_im.astype(F32))
    c_mat = lax.complex(c_re.astype(F32), c_im.astype(F32))
    tau = jnp.arange(q + 1, dtype=F32)
    pw = jnp.exp(lam_dt[:, :, None, :] * tau[None, None, :, None])
    kern = jnp.einsum('dgpn,dgtn,dgnr->dgtpr', c_mat, pw[:, :, :q], b_bar, precision=hp).real
    t_i = jnp.arange(q)[:, None]
    s_i = jnp.arange(q)[None, :]
    kf = jnp.where((t_i >= s_i)[None, :, :, None, None], kern[0][:, jnp.clip(t_i - s_i, 0, q - 1)], 0.0)
    kb = jnp.where((s_i >= t_i)[None, :, :, None, None], kern[1][:, jnp.clip(s_i - t_i, 0, q - 1)], 0.0)
    dsk = d_skip.astype(F32).reshape(g, pch)
    diag = (jnp.eye(q)[None, :, :, None, None] * jnp.eye(pch)[None, None, None] * dsk[:, None, None, :, None])
    mt = (kf + kb + diag).transpose(0, 2, 4, 1, 3).reshape(g, q * pch, q * pch)
    wsf = pw[0][:, ::-1][:, 1:, :, None] * b_bar[0][:, None]
    wsb = pw[1][:, :q, :, None] * b_bar[1][:, None]
    wsf = wsf.transpose(0, 1, 3, 2).reshape(g, q * pch, n_state)
    wsb = wsb.transpose(0, 1, 3, 2).reshape(g, q * pch, n_state)
    w1 = jnp.concatenate([mt, wsf.real, wsb.real, wsf.imag, wsb.imag], axis=-1)
    cf = c_mat[0][:, None] * pw[0][:, 1:, None, :]
    cb = c_mat[1][:, None] * pw[1][:, ::-1][:, :q, None, :]
    cf = cf.transpose(0, 3, 1, 2).reshape(g, n_state, q * pch)
    cb = cb.transpose(0, 3, 1, 2).reshape(g, n_state, q * pch)
    w3 = jnp.concatenate([cf.real, cb.real, -cf.imag, -cb.imag], axis=1)
    aq = pw[:, :, q]
    a_r = jnp.concatenate([aq[0].real, aq[1].real], axis=-1)
    a_i = jnp.concatenate([aq[0].imag, aq[1].imag], axis=-1)
    return w1.astype(BF16), w3.astype(BF16), a_r, a_i


def _s5_in_kernel(u_ref, w_ref, y_ref, sr_ref, si_ref):
    width = u_ref.shape[-1]
    ns2 = sr_ref.shape[-1]
    for g in range(u_ref.shape[0]):
        r = jnp.dot(u_ref[g], w_ref[g], preferred_element_type=F32)
        y_ref[g] = r[:, :width]
        sr_ref[0, g] = r[:, width:width + ns2]
        si_ref[0, g] = r[:, width + ns2:]


def _s5_scan_kernel(sr_ref, si_ref, ar_ref, ai_ref, hr_ref, hi_ref, fr, fi, br, bi, *, n_chunks, ctx_chunks):
    gb = ar_ref.shape[0]
    a_r = ar_ref[...]
    a_i = ai_ref[...]
    lane = lax.broadcasted_iota(jnp.int32, a_r.shape, 1)
    fwd = lane < (a_r.shape[1] // 2)

    def rows(ref, c):
        return ref[pl.ds(c, gb, stride=n_chunks), :]

    def body(i, carry):
        h_r, h_i = carry
        cf = i
        cb = jnp.where(i < ctx_chunks, ctx_chunks - 1 - i, n_chunks + ctx_chunks - 1 - i)
        fr[pl.ds(cf, gb, stride=n_chunks), :] = h_r
        fi[pl.ds(cf, gb, stride=n_chunks), :] = h_i
        br[pl.ds(cb, gb, stride=n_chunks), :] = h_r
        bi[pl.ds(cb, gb, stride=n_chunks), :] = h_i
        s_r = jnp.where(fwd, rows(sr_ref, cf), rows(sr_ref, cb))
        s_i = jnp.where(fwd, rows(si_ref, cf), rows(si_ref, cb))
        return (a_r * h_r - a_i * h_i + s_r, a_r * h_i + a_i * h_r + s_i)

    zero = jnp.zeros(a_r.shape, F32)
    lax.fori_loop(0, n_chunks, body, (zero, zero))
    full = lax.broadcasted_iota(jnp.int32, hr_ref.shape, 1) < (a_r.shape[1] // 2)
    hr_ref[...] = jnp.where(full, fr[...], br[...])
    hi_ref[...] = jnp.where(full, fi[...], bi[...])


def _s5_out_kernel(y_ref, hr_ref, hi_ref, w_ref, o_ref):
    ns2 = hr_ref.shape[-1]
    for g in range(y_ref.shape[0]):
        y = (y_ref[g]
             + jnp.dot(hr_ref[0, g].astype(BF16), w_ref[g, :ns2], preferred_element_type=F32)
             + jnp.dot(hi_ref[0, g].astype(BF16), w_ref[g, ns2:], preferred_element_type=F32))
        o_ref[g] = jax.nn.gelu(y).astype(o_ref.dtype)


def _s5_mix(u, tables, ctx_len):
    w1, w3, a_r, a_i = tables
    b, p, d = u.shape
    q, pch = SSM_CHUNK, SSM_GROUP
    g = d // pch
    nc = p // q
    width = q * pch
    ns2 = 2 * SSM_STATE
    gb = _pick(g, SSM_GROUP_BLOCK, 8)
    ut = u.reshape(b, nc, q, g, pch).transpose(3, 0, 1, 2, 4).reshape(g, b * nc, width)

    y_in, s_r, s_i = pl.pallas_call(
        _s5_in_kernel,
        out_shape=(jax.ShapeDtypeStruct((g, b * nc, width), F32),
                   jax.ShapeDtypeStruct((b, g, nc, ns2), F32),
                   jax.ShapeDtypeStruct((b, g, nc, ns2), F32)),
        grid=(g // gb, b),
        in_specs=[pl.BlockSpec((gb, nc, width), lambda gi, bb: (gi, bb, 0)),
                  pl.BlockSpec((gb, width, width + 2 * ns2), lambda gi, bb: (gi, 0, 0))],
        out_specs=(pl.BlockSpec((gb, nc, width), lambda gi, bb: (gi, bb, 0)),
                   pl.BlockSpec((1, gb, nc, ns2), lambda gi, bb: (bb, gi, 0, 0)),
                   pl.BlockSpec((1, gb, nc, ns2), lambda gi, bb: (bb, gi, 0, 0))),
        compiler_params=_cparams(("arbitrary", "arbitrary")),
        name="s5_chunk_in",
    )(ut, w1)

    flat = lambda gi, bb: (bb * (g // gb) + gi, 0)
    h_r, h_i = pl.pallas_call(
        functools.partial(_s5_scan_kernel, n_chunks=nc, ctx_chunks=ctx_len // q),
        out_shape=(jax.ShapeDtypeStruct((b * g * nc, ns2), F32),) * 2,
        grid=(g // gb, b),
        in_specs=[pl.BlockSpec((gb * nc, ns2), flat), pl.BlockSpec((gb * nc, ns2), flat),
                  pl.BlockSpec((gb, ns2), lambda gi, bb: (gi, 0)),
                  pl.BlockSpec((gb, ns2), lambda gi, bb: (gi, 0))],
        out_specs=(pl.BlockSpec((gb * nc, ns2), flat),) * 2,
        scratch_shapes=[pltpu.VMEM((gb * nc, ns2), F32)] * 4,
        compiler_params=_cparams(("arbitrary", "arbitrary")),
        name="s5_chunk_scan",
    )(s_r.reshape(b * g * nc, ns2), s_i.reshape(b * g * nc, ns2), a_r, a_i)

    yt = pl.pallas_call(
        _s5_out_kernel,
        out_shape=jax.ShapeDtypeStruct((g, b * nc, width), BF16),
        grid=(g // gb, b),
        in_specs=[pl.BlockSpec((gb, nc, width), lambda gi, bb: (gi, bb, 0)),
                  pl.BlockSpec((1, gb, nc, ns2), lambda gi, bb: (bb, gi, 0, 0)),
                  pl.BlockSpec((1, gb, nc, ns2), lambda gi, bb: (bb, gi, 0, 0)),
                  pl.BlockSpec((gb, 2 * ns2, width), lambda gi, bb: (gi, 0, 0))],
        out_specs=pl.BlockSpec((gb, nc, width), lambda gi, bb: (gi, bb, 0)),
        compiler_params=_cparams(("arbitrary", "arbitrary")),
        name="s5_chunk_out",
    )(y_in, h_r.reshape(b, g, nc, ns2), h_i.reshape(b, g, nc, ns2), w3)
    return yt.reshape(g, b, nc, q, pch).transpose(1, 2, 3, 0, 4).reshape(b, p, d)


def _moe_schedule(idx, rank, counts, tm, n_tiles):
    n_exp = counts.shape[0]
    padded = (counts + tm - 1) // tm * tm
    gstart = jnp.cumsum(padded) - padded
    gend = gstart + padded
    n_active = (jnp.sum(padded) // tm).astype(jnp.int32)
    tile_start = jnp.arange(n_tiles, dtype=jnp.int32) * tm
    tile_e = jnp.sum((gend[None, :] <= tile_start[:, None]).astype(jnp.int32), axis=1)
    tile_e = jnp.minimum(tile_e, n_exp - 1)
    tile_e = jnp.where(jnp.arange(n_tiles) < n_active, tile_e, tile_e[jnp.maximum(n_active - 1, 0)])
    pos = (jnp.take(gstart, idx) + rank).astype(jnp.int32)
    return tile_e.astype(jnp.int32), n_active.reshape(1), pos


def _scatter_kernel(pos_ref, h_ref, zero_hbm, o_hbm, size_buf, sem, *, rows):
    del zero_hbm

    def body(r, c):
        for k in range(TOP_K):
            pltpu.make_async_copy(h_ref.at[pl.ds(r, 1)], o_hbm.at[pl.ds(pos_ref[0, 0, r * TOP_K + k], 1)],
                                  sem).start()
        return c
    lax.fori_loop(0, rows, body, 0, unroll=4)
    for _ in range(TOP_K):
        pltpu.make_async_copy(size_buf, size_buf, sem).wait()


def _moe_scatter(hp2d, pos, n_rows, rows):
    t, half = hp2d.shape
    n_steps = t // rows
    zeros = jnp.zeros((n_rows, half), hp2d.dtype)
    return pl.pallas_call(
        functools.partial(_scatter_kernel, rows=rows),
        out_shape=jax.ShapeDtypeStruct((n_rows, half), hp2d.dtype),
        grid=(n_steps,),
        in_specs=[pl.BlockSpec((1, 1, rows * TOP_K), lambda i: (i, 0, 0), memory_space=pltpu.SMEM),
                  pl.BlockSpec((rows, half), lambda i: (i, 0)),
                  pl.BlockSpec(memory_space=pl.ANY)],
        out_specs=pl.BlockSpec(memory_space=pl.ANY),
        scratch_shapes=[pltpu.VMEM((rows, half), hp2d.dtype), pltpu.SemaphoreType.DMA(())],
        input_output_aliases={2: 0},
        compiler_params=_cparams(("arbitrary",)),
        name="moe_scatter",
    )(pos.reshape(n_steps, 1, rows * TOP_K), hp2d, zeros)


def _moe_prep_kernel(w_ref, sel_ref, o_ref):
    sel = sel_ref[...]
    blk = sel.shape[0]
    per = o_ref.shape[3] // blk
    for c in range(w_ref.shape[3] // blk):
        w = w_ref[0, 0, :, c * blk:(c + 1) * blk].astype(BF16)
        o_ref[0, c // per, :, (c % per) * blk:(c % per + 1) * blk] = jnp.dot(
            w, sel, preferred_element_type=F32).astype(BF16)


def _moe_chunk_width(n):
    return 4 * LANES if n % (4 * LANES) == 0 else 2 * LANES


def _moe_prep(w_gate_up, layer):
    _, n_exp, d, n = w_gate_up.shape
    blk = 2 * LANES
    cw = _moe_chunk_width(n)
    tk = _pick(d, 512, 8)
    src = jnp.arange(blk)
    sel = (jnp.arange(blk)[None, :] == (src // 2 + (src % 2) * LANES)[:, None]).astype(BF16)
    return pl.pallas_call(
        _moe_prep_kernel,
        out_shape=jax.ShapeDtypeStruct((n_exp, n // cw, d, cw), BF16),
        grid=(n_exp, d // tk),
        in_specs=[pl.BlockSpec((1, 1, tk, n), lambda e, k: (layer, e, k, 0)),
                  pl.BlockSpec((blk, blk), lambda e, k: (0, 0))],
        out_specs=pl.BlockSpec((1, n // cw, tk, cw), lambda e, k: (e, 0, k, 0)),
        compiler_params=_cparams(("arbitrary", "arbitrary")),
        name="moe_weight_prep",
    )(w_gate_up, sel)


def _cast_kernel(w_ref, o_ref):
    o_ref[0] = w_ref[0, 0].astype(o_ref.dtype)


def _moe_cast(w_down, layer):
    _, n_exp, f, d = w_down.shape
    tk = _pick(f, 512, 8)
    return pl.pallas_call(
        _cast_kernel,
        out_shape=jax.ShapeDtypeStruct((n_exp, f, d), BF16),
        grid=(n_exp, f // tk),
        in_specs=[pl.BlockSpec((1, 1, tk, d), lambda e, k: (layer, e, k, 0))],
        out_specs=pl.BlockSpec((1, tk, d), lambda e, k: (e, k, 0)),
        compiler_params=_cparams(("arbitrary", "arbitrary")),
        name="moe_weight_cast",
    )(w_down)


def _split_blocks(v):
    lead = v.shape[:-1]
    return v.reshape(lead + (-1, LANES, 2)).swapaxes(-1, -2).reshape(lead + (-1,))


def _moe_kernel(te_ref, na_ref, x_ref, w_ref, b_ref, wd_ref, bd_ref, o_ref):
    i = pl.program_id(0)
    n_act = na_ref[0]

    @pl.when(i < n_act)
    def _():
        lo, hi = _unpack_halves(x_ref[...])
        xb = jnp.concatenate([lo.astype(BF16), hi.astype(BF16)], axis=1)
        acts = []
        for c in range(w_ref.shape[1]):
            h = jnp.dot(xb, w_ref[0, c], preferred_element_type=F32) + b_ref[0, c]
            for q in range(h.shape[1] // (2 * LANES)):
                gate = jnp.minimum(h[:, 2 * q * LANES:(2 * q + 1) * LANES], SWIGLU_LIMIT)
                up = jnp.clip(h[:, (2 * q + 1) * LANES:(2 * q + 2) * LANES], -SWIGLU_LIMIT, SWIGLU_LIMIT)
                acts.append(((up + 1.0) * (gate * jax.nn.sigmoid(SWIGLU_ALPHA * gate))).astype(BF16))
        a = acts[0] if len(acts) == 1 else jnp.concatenate(acts, axis=1)
        y = jnp.dot(a, wd_ref[0], preferred_element_type=F32) + bd_ref[0]
        o_ref[:, 0, :] = _pack_halves(y)

    @pl.when(i >= n_act)
    def _():
        o_ref[...] = jnp.zeros(o_ref.shape, o_ref.dtype)


def _moe_experts(x_sorted, tile_e, n_active, wp, bp, wd, bd, tm):
    r_max, half = x_sorted.shape
    n_exp, n_chunks, d, cw = wp.shape
    f = wd.shape[1]
    n_tiles = r_max // tm
    last = lambda i, te, na: (jnp.minimum(i, jnp.maximum(na[0] - 1, 0)), 0)
    exp3 = lambda i, te, na: (te[i], 0, 0)
    exp4 = lambda i, te, na: (te[i], 0, 0, 0)
    return pl.pallas_call(
        _moe_kernel,
        out_shape=jax.ShapeDtypeStruct((r_max, 1, half), jnp.uint32),
        grid_spec=pltpu.PrefetchScalarGridSpec(
            num_scalar_prefetch=2,
            grid=(n_tiles,),
            in_specs=[pl.BlockSpec((tm, half), last),
                      pl.BlockSpec((1, n_chunks, d, cw), exp4), pl.BlockSpec((1, n_chunks, 1, cw), exp4),
                      pl.BlockSpec((1, f, d), exp3), pl.BlockSpec((1, 1, d), exp3)],
            out_specs=pl.BlockSpec((tm, 1, half), lambda i, te, na: (i, 0, 0))),
        compiler_params=_cparams(("arbitrary",)),
        name="moe_experts",
    )(tile_e, n_active, x_sorted, wp, bp, wd, bd)


def _combine_kernel(pos_ref, posn_ref, gw_ref, x_ref, g_ref, y_hbm, o_ref, buf, sem, *, rows):
    i = pl.program_id(0)
    n = pl.num_programs(0)

    slot = i % 2

    def row_copy(src, k, r, s):
        return pltpu.make_async_copy(y_hbm.at[src], buf.at[s, k, pl.ds(r, 1)], sem.at[s])

    @pl.when(i == 0)
    def _():
        def body(r, c):
            for k in range(TOP_K):
                row_copy(pos_ref[0, 0, r * TOP_K + k], k, r, 0).start()
            return c
        lax.fori_loop(0, rows, body, 0, unroll=2)

    pltpu.make_async_copy(buf.at[1 - slot], buf.at[slot], sem.at[slot]).wait()

    def combine():
        gw = gw_ref[...]
        half = buf.shape[-1]
        acc_lo = acc_hi = None
        for k in range(TOP_K):
            lo, hi = _unpack_halves(buf[slot, k])
            w = gw[:, k:k + 1]
            acc_lo = w * lo if k == 0 else acc_lo + w * lo
            acc_hi = w * hi if k == 0 else acc_hi + w * hi
        g = g_ref[0]
        o_ref[:, :half] = x_ref[:, :half] + g[:, :half] * acc_lo
        o_ref[:, half:] = x_ref[:, half:] + g[:, half:] * acc_hi

    @pl.when(i + 1 < n)
    def _():
        for r in range(rows):
            for k in range(TOP_K):
                row_copy(posn_ref[0, 0, r * TOP_K + k], k, r, 1 - slot).start()
        combine()

    @pl.when(i + 1 == n)
    def _():
        combine()


def _moe_combine(y_sorted, pos, gw2d, x2d, modt, tm_row, ctx_tiles, tiles_per_batch):
    t, d = x2d.shape
    rows = _pick(tm_row, COMBINE_ROWS, 8)
    n_steps = t // rows
    sub = tm_row // rows
    pos3 = pos.reshape(n_steps, 1, rows * TOP_K)

    def gmap(i):
        tile = i // sub
        bb = tile // tiles_per_batch
        seg = jnp.where(tile % tiles_per_batch >= ctx_tiles, 1, 0)
        return (bb * 2 + seg, 0, 5)

    return pl.pallas_call(
        functools.partial(_combine_kernel, rows=rows),
        out_shape=jax.ShapeDtypeStruct((t, d), F32),
        grid=(n_steps,),
        in_specs=[pl.BlockSpec((1, 1, rows * TOP_K), lambda i: (i, 0, 0), memory_space=pltpu.SMEM),
                  pl.BlockSpec((1, 1, rows * TOP_K), lambda i: (jnp.minimum(i + 1, n_steps - 1), 0, 0),
                               memory_space=pltpu.SMEM),
                  pl.BlockSpec((rows, LANES), lambda i: (i, 0)),
                  pl.BlockSpec((rows, d), lambda i: (i, 0)),
                  pl.BlockSpec((1, 1, d), gmap),
                  pl.BlockSpec(memory_space=pl.ANY)],
        out_specs=pl.BlockSpec((rows, d), lambda i: (i, 0)),
        scratch_shapes=[pltpu.VMEM((2, TOP_K, rows, d // 2), jnp.uint32), pltpu.SemaphoreType.DMA((2,))],
        compiler_params=_cparams(("arbitrary",)),
        name="moe_combine",
    )(pos3, pos3, gw2d, x2d, modt, y_sorted)


def _rope_tables(seq, ctx_len):
    n_rows = seq // GRID_W
    axis_rot = HEAD_DIM // 2
    rows = jnp.repeat(jnp.arange(n_rows, dtype=F32), GRID_W)
    cols = jnp.tile(jnp.arange(GRID_W, dtype=F32), n_rows)
    inv_freq = ROPE_BASE ** (-jnp.arange(0, axis_rot, 2, dtype=F32) / axis_rot)
    ang_r = rows[:, None] * inv_freq
    ang_c = cols[:, None] * inv_freq
    ang = jnp.concatenate([ang_r, ang_r, ang_c, ang_c], axis=-1)
    sign = jnp.tile(jnp.concatenate([-jnp.ones(axis_rot // 2, F32), jnp.ones(axis_rot // 2, F32)]), 2)
    cos = jnp.concatenate([jnp.ones((ctx_len, HEAD_DIM), F32), jnp.cos(ang)], axis=0)
    sin = jnp.concatenate([jnp.zeros((ctx_len, HEAD_DIM), F32), jnp.sin(ang) * sign], axis=0)
    return cos, sin


def kernel(x, c, ctx, c_ctx, w_mod, b_mod, norm_mix, norm_ffn, w_router, b_router, w_gate_up, b_gate_up, w_down, b_down, attn_w_qkv, attn_w_o, attn_q_gain, attn_k_gain, attn_sinks, ssm_a_re, ssm_a_im, ssm_log_dt, ssm_b_re, ssm_b_im, ssm_c_re, ssm_c_im, ssm_d, ssm_w_glu, ssm_b_glu):
    b, seq, d = x.shape
    ctx_len = ctx.shape[1]
    depth = w_mod.shape[0]
    p = ctx_len + seq
    t = b * p
    n_exp = w_router.shape[-1]
    n_q = d // HEAD_DIM
    n_kv = (attn_w_qkv.shape[-1] // HEAD_DIM - n_q) // 2
    assert ctx_len % ATT_BLOCK == 0 and seq % ATT_BLOCK == 0 and seq % GRID_W == 0
    assert d % (2 * LANES) == 0 and ctx_len % SSM_CHUNK == 0 and seq % SSM_CHUNK == 0

    tm = _pick(math.gcd(p, ctx_len), ROW_TILE)
    mm_tm = _pick(math.gcd(p, ctx_len), MM_TILE_M)
    tiles_pb = p // mm_tm
    ctx_tiles = ctx_len // mm_tm

    pad = (-(b + 1)) % 8
    cvec = jnp.concatenate([c, c_ctx[None, :], jnp.zeros((pad, d), F32)], axis=0)
    mod_all = _mod_all(cvec, w_mod, b_mod)

    cos_t, sin_t = _rope_tables(seq, ctx_len)
    xs = jnp.concatenate([ctx, x], axis=1)

    tm_moe = min(MOE_TILE, _pick(t * TOP_K, MOE_TILE, 8))
    n_tiles = (t * TOP_K) // tm_moe + n_exp
    scale = HEAD_DIM ** -0.5

    for i in range(depth):
        j = i // 2
        lat = mod_all[i, :b]
        cx = jnp.broadcast_to(mod_all[i, b][None], lat.shape)
        modt = jnp.stack([cx, lat], axis=1).reshape(b * 2, 1, 6 * d)

        x2d = xs.reshape(t, d)
        if i % 2 == 0:
            gain = jnp.concatenate([jnp.tile(attn_q_gain[j] * scale, n_q), jnp.tile(attn_k_gain[j], n_kv),
                                    jnp.ones((n_kv * HEAD_DIM,), F32)]).reshape(1, -1)
            qkv = _qkv_proj(x2d, norm_mix[i], modt, attn_w_qkv[j].astype(BF16), gain, cos_t, sin_t,
                            n_q + n_kv, mm_tm, ctx_tiles, tiles_pb)
            o = _attention(qkv.reshape(b, p, -1), attn_sinks[j].astype(F32), n_q, n_kv, ctx_len)
            x2d = _proj_residual(o.reshape(t, d), attn_w_o[j].astype(BF16), x2d, modt, 2,
                                 mm_tm, ctx_tiles, tiles_pb)
        else:
            u = _norm_mod(xs, norm_mix[i], modt, 0, ctx_len)
            tables = _s5_tables(ssm_a_re[j], ssm_a_im[j], ssm_log_dt[j], ssm_b_re[j], ssm_b_im[j],
                                ssm_c_re[j], ssm_c_im[j], ssm_d[j])
            gy = _s5_mix(u, tables, ctx_len)
            x2d = _glu_residual(gy.reshape(t, d), ssm_w_glu[j].astype(BF16), ssm_b_glu[j], x2d, modt, 2,
                                mm_tm, ctx_tiles, tiles_pb)

        hp, idx, gw, rank, counts = _norm_router(x2d.reshape(b, p, d), norm_ffn[i], modt, w_router[i],
                                                 b_router[i], ctx_len)
        tile_e, n_active, pos = _moe_schedule(idx.reshape(t, LANES)[:, :TOP_K],
                                              rank.reshape(t, LANES)[:, :TOP_K],
                                              counts.reshape(n_exp).astype(jnp.int32), tm_moe, n_tiles)
        x_sorted = _moe_scatter(hp.reshape(t, d // 2), pos, n_tiles * tm_moe, tm)
        wp = _moe_prep(w_gate_up, i)
        bp = _split_blocks(b_gate_up[i]).reshape(n_exp, wp.shape[1], 1, wp.shape[3])
        y_sorted = _moe_experts(x_sorted, tile_e, n_active, wp, bp,
                                _moe_cast(w_down, i), b_down[i].reshape(n_exp, 1, d), tm_moe)
        x2d = _moe_combine(y_sorted, pos, gw.reshape(t, LANES), x2d, modt, tm, ctx_len // tm, p // tm)
        xs = x2d.reshape(b, p, d)

    return xs[:, ctx_len:]
```

```python
import functools
import math

import jax
import jax.numpy as jnp
from jax import lax
from jax.experimental import pallas as pl
from jax.experimental.pallas import tpu as pltpu

F32 = jnp.float32
BF16 = jnp.bfloat16

HEAD_DIM = 128
ATT_BLOCK = 128
GRID_W = 64
ROPE_BASE = 10000.0
SSM_GROUP = 16
SSM_STATE = 64
SSM_CHUNK = 16
TOP_K = 4
SWIGLU_LIMIT = 7.0
SWIGLU_ALPHA = 1.702
EPS = 1e-6
NEG_INF = -1e30
DT_FLOOR_RE = -1e-4

LANES = 128
VMEM_LIMIT = 56 * 1024 * 1024

ROW_TILE = 256
MM_TILE_M = 512
PROJ_TILE_N = 2048
GLU_TILE_N = 1024
MOE_TILE = 512
COMBINE_ROWS = 128
SSM_GROUP_BLOCK = 16


def _cparams(sem):
    return pltpu.CompilerParams(dimension_semantics=sem, vmem_limit_bytes=VMEM_LIMIT)


def _pick(n, pref, mult=8):
    t = min(n, pref)
    while n % t or t % mult:
        t -= 1
    return t


def _mod_kernel(c_ref, w_ref, b_ref, o_ref):
    c = c_ref[...]
    s = c * jax.nn.sigmoid(c)
    o_ref[0] = jnp.dot(s.astype(BF16), w_ref[0].astype(BF16), preferred_element_type=F32) + b_ref[0]


def _mod_all(cvec, w_mod, b_mod):
    depth, d, n = w_mod.shape
    tn = _pick(n, 1024, LANES)
    return pl.pallas_call(
        _mod_kernel,
        out_shape=jax.ShapeDtypeStruct((depth, cvec.shape[0], n), F32),
        grid=(depth, n // tn),
        in_specs=[pl.BlockSpec(cvec.shape, lambda l, j: (0, 0)),
                  pl.BlockSpec((1, d, tn), lambda l, j: (l, 0, j)),
                  pl.BlockSpec((1, 1, tn), lambda l, j: (l, 0, j))],
        out_specs=pl.BlockSpec((1, cvec.shape[0], tn), lambda l, j: (l, 0, j)),
        compiler_params=_cparams(("arbitrary", "arbitrary")),
        name="adaln_rows",
    )(cvec, w_mod, b_mod.reshape(depth, 1, n))


def _normed(x, w, sh, sc):
    r = lax.rsqrt(jnp.mean(x * x, axis=-1, keepdims=True) + EPS)
    return (x * r * w) * (1.0 + sc) + sh


def _norm_mod_kernel(x_ref, w_ref, sh_ref, sc_ref, o_ref):
    o_ref[0] = _normed(x_ref[0], w_ref[...], sh_ref[0], sc_ref[0]).astype(o_ref.dtype)


def _mod_spec(d, which, ctx_tiles):
    return pl.BlockSpec((1, 1, d), lambda b, i: (b * 2 + jnp.where(i >= ctx_tiles, 1, 0), 0, which))


def _norm_mod(x, w, modt, which_shift, ctx_len):
    b, p, d = x.shape
    tm = _pick(math.gcd(p, ctx_len), ROW_TILE)
    ct = ctx_len // tm
    return pl.pallas_call(
        _norm_mod_kernel,
        out_shape=jax.ShapeDtypeStruct((b, p, d), BF16),
        grid=(b, p // tm),
        in_specs=[pl.BlockSpec((1, tm, d), lambda bb, i: (bb, i, 0)),
                  pl.BlockSpec((1, d), lambda bb, i: (0, 0)),
                  _mod_spec(d, which_shift, ct), _mod_spec(d, which_shift + 1, ct)],
        out_specs=pl.BlockSpec((1, tm, d), lambda bb, i: (bb, i, 0)),
        compiler_params=_cparams(("arbitrary", "arbitrary")),
        name="norm_mod",
    )(x, w.reshape(1, d), modt, modt)


def _pack_halves(h):
    half = h.shape[-1] // 2
    bits = lax.bitcast_convert_type(h.astype(BF16).astype(F32), jnp.uint32)
    return (bits[:, :half] >> 16) | (bits[:, half:] & jnp.uint32(0xFFFF0000))


def _unpack_halves(xp):
    lo = lax.bitcast_convert_type(xp << 16, F32)
    hi = lax.bitcast_convert_type(xp & jnp.uint32(0xFFFF0000), F32)
    return lo, hi


def _norm_router_kernel(x_ref, w_ref, sh_ref, sc_ref, wr_ref, br_ref, hp_ref, idx_ref, gw_ref, rank_ref,
                        cnt_ref, base):
    @pl.when((pl.program_id(0) == 0) & (pl.program_id(1) == 0))
    def _():
        base[...] = jnp.zeros(base.shape, F32)

    h = _normed(x_ref[0], w_ref[...], sh_ref[0], sc_ref[0])
    hp_ref[0] = _pack_halves(h)
    logits = jnp.dot(h, wr_ref[...], precision=lax.Precision.HIGHEST,
                     preferred_element_type=F32) + br_ref[...]
    n_exp = logits.shape[-1]
    lane = lax.broadcasted_iota(jnp.int32, logits.shape, 1)
    out_lane = lax.broadcasted_iota(jnp.int32, idx_ref.shape[1:], 1)
    vals, idxs = [], []
    rest = logits
    for _ in range(TOP_K):
        m = jnp.max(rest, axis=-1, keepdims=True)
        idx = jnp.min(jnp.where(rest == m, lane, n_exp), axis=-1, keepdims=True)
        vals.append(m)
        idxs.append(idx)
        rest = jnp.where(lane == idx, -jnp.inf, rest)
    exps = [jnp.exp(v - vals[0]) for v in vals]
    tot = exps[0]
    for e in exps[1:]:
        tot = tot + e
    idx_out = jnp.zeros(idx_ref.shape[1:], jnp.int32)
    gw_out = jnp.zeros(gw_ref.shape[1:], F32)
    for k in range(TOP_K):
        idx_out = jnp.where(out_lane == k, idxs[k], idx_out)
        gw_out = jnp.where(out_lane == k, exps[k] / tot, gw_out)
    idx_ref[0] = idx_out
    gw_ref[0] = gw_out

    sel = jnp.zeros(logits.shape, F32)
    for k in range(TOP_K):
        sel = sel + (lane == idxs[k]).astype(F32)
    tm = sel.shape[0]
    earlier = (lax.broadcasted_iota(jnp.int32, (tm, tm), 1)
               < lax.broadcasted_iota(jnp.int32, (tm, tm), 0)).astype(BF16)
    before = jnp.dot(earlier, sel.astype(BF16), preferred_element_type=F32) + base[...]
    rank_out = jnp.zeros(rank_ref.shape[1:], jnp.int32)
    for k in range(TOP_K):
        rk = jnp.sum(jnp.where(lane == idxs[k], before, 0.0), axis=-1, keepdims=True)
        rank_out = jnp.where(out_lane == k, rk.astype(jnp.int32), rank_out)
    rank_ref[0] = rank_out
    base[...] = base[...] + jnp.sum(sel, axis=0, keepdims=True)
    cnt_ref[...] = base[...]


def _norm_router(x, w, modt, w_router, b_router, ctx_len):
    b, p, d = x.shape
    n_exp = w_router.shape[-1]
    tm = _pick(math.gcd(p, ctx_len), ROW_TILE)
    ct = ctx_len // tm
    row = lambda bb, i: (bb, i, 0)
    return pl.pallas_call(
        _norm_router_kernel,
        out_shape=(jax.ShapeDtypeStruct((b, p, d // 2), jnp.uint32),
                   jax.ShapeDtypeStruct((b, p, LANES), jnp.int32),
                   jax.ShapeDtypeStruct((b, p, LANES), F32),
                   jax.ShapeDtypeStruct((b, p, LANES), jnp.int32),
                   jax.ShapeDtypeStruct((1, n_exp), F32)),
        grid=(b, p // tm),
        in_specs=[pl.BlockSpec((1, tm, d), row),
                  pl.BlockSpec((1, d), lambda bb, i: (0, 0)),
                  _mod_spec(d, 3, ct), _mod_spec(d, 4, ct),
                  pl.BlockSpec((d, n_exp), lambda bb, i: (0, 0)),
                  pl.BlockSpec((1, n_exp), lambda bb, i: (0, 0))],
        out_specs=(pl.BlockSpec((1, tm, d // 2), row),
                   pl.BlockSpec((1, tm, LANES), row),
                   pl.BlockSpec((1, tm, LANES), row),
                   pl.BlockSpec((1, tm, LANES), row),
                   pl.BlockSpec((1, n_exp), lambda bb, i: (0, 0))),
        scratch_shapes=[pltpu.VMEM((1, n_exp), F32)],
        compiler_params=_cparams(("arbitrary", "arbitrary")),
        name="norm_router",
    )(x, w.reshape(1, d), modt, modt, w_router, b_router.reshape(1, n_exp))


def _qkv_kernel(x_ref, nw_ref, sh_ref, sc_ref, w_ref, g_ref, cos_ref, sin_ref, o_ref, *, n_norm_heads):
    u = _normed(x_ref[...], nw_ref[...], sh_ref[0], sc_ref[0]).astype(BF16)
    acc = jnp.dot(u, w_ref[...], preferred_element_type=F32)
    cos = cos_ref[...]
    sin = sin_ref[...]
    lane = lax.broadcasted_iota(jnp.int32, cos.shape, 1)
    first = (lane // (HEAD_DIM // 4)) % 2 == 0
    for hh in range(acc.shape[1] // HEAD_DIM):
        sl = slice(hh * HEAD_DIM, (hh + 1) * HEAD_DIM)
        xh = acc[:, sl]
        if hh < n_norm_heads:
            r = lax.rsqrt(jnp.mean(xh * xh, axis=-1, keepdims=True) + EPS)
            y = xh * r * g_ref[:, sl]
            partner = jnp.where(first, pltpu.roll(y, HEAD_DIM - HEAD_DIM // 4, 1),
                                pltpu.roll(y, HEAD_DIM // 4, 1))
            xh = y * cos + partner * sin
        o_ref[:, sl] = xh.astype(o_ref.dtype)


def _row_mod_spec(d, which, ctx_tiles, tiles_per_batch):
    def imap(i):
        seg = jnp.where(i % tiles_per_batch >= ctx_tiles, 1, 0)
        return ((i // tiles_per_batch) * 2 + seg, 0, which)

    return pl.BlockSpec((1, 1, d), imap)


def _qkv_proj(x2d, norm_w, modt, w_bf16, gain_row, cos_t, sin_t, n_norm_heads, tm, ctx_tiles, tiles_per_batch):
    m, k = x2d.shape
    n = w_bf16.shape[1]
    pos = lambda i: (i % tiles_per_batch, 0)
    return pl.pallas_call(
        functools.partial(_qkv_kernel, n_norm_heads=n_norm_heads),
        out_shape=jax.ShapeDtypeStruct((m, n), BF16),
        grid=(m // tm,),
        in_specs=[pl.BlockSpec((tm, k), lambda i: (i, 0)),
                  pl.BlockSpec((1, k), lambda i: (0, 0)),
                  _row_mod_spec(k, 0, ctx_tiles, tiles_per_batch),
                  _row_mod_spec(k, 1, ctx_tiles, tiles_per_batch),
                  pl.BlockSpec((k, n), lambda i: (0, 0)),
                  pl.BlockSpec((1, n), lambda i: (0, 0)),
                  pl.BlockSpec((tm, HEAD_DIM), pos),
                  pl.BlockSpec((tm, HEAD_DIM), pos)],
        out_specs=pl.BlockSpec((tm, n), lambda i: (i, 0)),
        compiler_params=_cparams(("arbitrary",)),
        name="qkv_proj",
    )(x2d, norm_w.reshape(1, k), modt, modt, w_bf16, gain_row, cos_t, sin_t)


def _attn_kernel(sink_ref, q_ref, kc_ref, vc_ref, kp_ref, ks_ref, kn_ref, vp_ref, vs_ref, vn_ref,
                 o_ref, *, q_per_kv, heads, ctx_blocks, n_blocks):
    hg = pl.program_id(1)
    i = pl.program_id(2)
    blk = ATT_BLOCK
    rows = q_per_kv * blk
    qw = q_per_kv * HEAD_DIM
    dn = (((1,), (1,)), ((), ()))

    qi = lax.broadcasted_iota(jnp.int32, (rows, blk), 0) % blk
    kj = lax.broadcasted_iota(jnp.int32, (rows, blk), 1)
    is_lat = i >= ctx_blocks
    ok_prev = (kj >= qi) & (i > ctx_blocks)
    ok_self = jnp.broadcast_to(is_lat, (rows, blk))
    ok_next = (kj <= qi) & is_lat & (i < n_blocks - 1)
    valid = jnp.concatenate([ok_prev, ok_self, ok_next], axis=1)
    row_head = lax.broadcasted_iota(jnp.int32, (rows, 1), 0) // blk

    for hh in range(heads):
        hd = slice(hh * HEAD_DIM, (hh + 1) * HEAD_DIM)
        q = q_ref[0, :, hh * qw:(hh + 1) * qw]
        qs = jnp.concatenate([q[:, g * HEAD_DIM:(g + 1) * HEAD_DIM] for g in range(q_per_kv)], axis=0)
        k_loc = jnp.concatenate([kp_ref[0, :, hd], ks_ref[0, :, hd], kn_ref[0, :, hd]], axis=0)
        v_loc = jnp.concatenate([vp_ref[0, :, hd], vs_ref[0, :, hd], vn_ref[0, :, hd]], axis=0)
        s_loc = jnp.where(valid, lax.dot_general(qs, k_loc, dn, preferred_element_type=F32), NEG_INF)
        s_ctx = lax.dot_general(qs, kc_ref[0, :, hd], dn, preferred_element_type=F32)

        sink = jnp.zeros((rows, 1), F32)
        for g in range(q_per_kv):
            sink = jnp.where(row_head == g, sink_ref[(hg * heads + hh) * q_per_kv + g], sink)

        m = jnp.maximum(jnp.maximum(jnp.max(s_loc, axis=-1, keepdims=True),
                                    jnp.max(s_ctx, axis=-1, keepdims=True)), sink)
        p_loc = jnp.exp(s_loc - m)
        p_ctx = jnp.exp(s_ctx - m)
        denom = (jnp.sum(p_loc, axis=-1, keepdims=True) + jnp.sum(p_ctx, axis=-1, keepdims=True)
                 + jnp.exp(sink - m))
        o = (jnp.dot(p_loc.astype(BF16), v_loc, preferred_element_type=F32)
             + jnp.dot(p_ctx.astype(BF16), vc_ref[0, :, hd], preferred_element_type=F32)) / denom
        o = o.astype(o_ref.dtype)
        for g in range(q_per_kv):
            o_ref[0, :, hh * qw + g * HEAD_DIM:hh * qw + (g + 1) * HEAD_DIM] = o[g * blk:(g + 1) * blk]


def _attention(qkv, sinks, n_q, n_kv, ctx_len):
    b, p, _ = qkv.shape
    blk = ATT_BLOCK
    nb = p // blk
    cb = ctx_len // blk
    qpk = n_q // n_kv
    heads = 2 if (n_kv % 2 == 0 and n_q % 2 == 0) else 1
    qw = heads * qpk * HEAD_DIM
    hw = heads * HEAD_DIM
    kcol = n_q // heads
    vcol = (n_q + n_kv) // heads

    def loc(col0, off):
        return pl.BlockSpec(
            (1, blk, hw),
            lambda bb, h, i, s: (bb, jnp.clip(i + off, cb, nb - 1), col0 + h))

    return pl.pallas_call(
        functools.partial(_attn_kernel, q_per_kv=qpk, heads=heads, ctx_blocks=cb, n_blocks=nb),
        out_shape=jax.ShapeDtypeStruct((b, p, n_q * HEAD_DIM), BF16),
        grid_spec=pltpu.PrefetchScalarGridSpec(
            num_scalar_prefetch=1,
            grid=(b, n_kv // heads, nb),
            in_specs=[pl.BlockSpec((1, blk, qw), lambda bb, h, i, s: (bb, i, h)),
                      pl.BlockSpec((1, ctx_len, hw), lambda bb, h, i, s: (bb, 0, kcol + h)),
                      pl.BlockSpec((1, ctx_len, hw), lambda bb, h, i, s: (bb, 0, vcol + h)),
                      loc(kcol, -1), loc(kcol, 0), loc(kcol, 1),
                      loc(vcol, -1), loc(vcol, 0), loc(vcol, 1)],
            out_specs=pl.BlockSpec((1, blk, qw), lambda bb, h, i, s: (bb, i, h))),
        compiler_params=_cparams(("arbitrary", "arbitrary", "arbitrary")),
        name="window_attn",
    )(sinks, qkv, qkv, qkv, qkv, qkv, qkv, qkv, qkv, qkv)


def _gate_spec(d, which, ctx_tiles, tiles_per_batch, tn):
    per = d // tn

    def imap(j, i):
        bb = i // tiles_per_batch
        seg = jnp.where(i % tiles_per_batch >= ctx_tiles, 1, 0)
        return (bb * 2 + seg, 0, which * per + j)

    return pl.BlockSpec((1, 1, tn), imap)


def _proj_res_kernel(a_ref, w_ref, x_ref, g_ref, o_ref):
    y = jnp.dot(a_ref[...], w_ref[...], preferred_element_type=F32)
    o_ref[...] = x_ref[...] + g_ref[0] * y


def _proj_residual(a2d, w_bf16, x2d, modt, which_gate, tm, ctx_tiles, tiles_per_batch):
    m, k = a2d.shape
    n = w_bf16.shape[1]
    tn = _pick(n, PROJ_TILE_N, LANES)
    return pl.pallas_call(
        _proj_res_kernel,
        out_shape=jax.ShapeDtypeStruct((m, n), F32),
        grid=(n // tn, m // tm),
        in_specs=[pl.BlockSpec((tm, k), lambda j, i: (i, 0)),
                  pl.BlockSpec((k, tn), lambda j, i: (0, j)),
                  pl.BlockSpec((tm, tn), lambda j, i: (i, j)),
                  _gate_spec(n, which_gate, ctx_tiles, tiles_per_batch, tn)],
        out_specs=pl.BlockSpec((tm, tn), lambda j, i: (i, j)),
        compiler_params=_cparams(("arbitrary", "arbitrary")),
        name="proj_residual",
    )(a2d, w_bf16, x2d, modt)


def _glu_res_kernel(a_ref, w1_ref, w2_ref, b1_ref, b2_ref, x_ref, g_ref, o_ref):
    a = a_ref[...]
    z1 = jnp.dot(a, w1_ref[...], preferred_element_type=F32) + b1_ref[...]
    z2 = jnp.dot(a, w2_ref[...], preferred_element_type=F32) + b2_ref[...]
    o_ref[...] = x_ref[...] + g_ref[0] * (z1 * jax.nn.sigmoid(z2))


def _glu_residual(a2d, w_bf16, b_glu, x2d, modt, which_gate, tm, ctx_tiles, tiles_per_batch):
    m, k = a2d.shape
    d = w_bf16.shape[1] // 2
    tn = _pick(d, GLU_TILE_N, LANES)
    nj = d // tn
    b2 = b_glu.reshape(1, 2 * d)
    return pl.pallas_call(
        _glu_res_kernel,
        out_shape=jax.ShapeDtypeStruct((m, d), F32),
        grid=(nj, m // tm),
        in_specs=[pl.BlockSpec((tm, k), lambda j, i: (i, 0)),
                  pl.BlockSpec((k, tn), lambda j, i: (0, j)),
                  pl.BlockSpec((k, tn), lambda j, i: (0, j + nj)),
                  pl.BlockSpec((1, tn), lambda j, i: (0, j)),
                  pl.BlockSpec((1, tn), lambda j, i: (0, j + nj)),
                  pl.BlockSpec((tm, tn), lambda j, i: (i, j)),
                  _gate_spec(d, which_gate, ctx_tiles, tiles_per_batch, tn)],
        out_specs=pl.BlockSpec((tm, tn), lambda j, i: (i, j)),
        compiler_params=_cparams(("arbitrary", "arbitrary")),
        name="glu_residual",
    )(a2d, w_bf16, w_bf16, b2, b2, x2d, modt)


def _s5_tables(a_re, a_im, log_dt, b_re, b_im, c_re, c_im, d_skip):
    q, n_state, pch = SSM_CHUNK, a_re.shape[-1], b_re.shape[-1]
    g = a_re.shape[1]
    hp = lax.Precision.HIGHEST
    lam = lax.complex(jnp.minimum(a_re.astype(F32), DT_FLOOR_RE), a_im.astype(F32))
    lam_dt = lam * jnp.exp(log_dt.astype(F32))[..., None]
    b_bar = ((jnp.exp(lam_dt) - 1.0) / lam)[..., None] * lax.complex(b_re.astype(F32), b_im.astype(F32))
    c_mat = lax.complex(c_re.astype(F32), c_im.astype(F32))
    tau = jnp.arange(q + 1, dtype=F32)
    pw = jnp.exp(lam_dt[:, :, None, :] * tau[None, None, :, None])
    kern = jnp.einsum('dgpn,dgtn,dgnr->dgtpr', c_mat, pw[:, :, :q], b_bar, precision=hp).real
    t_i = jnp.arange(q)[:, None]
    s_i = jnp.arange(q)[None, :]
    kf = jnp.where((t_i >= s_i)[None, :, :, None, None], kern[0][:, jnp.clip(t_i - s_i, 0, q - 1)], 0.0)
    kb = jnp.where((s_i >= t_i)[None, :, :, None, None], kern[1][:, jnp.clip(s_i - t_i, 0, q - 1)], 0.0)
    dsk = d_skip.astype(F32).reshape(g, pch)
    diag = (jnp.eye(q)[None, :, :, None, None] * jnp.eye(pch)[None, None, None] * dsk[:, None, None, :, None])
    mt = (kf + kb + diag).transpose(0, 2, 4, 1, 3).reshape(g, q * pch, q * pch)
    wsf = pw[0][:, ::-1][:, 1:, :, None] * b_bar[0][:, None]
    wsb = pw[1][:, :q, :, None] * b_bar[1][:, None]
    wsf = wsf.transpose(0, 1, 3, 2).reshape(g, q * pch, n_state)
    wsb = wsb.transpose(0, 1, 3, 2).reshape(g, q * pch, n_state)
    w1 = jnp.concatenate([mt, wsf.real, wsb.real, wsf.imag, wsb.imag], axis=-1)
    cf = c_mat[0][:, None] * pw[0][:, 1:, None, :]
    cb = c_mat[1][:, None] * pw[1][:, ::-1][:, :q, None, :]
    cf = cf.transpose(0, 3, 1, 2).reshape(g, n_state, q * pch)
    cb = cb.transpose(0, 3, 1, 2).reshape(g, n_state, q * pch)
    w3 = jnp.concatenate([cf.real, cb.real, -cf.imag, -cb.imag], axis=1)
    aq = pw[:, :, q]
    a_r = jnp.concatenate([aq[0].real, aq[1].real], axis=-1)
    a_i = jnp.concatenate([aq[0].imag, aq[1].imag], axis=-1)
    return w1.astype(BF16), w3.astype(BF16), a_r, a_i


def _s5_in_kernel(u_ref, w_ref, y_ref, sr_ref, si_ref):
    width = u_ref.shape[-1]
    ns2 = sr_ref.shape[-1]
    for g in range(u_ref.shape[0]):
        r = jnp.dot(u_ref[g], w_ref[g], preferred_element_type=F32)
        y_ref[g] = r[:, :width]
        sr_ref[0, g] = r[:, width:width + ns2]
        si_ref[0, g] = r[:, width + ns2:]


def _s5_scan_kernel(sr_ref, si_ref, ar_ref, ai_ref, hr_ref, hi_ref, fr, fi, br, bi, *, n_chunks, ctx_chunks):
    gb = ar_ref.shape[0]
    a_r = ar_ref[...]
    a_i = ai_ref[...]
    lane = lax.broadcasted_iota(jnp.int32, a_r.shape, 1)
    fwd = lane < (a_r.shape[1] // 2)

    def rows(ref, c):
        return ref[pl.ds(c, gb, stride=n_chunks), :]

    def body(i, carry):
        h_r, h_i = carry
        cf = i
        cb = jnp.where(i < ctx_chunks, ctx_chunks - 1 - i, n_chunks + ctx_chunks - 1 - i)
        fr[pl.ds(cf, gb, stride=n_chunks), :] = h_r
        fi[pl.ds(cf, gb, stride=n_chunks), :] = h_i
        br[pl.ds(cb, gb, stride=n_chunks), :] = h_r
        bi[pl.ds(cb, gb, stride=n_chunks), :] = h_i
        s_r = jnp.where(fwd, rows(sr_ref, cf), rows(sr_ref, cb))
        s_i = jnp.where(fwd, rows(si_ref, cf), rows(si_ref, cb))
        return (a_r * h_r - a_i * h_i + s_r, a_r * h_i + a_i * h_r + s_i)

    zero = jnp.zeros(a_r.shape, F32)
    lax.fori_loop(0, n_chunks, body, (zero, zero))
    full = lax.broadcasted_iota(jnp.int32, hr_ref.shape, 1) < (a_r.shape[1] // 2)
    hr_ref[...] = jnp.where(full, fr[...], br[...])
    hi_ref[...] = jnp.where(full, fi[...], bi[...])


def _s5_out_kernel(y_ref, hr_ref, hi_ref, w_ref, o_ref):
    ns2 = hr_ref.shape[-1]
    for g in range(y_ref.shape[0]):
        y = (y_ref[g]
             + jnp.dot(hr_ref[0, g].astype(BF16), w_ref[g, :ns2], preferred_element_type=F32)
             + jnp.dot(hi_ref[0, g].astype(BF16), w_ref[g, ns2:], preferred_element_type=F32))
        o_ref[g] = jax.nn.gelu(y).astype(o_ref.dtype)


def _s5_mix(u, tables, ctx_len):
    w1, w3, a_r, a_i = tables
    b, p, d = u.shape
    q, pch = SSM_CHUNK, SSM_GROUP
    g = d // pch
    nc = p // q
    width = q * pch
    ns2 = 2 * SSM_STATE
    gb = _pick(g, SSM_GROUP_BLOCK, 8)
    ut = u.reshape(b, nc, q, g, pch).transpose(3, 0, 1, 2, 4).reshape(g, b * nc, width)

    y_in, s_r, s_i = pl.pallas_call(
        _s5_in_kernel,
        out_shape=(jax.ShapeDtypeStruct((g, b * nc, width), F32),
                   jax.ShapeDtypeStruct((b, g, nc, ns2), F32),
                   jax.ShapeDtypeStruct((b, g, nc, ns2), F32)),
        grid=(g // gb, b),
        in_specs=[pl.BlockSpec((gb, nc, width), lambda gi, bb: (gi, bb, 0)),
                  pl.BlockSpec((gb, width, width + 2 * ns2), lambda gi, bb: (gi, 0, 0))],
        out_specs=(pl.BlockSpec((gb, nc, width), lambda gi, bb: (gi, bb, 0)),
                   pl.BlockSpec((1, gb, nc, ns2), lambda gi, bb: (bb, gi, 0, 0)),
                   pl.BlockSpec((1, gb, nc, ns2), lambda gi, bb: (bb, gi, 0, 0))),
        compiler_params=_cparams(("arbitrary", "arbitrary")),
        name="s5_chunk_in",
    )(ut, w1)

    flat = lambda gi, bb: (bb * (g // gb) + gi, 0)
    h_r, h_i = pl.pallas_call(
        functools.partial(_s5_scan_kernel, n_chunks=nc, ctx_chunks=ctx_len // q),
        out_shape=(jax.ShapeDtypeStruct((b * g * nc, ns2), F32),) * 2,
        grid=(g // gb, b),
        in_specs=[pl.BlockSpec((gb * nc, ns2), flat), pl.BlockSpec((gb * nc, ns2), flat),
                  pl.BlockSpec((gb, ns2), lambda gi, bb: (gi, 0)),
                  pl.BlockSpec((gb, ns2), lambda gi, bb: (gi, 0))],
        out_specs=(pl.BlockSpec((gb * nc, ns2), flat),) * 2,
        scratch_shapes=[pltpu.VMEM((gb * nc, ns2), F32)] * 4,
        compiler_params=_cparams(("arbitrary", "arbitrary")),
        name="s5_chunk_scan",
    )(s_r.reshape(b * g * nc, ns2), s_i.reshape(b * g * nc, ns2), a_r, a_i)

    yt = pl.pallas_call(
        _s5_out_kernel,
        out_shape=jax.ShapeDtypeStruct((g, b * nc, width), BF16),
        grid=(g // gb, b),
        in_specs=[pl.BlockSpec((gb, nc, width), lambda gi, bb: (gi, bb, 0)),
                  pl.BlockSpec((1, gb, nc, ns2), lambda gi, bb: (bb, gi, 0, 0)),
                  pl.BlockSpec((1, gb, nc, ns2), lambda gi, bb: (bb, gi, 0, 0)),
                  pl.BlockSpec((gb, 2 * ns2, width), lambda gi, bb: (gi, 0, 0))],
        out_specs=pl.BlockSpec((gb, nc, width), lambda gi, bb: (gi, bb, 0)),
        compiler_params=_cparams(("arbitrary", "arbitrary")),
        name="s5_chunk_out",
    )(y_in, h_r.reshape(b, g, nc, ns2), h_i.reshape(b, g, nc, ns2), w3)
    return yt.reshape(g, b, nc, q, pch).transpose(1, 2, 3, 0, 4).reshape(b, p, d)


def _moe_schedule(idx, rank, counts, tm, n_tiles):
    n_exp = counts.shape[0]
    padded = (counts + tm - 1) // tm * tm
    gstart = jnp.cumsum(padded) - padded
    gend = gstart + padded
    n_active = (jnp.sum(padded) // tm).astype(jnp.int32)
    tile_start = jnp.arange(n_tiles, dtype=jnp.int32) * tm
    tile_e = jnp.sum((gend[None, :] <= tile_start[:, None]).astype(jnp.int32), axis=1)
    tile_e = jnp.minimum(tile_e, n_exp - 1)
    tile_e = jnp.where(jnp.arange(n_tiles) < n_active, tile_e, tile_e[jnp.maximum(n_active - 1, 0)])
    start = jnp.zeros(idx.shape, jnp.int32)
    for e in range(n_exp):
        start = jnp.where(idx == e, gstart[e], start)
    pos = (start + rank)[:, :TOP_K].astype(jnp.int32)
    return tile_e.astype(jnp.int32), n_active.reshape(1), pos


def _scatter_kernel(pos_ref, h_ref, zero_hbm, o_hbm, size_buf, sem, *, rows):
    del zero_hbm

    def body(r, c):
        for k in range(TOP_K):
            pltpu.make_async_copy(h_ref.at[pl.ds(r, 1)], o_hbm.at[pl.ds(pos_ref[0, 0, r * TOP_K + k], 1)],
                                  sem).start()
        return c
    lax.fori_loop(0, rows, body, 0, unroll=4)
    for _ in range(TOP_K):
        pltpu.make_async_copy(size_buf, size_buf, sem).wait()


def _moe_scatter(hp2d, pos, n_rows, rows):
    t, half = hp2d.shape
    n_steps = t // rows
    zeros = jnp.zeros((n_rows, half), hp2d.dtype)
    return pl.pallas_call(
        functools.partial(_scatter_kernel, rows=rows),
        out_shape=jax.ShapeDtypeStruct((n_rows, half), hp2d.dtype),
        grid=(n_steps,),
        in_specs=[pl.BlockSpec((1, 1, rows * TOP_K), lambda i: (i, 0, 0), memory_space=pltpu.SMEM),
                  pl.BlockSpec((rows, half), lambda i: (i, 0)),
                  pl.BlockSpec(memory_space=pl.ANY)],
        out_specs=pl.BlockSpec(memory_space=pl.ANY),
        scratch_shapes=[pltpu.VMEM((rows, half), hp2d.dtype), pltpu.SemaphoreType.DMA(())],
        input_output_aliases={2: 0},
        compiler_params=_cparams(("arbitrary",)),
        name="moe_scatter",
    )(pos.reshape(n_steps, 1, rows * TOP_K), hp2d, zeros)


def _moe_prep_kernel(w_ref, sel_ref, o_ref):
    sel = sel_ref[...]
    blk = sel.shape[0]
    per = o_ref.shape[3] // blk
    for c in range(w_ref.shape[3] // blk):
        w = w_ref[0, 0, :, c * blk:(c + 1) * blk].astype(BF16)
        o_ref[0, c // per, :, (c % per) * blk:(c % per + 1) * blk] = jnp.dot(
            w, sel, preferred_element_type=F32).astype(BF16)


def _moe_chunk_width(n):
    return 4 * LANES if n % (4 * LANES) == 0 else 2 * LANES


def _moe_prep(w_gate_up, layer):
    _, n_exp, d, n = w_gate_up.shape
    blk = 2 * LANES
    cw = _moe_chunk_width(n)
    tk = _pick(d, 512, 8)
    src = jnp.arange(blk)
    sel = (jnp.arange(blk)[None, :] == (src // 2 + (src % 2) * LANES)[:, None]).astype(BF16)
    return pl.pallas_call(
        _moe_prep_kernel,
        out_shape=jax.ShapeDtypeStruct((n_exp, n // cw, d, cw), BF16),
        grid=(n_exp, d // tk),
        in_specs=[pl.BlockSpec((1, 1, tk, n), lambda e, k: (layer, e, k, 0)),
                  pl.BlockSpec((blk, blk), lambda e, k: (0, 0))],
        out_specs=pl.BlockSpec((1, n // cw, tk, cw), lambda e, k: (e, 0, k, 0)),
        compiler_params=_cparams(("arbitrary", "arbitrary")),
        name="moe_weight_prep",
    )(w_gate_up, sel)


def _cast_kernel(w_ref, o_ref):
    o_ref[0] = w_ref[0, 0].astype(o_ref.dtype)


def _moe_cast(w_down, layer):
    _, n_exp, f, d = w_down.shape
    tk = _pick(f, 512, 8)
    return pl.pallas_call(
        _cast_kernel,
        out_shape=jax.ShapeDtypeStruct((n_exp, f, d), BF16),
        grid=(n_exp, f // tk),
        in_specs=[pl.BlockSpec((1, 1, tk, d), lambda e, k: (layer, e, k, 0))],
        out_specs=pl.BlockSpec((1, tk, d), lambda e, k: (e, k, 0)),
        compiler_params=_cparams(("arbitrary", "arbitrary")),
        name="moe_weight_cast",
    )(w_down)


def _split_blocks(v):
    lead = v.shape[:-1]
    return v.reshape(lead + (-1, LANES, 2)).swapaxes(-1, -2).reshape(lead + (-1,))


def _moe_kernel(te_ref, na_ref, x_ref, w_ref, b_ref, wd_ref, bd_ref, o_ref):
    i = pl.program_id(0)
    n_act = na_ref[0]

    @pl.when(i < n_act)
    def _():
        lo, hi = _unpack_halves(x_ref[...])
        xb = jnp.concatenate([lo.astype(BF16), hi.astype(BF16)], axis=1)
        acts = []
        for c in range(w_ref.shape[1]):
            h = jnp.dot(xb, w_ref[0, c], preferred_element_type=F32) + b_ref[0, c]
            for q in range(h.shape[1] // (2 * LANES)):
                gate = jnp.minimum(h[:, 2 * q * LANES:(2 * q + 1) * LANES], SWIGLU_LIMIT)
                up = jnp.clip(h[:, (2 * q + 1) * LANES:(2 * q + 2) * LANES], -SWIGLU_LIMIT, SWIGLU_LIMIT)
                acts.append(((up + 1.0) * (gate * jax.nn.sigmoid(SWIGLU_ALPHA * gate))).astype(BF16))
        a = acts[0] if len(acts) == 1 else jnp.concatenate(acts, axis=1)
        y = jnp.dot(a, wd_ref[0], preferred_element_type=F32) + bd_ref[0]
        o_ref[:, 0, :] = _pack_halves(y)

    @pl.when(i >= n_act)
    def _():
        o_ref[...] = jnp.zeros(o_ref.shape, o_ref.dtype)


def _moe_experts(x_sorted, tile_e, n_active, wp, bp, wd, bd, tm):
    r_max, half = x_sorted.shape
    n_exp, n_chunks, d, cw = wp.shape
    f = wd.shape[1]
    n_tiles = r_max // tm
    last = lambda i, te, na: (jnp.minimum(i, jnp.maximum(na[0] - 1, 0)), 0)
    exp3 = lambda i, te, na: (te[i], 0, 0)
    exp4 = lambda i, te, na: (te[i], 0, 0, 0)
    return pl.pallas_call(
        _moe_kernel,
        out_shape=jax.ShapeDtypeStruct((r_max, 1, half), jnp.uint32),
        grid_spec=pltpu.PrefetchScalarGridSpec(
            num_scalar_prefetch=2,
            grid=(n_tiles,),
            in_specs=[pl.BlockSpec((tm, half), last),
                      pl.BlockSpec((1, n_chunks, d, cw), exp4), pl.BlockSpec((1, n_chunks, 1, cw), exp4),
                      pl.BlockSpec((1, f, d), exp3), pl.BlockSpec((1, 1, d), exp3)],
            out_specs=pl.BlockSpec((tm, 1, half), lambda i, te, na: (i, 0, 0))),
        compiler_params=_cparams(("arbitrary",)),
        name="moe_experts",
    )(tile_e, n_active, x_sorted, wp, bp, wd, bd)


def _combine_kernel(pos_ref, posn_ref, gw_ref, x_ref, g_ref, y_hbm, o_ref, buf, sem, *, rows):
    i = pl.program_id(0)
    n = pl.num_programs(0)

    slot = i % 2

    def row_copy(src, k, r, s):
        return pltpu.make_async_copy(y_hbm.at[src], buf.at[s, k, pl.ds(r, 1)], sem.at[s])

    @pl.when(i == 0)
    def _():
        def body(r, c):
            for k in range(TOP_K):
                row_copy(pos_ref[0, 0, r * TOP_K + k], k, r, 0).start()
            return c
        lax.fori_loop(0, rows, body, 0, unroll=2)

    pltpu.make_async_copy(buf.at[1 - slot], buf.at[slot], sem.at[slot]).wait()

    def combine():
        gw = gw_ref[...]
        half = buf.shape[-1]
        acc_lo = acc_hi = None
        for k in range(TOP_K):
            lo, hi = _unpack_halves(buf[slot, k])
            w = gw[:, k:k + 1]
            acc_lo = w * lo if k == 0 else acc_lo + w * lo
            acc_hi = w * hi if k == 0 else acc_hi + w * hi
        g = g_ref[0]
        o_ref[:, :half] = x_ref[:, :half] + g[:, :half] * acc_lo
        o_ref[:, half:] = x_ref[:, half:] + g[:, half:] * acc_hi

    @pl.when(i + 1 < n)
    def _():
        for r in range(rows):
            for k in range(TOP_K):
                row_copy(posn_ref[0, 0, r * TOP_K + k], k, r, 1 - slot).start()
        combine()

    @pl.when(i + 1 == n)
    def _():
        combine()


def _moe_combine(y_sorted, pos, gw2d, x2d, modt, tm_row, ctx_tiles, tiles_per_batch):
    t, d = x2d.shape
    rows = _pick(tm_row, COMBINE_ROWS, 8)
    n_steps = t // rows
    sub = tm_row // rows
    pos3 = pos.reshape(n_steps, 1, rows * TOP_K)

    def gmap(i):
        tile = i // sub
        bb = tile // tiles_per_batch
        seg = jnp.where(tile % tiles_per_batch >= ctx_tiles, 1, 0)
        return (bb * 2 + seg, 0, 5)

    return pl.pallas_call(
        functools.partial(_combine_kernel, rows=rows),
        out_shape=jax.ShapeDtypeStruct((t, d), F32),
        grid=(n_steps,),
        in_specs=[pl.BlockSpec((1, 1, rows * TOP_K), lambda i: (i, 0, 0), memory_space=pltpu.SMEM),
                  pl.BlockSpec((1, 1, rows * TOP_K), lambda i: (jnp.minimum(i + 1, n_steps - 1), 0, 0),
                               memory_space=pltpu.SMEM),
                  pl.BlockSpec((rows, LANES), lambda i: (i, 0)),
                  pl.BlockSpec((rows, d), lambda i: (i, 0)),
                  pl.BlockSpec((1, 1, d), gmap),
                  pl.BlockSpec(memory_space=pl.ANY)],
        out_specs=pl.BlockSpec((rows, d), lambda i: (i, 0)),
        scratch_shapes=[pltpu.VMEM((2, TOP_K, rows, d // 2), jnp.uint32), pltpu.SemaphoreType.DMA((2,))],
        compiler_params=_cparams(("arbitrary",)),
        name="moe_combine",
    )(pos3, pos3, gw2d, x2d, modt, y_sorted)


def _rope_tables(seq, ctx_len):
    n_rows = seq // GRID_W
    axis_rot = HEAD_DIM // 2
    rows = jnp.repeat(jnp.arange(n_rows, dtype=F32), GRID_W)
    cols = jnp.tile(jnp.arange(GRID_W, dtype=F32), n_rows)
    inv_freq = ROPE_BASE ** (-jnp.arange(0, axis_rot, 2, dtype=F32) / axis_rot)
    ang_r = rows[:, None] * inv_freq
    ang_c = cols[:, None] * inv_freq
    ang = jnp.concatenate([ang_r, ang_r, ang_c, ang_c], axis=-1)
    sign = jnp.tile(jnp.concatenate([-jnp.ones(axis_rot // 2, F32), jnp.ones(axis_rot // 2, F32)]), 2)
    cos = jnp.concatenate([jnp.ones((ctx_len, HEAD_DIM), F32), jnp.cos(ang)], axis=0)
    sin = jnp.concatenate([jnp.zeros((ctx_len, HEAD_DIM), F32), jnp.sin(ang) * sign], axis=0)
    return cos, sin


def kernel(x, c, ctx, c_ctx, w_mod, b_mod, norm_mix, norm_ffn, w_router, b_router, w_gate_up, b_gate_up, w_down, b_down, attn_w_qkv, attn_w_o, attn_q_gain, attn_k_gain, attn_sinks, ssm_a_re, ssm_a_im, ssm_log_dt, ssm_b_re, ssm_b_im, ssm_c_re, ssm_c_im, ssm_d, ssm_w_glu, ssm_b_glu):
    b, seq, d = x.shape
    ctx_len = ctx.shape[1]
    depth = w_mod.shape[0]
    p = ctx_len + seq
    t = b * p
    n_exp = w_router.shape[-1]
    n_q = d // HEAD_DIM
    n_kv = (attn_w_qkv.shape[-1] // HEAD_DIM - n_q) // 2
    assert ctx_len % ATT_BLOCK == 0 and seq % ATT_BLOCK == 0 and seq % GRID_W == 0
    assert d % (2 * LANES) == 0 and ctx_len % SSM_CHUNK == 0 and seq % SSM_CHUNK == 0

    tm = _pick(math.gcd(p, ctx_len), ROW_TILE)
    mm_tm = _pick(math.gcd(p, ctx_len), MM_TILE_M)
    tiles_pb = p // mm_tm
    ctx_tiles = ctx_len // mm_tm

    pad = (-(b + 1)) % 8
    cvec = jnp.concatenate([c, c_ctx[None, :], jnp.zeros((pad, d), F32)], axis=0)
    mod_all = _mod_all(cvec, w_mod, b_mod)

    cos_t, sin_t = _rope_tables(seq, ctx_len)
    xs = jnp.concatenate([ctx, x], axis=1)

    tm_moe = min(MOE_TILE, _pick(t * TOP_K, MOE_TILE, 8))
    n_tiles = (t * TOP_K) // tm_moe + n_exp
    scale = HEAD_DIM ** -0.5

    for i in range(depth):
        j = i // 2
        lat = mod_all[i, :b]
        cx = jnp.broadcast_to(mod_all[i, b][None], lat.shape)
        modt = jnp.stack([cx, lat], axis=1).reshape(b * 2, 1, 6 * d)

        x2d = xs.reshape(t, d)
        if i % 2 == 0:
            gain = jnp.concatenate([jnp.tile(attn_q_gain[j] * scale, n_q), jnp.tile(attn_k_gain[j], n_kv),
                                    jnp.ones((n_kv * HEAD_DIM,), F32)]).reshape(1, -1)
            qkv = _qkv_proj(x2d, norm_mix[i], modt, attn_w_qkv[j].astype(BF16), gain, cos_t, sin_t,
                            n_q + n_kv, mm_tm, ctx_tiles, tiles_pb)
            o = _attention(qkv.reshape(b, p, -1), attn_sinks[j].astype(F32), n_q, n_kv, ctx_len)
            x2d = _proj_residual(o.reshape(t, d), attn_w_o[j].astype(BF16), x2d, modt, 2,
                                 mm_tm, ctx_tiles, tiles_pb)
        else:
            u = _norm_mod(xs, norm_mix[i], modt, 0, ctx_len)
            tables = _s5_tables(ssm_a_re[j], ssm_a_im[j], ssm_log_dt[j], ssm_b_re[j], ssm_b_im[j],
                                ssm_c_re[j], ssm_c_im[j], ssm_d[j])
            gy = _s5_mix(u, tables, ctx_len)
            x2d = _glu_residual(gy.reshape(t, d), ssm_w_glu[j].astype(BF16), ssm_b_glu[j], x2d, modt, 2,
                                mm_tm, ctx_tiles, tiles_pb)

        hp, idx, gw, rank, counts = _norm_router(x2d.reshape(b, p, d), norm_ffn[i], modt, w_router[i],
                                                 b_router[i], ctx_len)
        tile_e, n_active, pos = _moe_schedule(idx.reshape(t, LANES), rank.reshape(t, LANES),
                                              counts.reshape(n_exp).astype(jnp.int32), tm_moe, n_tiles)
        x_sorted = _moe_scatter(hp.reshape(t, d // 2), pos, n_tiles * tm_moe, tm)
        wp = _moe_prep(w_gate_up, i)
        bp = _split_blocks(b_gate_up[i]).reshape(n_exp, wp.shape[1], 1, wp.shape[3])
        y_sorted = _moe_experts(x_sorted, tile_e, n_active, wp, bp,
                                _moe_cast(w_down, i), b_down[i].reshape(n_exp, 1, d), tm_moe)
        x2d = _moe_combine(y_sorted, pos, gw.reshape(t, LANES), x2d, modt, tm, ctx_len // tm, p // tm)
        xs = x2d.reshape(b, p, d)

    return xs[:, ctx_len:]
```

```python
import functools
import math

import jax
import jax.numpy as jnp
from jax import lax
from jax.experimental import pallas as pl
from jax.experimental.pallas import tpu as pltpu

F32 = jnp.float32
BF16 = jnp.bfloat16

HEAD_DIM = 128
ATT_BLOCK = 128
GRID_W = 64
ROPE_BASE = 10000.0
SSM_GROUP = 16
SSM_STATE = 64
SSM_CHUNK = 16
TOP_K = 4
SWIGLU_LIMIT = 7.0
SWIGLU_ALPHA = 1.702
EPS = 1e-6
NEG_INF = -1e30
DT_FLOOR_RE = -1e-4

LANES = 128
VMEM_LIMIT = 56 * 1024 * 1024

ROW_TILE = 256
MM_TILE_M = 512
PROJ_TILE_N = 2048
GLU_TILE_N = 1024
MOE_TILE = 512
COMBINE_ROWS = 128
SSM_GROUP_BLOCK = 16


def _cparams(sem):
    return pltpu.CompilerParams(dimension_semantics=sem, vmem_limit_bytes=VMEM_LIMIT)


def _pick(n, pref, mult=8):
    t = min(n, pref)
    while n % t or t % mult:
        t -= 1
    return t


def _mod_kernel(c_ref, w_ref, b_ref, o_ref):
    c = c_ref[...]
    s = c * jax.nn.sigmoid(c)
    o_ref[0] = jnp.dot(s.astype(BF16), w_ref[0].astype(BF16), preferred_element_type=F32) + b_ref[0]


def _mod_all(cvec, w_mod, b_mod):
    depth, d, n = w_mod.shape
    tn = _pick(n, 1024, LANES)
    return pl.pallas_call(
        _mod_kernel,
        out_shape=jax.ShapeDtypeStruct((depth, cvec.shape[0], n), F32),
        grid=(depth, n // tn),
        in_specs=[pl.BlockSpec(cvec.shape, lambda l, j: (0, 0)),
                  pl.BlockSpec((1, d, tn), lambda l, j: (l, 0, j)),
                  pl.BlockSpec((1, 1, tn), lambda l, j: (l, 0, j))],
        out_specs=pl.BlockSpec((1, cvec.shape[0], tn), lambda l, j: (l, 0, j)),
        compiler_params=_cparams(("arbitrary", "arbitrary")),
        name="adaln_rows",
    )(cvec, w_mod, b_mod.reshape(depth, 1, n))


def _normed(x, w, sh, sc):
    r = lax.rsqrt(jnp.mean(x * x, axis=-1, keepdims=True) + EPS)
    return (x * r * w) * (1.0 + sc) + sh


def _norm_mod_kernel(x_ref, w_ref, sh_ref, sc_ref, o_ref):
    o_ref[0] = _normed(x_ref[0], w_ref[...], sh_ref[0], sc_ref[0]).astype(o_ref.dtype)


def _mod_spec(d, which, ctx_tiles):
    return pl.BlockSpec((1, 1, d), lambda b, i: (b * 2 + jnp.where(i >= ctx_tiles, 1, 0), 0, which))


def _norm_mod(x, w, modt, which_shift, ctx_len):
    b, p, d = x.shape
    tm = _pick(math.gcd(p, ctx_len), ROW_TILE)
    ct = ctx_len // tm
    return pl.pallas_call(
        _norm_mod_kernel,
        out_shape=jax.ShapeDtypeStruct((b, p, d), BF16),
        grid=(b, p // tm),
        in_specs=[pl.BlockSpec((1, tm, d), lambda bb, i: (bb, i, 0)),
                  pl.BlockSpec((1, d), lambda bb, i: (0, 0)),
                  _mod_spec(d, which_shift, ct), _mod_spec(d, which_shift + 1, ct)],
        out_specs=pl.BlockSpec((1, tm, d), lambda bb, i: (bb, i, 0)),
        compiler_params=_cparams(("arbitrary", "arbitrary")),
        name="norm_mod",
    )(x, w.reshape(1, d), modt, modt)


def _pack_halves(h):
    half = h.shape[-1] // 2
    bits = lax.bitcast_convert_type(h.astype(BF16).astype(F32), jnp.uint32)
    return (bits[:, :half] >> 16) | (bits[:, half:] & jnp.uint32(0xFFFF0000))


def _unpack_halves(xp):
    lo = lax.bitcast_convert_type(xp << 16, F32)
    hi = lax.bitcast_convert_type(xp & jnp.uint32(0xFFFF0000), F32)
    return lo, hi


def _norm_router_kernel(x_ref, w_ref, sh_ref, sc_ref, wr_ref, br_ref, hp_ref, idx_ref, gw_ref, rank_ref,
                        cnt_ref, base):
    @pl.when((pl.program_id(0) == 0) & (pl.program_id(1) == 0))
    def _():
        base[...] = jnp.zeros(base.shape, F32)

    h = _normed(x_ref[0], w_ref[...], sh_ref[0], sc_ref[0])
    hp_ref[0] = _pack_halves(h)
    wr = wr_ref[...]
    h_hi = h.astype(BF16)
    h_lo = (h - h_hi.astype(F32)).astype(BF16)
    w_hi = wr.astype(BF16)
    w_lo = (wr - w_hi.astype(F32)).astype(BF16)
    logits = (jnp.dot(h_hi, w_hi, preferred_element_type=F32)
              + (jnp.dot(h_lo, w_hi, preferred_element_type=F32)
                 + jnp.dot(h_hi, w_lo, preferred_element_type=F32))) + br_ref[...]
    n_exp = logits.shape[-1]
    lane = lax.broadcasted_iota(jnp.int32, logits.shape, 1)
    out_lane = lax.broadcasted_iota(jnp.int32, idx_ref.shape[1:], 1)
    vals, idxs = [], []
    rest = logits
    for _ in range(TOP_K):
        m = jnp.max(rest, axis=-1, keepdims=True)
        idx = jnp.min(jnp.where(rest == m, lane, n_exp), axis=-1, keepdims=True)
        vals.append(m)
        idxs.append(idx)
        rest = jnp.where(lane == idx, -jnp.inf, rest)
    exps = [jnp.exp(v - vals[0]) for v in vals]
    tot = exps[0]
    for e in exps[1:]:
        tot = tot + e
    idx_out = jnp.zeros(idx_ref.shape[1:], jnp.int32)
    gw_out = jnp.zeros(gw_ref.shape[1:], F32)
    for k in range(TOP_K):
        idx_out = jnp.where(out_lane == k, idxs[k], idx_out)
        gw_out = jnp.where(out_lane == k, exps[k] / tot, gw_out)
    idx_ref[0] = idx_out
    gw_ref[0] = gw_out

    sel = jnp.zeros(logits.shape, F32)
    for k in range(TOP_K):
        sel = sel + (lane == idxs[k]).astype(F32)
    tm = sel.shape[0]
    earlier = (lax.broadcasted_iota(jnp.int32, (tm, tm), 1)
               < lax.broadcasted_iota(jnp.int32, (tm, tm), 0)).astype(BF16)
    before = jnp.dot(earlier, sel.astype(BF16), preferred_element_type=F32) + base[...]
    rank_out = jnp.zeros(rank_ref.shape[1:], jnp.int32)
    for k in range(TOP_K):
        rk = jnp.sum(jnp.where(lane == idxs[k], before, 0.0), axis=-1, keepdims=True)
        rank_out = jnp.where(out_lane == k, rk.astype(jnp.int32), rank_out)
    rank_ref[0] = rank_out
    base[...] = base[...] + jnp.sum(sel, axis=0, keepdims=True)
    cnt_ref[...] = base[...]


def _norm_router(x, w, modt, w_router, b_router, ctx_len):
    b, p, d = x.shape
    n_exp = w_router.shape[-1]
    tm = _pick(math.gcd(p, ctx_len), ROW_TILE)
    ct = ctx_len // tm
    row = lambda bb, i: (bb, i, 0)
    return pl.pallas_call(
        _norm_router_kernel,
        out_shape=(jax.ShapeDtypeStruct((b, p, d // 2), jnp.uint32),
                   jax.ShapeDtypeStruct((b, p, LANES), jnp.int32),
                   jax.ShapeDtypeStruct((b, p, LANES), F32),
                   jax.ShapeDtypeStruct((b, p, LANES), jnp.int32),
                   jax.ShapeDtypeStruct((1, n_exp), F32)),
        grid=(b, p // tm),
        in_specs=[pl.BlockSpec((1, tm, d), row),
                  pl.BlockSpec((1, d), lambda bb, i: (0, 0)),
                  _mod_spec(d, 3, ct), _mod_spec(d, 4, ct),
                  pl.BlockSpec((d, n_exp), lambda bb, i: (0, 0)),
                  pl.BlockSpec((1, n_exp), lambda bb, i: (0, 0))],
        out_specs=(pl.BlockSpec((1, tm, d // 2), row),
                   pl.BlockSpec((1, tm, LANES), row),
                   pl.BlockSpec((1, tm, LANES), row),
                   pl.BlockSpec((1, tm, LANES), row),
                   pl.BlockSpec((1, n_exp), lambda bb, i: (0, 0))),
        scratch_shapes=[pltpu.VMEM((1, n_exp), F32)],
        compiler_params=_cparams(("arbitrary", "arbitrary")),
        name="norm_router",
    )(x, w.reshape(1, d), modt, modt, w_router, b_router.reshape(1, n_exp))


def _qkv_kernel(x_ref, nw_ref, sh_ref, sc_ref, w_ref, g_ref, cos_ref, sin_ref, o_ref, *, n_norm_heads):
    u = _normed(x_ref[...], nw_ref[...], sh_ref[0], sc_ref[0]).astype(BF16)
    acc = jnp.dot(u, w_ref[...], preferred_element_type=F32)
    cos = cos_ref[...]
    sin = sin_ref[...]
    lane = lax.broadcasted_iota(jnp.int32, cos.shape, 1)
    first = (lane // (HEAD_DIM // 4)) % 2 == 0
    for hh in range(acc.shape[1] // HEAD_DIM):
        sl = slice(hh * HEAD_DIM, (hh + 1) * HEAD_DIM)
        xh = acc[:, sl]
        if hh < n_norm_heads:
            r = lax.rsqrt(jnp.mean(xh * xh, axis=-1, keepdims=True) + EPS)
            y = xh * r * g_ref[:, sl]
            partner = jnp.where(first, pltpu.roll(y, HEAD_DIM - HEAD_DIM // 4, 1),
                                pltpu.roll(y, HEAD_DIM // 4, 1))
            xh = y * cos + partner * sin
        o_ref[:, sl] = xh.astype(o_ref.dtype)


def _row_mod_spec(d, which, ctx_tiles, tiles_per_batch):
    def imap(i):
        seg = jnp.where(i % tiles_per_batch >= ctx_tiles, 1, 0)
        return ((i // tiles_per_batch) * 2 + seg, 0, which)

    return pl.BlockSpec((1, 1, d), imap)


def _qkv_proj(x2d, norm_w, modt, w_bf16, gain_row, cos_t, sin_t, n_norm_heads, tm, ctx_tiles, tiles_per_batch):
    m, k = x2d.shape
    n = w_bf16.shape[1]
    pos = lambda i: (i % tiles_per_batch, 0)
    return pl.pallas_call(
        functools.partial(_qkv_kernel, n_norm_heads=n_norm_heads),
        out_shape=jax.ShapeDtypeStruct((m, n), BF16),
        grid=(m // tm,),
        in_specs=[pl.BlockSpec((tm, k), lambda i: (i, 0)),
                  pl.BlockSpec((1, k), lambda i: (0, 0)),
                  _row_mod_spec(k, 0, ctx_tiles, tiles_per_batch),
                  _row_mod_spec(k, 1, ctx_tiles, tiles_per_batch),
                  pl.BlockSpec((k, n), lambda i: (0, 0)),
                  pl.BlockSpec((1, n), lambda i: (0, 0)),
                  pl.BlockSpec((tm, HEAD_DIM), pos),
                  pl.BlockSpec((tm, HEAD_DIM), pos)],
        out_specs=pl.BlockSpec((tm, n), lambda i: (i, 0)),
        compiler_params=_cparams(("arbitrary",)),
        name="qkv_proj",
    )(x2d, norm_w.reshape(1, k), modt, modt, w_bf16, gain_row, cos_t, sin_t)


def _attn_kernel(sink_ref, q_ref, kc_ref, vc_ref, kp_ref, ks_ref, kn_ref, vp_ref, vs_ref, vn_ref,
                 o_ref, *, q_per_kv, heads, ctx_blocks, n_blocks):
    hg = pl.program_id(1)
    i = pl.program_id(2)
    blk = ATT_BLOCK
    rows = q_per_kv * blk
    qw = q_per_kv * HEAD_DIM
    dn = (((1,), (1,)), ((), ()))

    qi = lax.broadcasted_iota(jnp.int32, (rows, blk), 0) % blk
    kj = lax.broadcasted_iota(jnp.int32, (rows, blk), 1)
    is_lat = i >= ctx_blocks
    ok_prev = (kj >= qi) & (i > ctx_blocks)
    ok_self = jnp.broadcast_to(is_lat, (rows, blk))
    ok_next = (kj <= qi) & is_lat & (i < n_blocks - 1)
    valid = jnp.concatenate([ok_prev, ok_self, ok_next], axis=1)
    row_head = lax.broadcasted_iota(jnp.int32, (rows, 1), 0) // blk

    for hh in range(heads):
        hd = slice(hh * HEAD_DIM, (hh + 1) * HEAD_DIM)
        q = q_ref[0, :, hh * qw:(hh + 1) * qw]
        qs = jnp.concatenate([q[:, g * HEAD_DIM:(g + 1) * HEAD_DIM] for g in range(q_per_kv)], axis=0)
        k_loc = jnp.concatenate([kp_ref[0, :, hd], ks_ref[0, :, hd], kn_ref[0, :, hd]], axis=0)
        v_loc = jnp.concatenate([vp_ref[0, :, hd], vs_ref[0, :, hd], vn_ref[0, :, hd]], axis=0)
        s_loc = jnp.where(valid, lax.dot_general(qs, k_loc, dn, preferred_element_type=F32), NEG_INF)
        s_ctx = lax.dot_general(qs, kc_ref[0, :, hd], dn, preferred_element_type=F32)

        sink = jnp.zeros((rows, 1), F32)
        for g in range(q_per_kv):
            sink = jnp.where(row_head == g, sink_ref[(hg * heads + hh) * q_per_kv + g], sink)

        m = jnp.maximum(jnp.maximum(jnp.max(s_loc, axis=-1, keepdims=True),
                                    jnp.max(s_ctx, axis=-1, keepdims=True)), sink)
        p_loc = jnp.exp(s_loc - m)
        p_ctx = jnp.exp(s_ctx - m)
        denom = (jnp.sum(p_loc, axis=-1, keepdims=True) + jnp.sum(p_ctx, axis=-1, keepdims=True)
                 + jnp.exp(sink - m))
        o = (jnp.dot(p_loc.astype(BF16), v_loc, preferred_element_type=F32)
             + jnp.dot(p_ctx.astype(BF16), vc_ref[0, :, hd], preferred_element_type=F32)) / denom
        o = o.astype(o_ref.dtype)
        for g in range(q_per_kv):
            o_ref[0, :, hh * qw + g * HEAD_DIM:hh * qw + (g + 1) * HEAD_DIM] = o[g * blk:(g + 1) * blk]


def _attention(qkv, sinks, n_q, n_kv, ctx_len):
    b, p, _ = qkv.shape
    blk = ATT_BLOCK
    nb = p // blk
    cb = ctx_len // blk
    qpk = n_q // n_kv
    heads = next(h for h in (4, 2, 1) if n_kv % h == 0 and n_q % h == 0)
    qw = heads * qpk * HEAD_DIM
    hw = heads * HEAD_DIM
    kcol = n_q // heads
    vcol = (n_q + n_kv) // heads

    def loc(col0, off):
        return pl.BlockSpec(
            (1, blk, hw),
            lambda bb, h, i, s: (bb, jnp.clip(i + off, cb, nb - 1), col0 + h))

    return pl.pallas_call(
        functools.partial(_attn_kernel, q_per_kv=qpk, heads=heads, ctx_blocks=cb, n_blocks=nb),
        out_shape=jax.ShapeDtypeStruct((b, p, n_q * HEAD_DIM), BF16),
        grid_spec=pltpu.PrefetchScalarGridSpec(
            num_scalar_prefetch=1,
            grid=(b, n_kv // heads, nb),
            in_specs=[pl.BlockSpec((1, blk, qw), lambda bb, h, i, s: (bb, i, h)),
                      pl.BlockSpec((1, ctx_len, hw), lambda bb, h, i, s: (bb, 0, kcol + h)),
                      pl.BlockSpec((1, ctx_len, hw), lambda bb, h, i, s: (bb, 0, vcol + h)),
                      loc(kcol, -1), loc(kcol, 0), loc(kcol, 1),
                      loc(vcol, -1), loc(vcol, 0), loc(vcol, 1)],
            out_specs=pl.BlockSpec((1, blk, qw), lambda bb, h, i, s: (bb, i, h))),
        compiler_params=_cparams(("arbitrary", "arbitrary", "arbitrary")),
        name="window_attn",
    )(sinks, qkv, qkv, qkv, qkv, qkv, qkv, qkv, qkv, qkv)


def _gate_spec(d, which, ctx_tiles, tiles_per_batch, tn):
    per = d // tn

    def imap(j, i):
        bb = i // tiles_per_batch
        seg = jnp.where(i % tiles_per_batch >= ctx_tiles, 1, 0)
        return (bb * 2 + seg, 0, which * per + j)

    return pl.BlockSpec((1, 1, tn), imap)


def _proj_res_kernel(a_ref, w_ref, x_ref, g_ref, o_ref):
    y = jnp.dot(a_ref[...], w_ref[...], preferred_element_type=F32)
    o_ref[...] = x_ref[...] + g_ref[0] * y


def _proj_residual(a2d, w_bf16, x2d, modt, which_gate, tm, ctx_tiles, tiles_per_batch):
    m, k = a2d.shape
    n = w_bf16.shape[1]
    tn = _pick(n, PROJ_TILE_N, LANES)
    return pl.pallas_call(
        _proj_res_kernel,
        out_shape=jax.ShapeDtypeStruct((m, n), F32),
        grid=(n // tn, m // tm),
        in_specs=[pl.BlockSpec((tm, k), lambda j, i: (i, 0)),
                  pl.BlockSpec((k, tn), lambda j, i: (0, j)),
                  pl.BlockSpec((tm, tn), lambda j, i: (i, j)),
                  _gate_spec(n, which_gate, ctx_tiles, tiles_per_batch, tn)],
        out_specs=pl.BlockSpec((tm, tn), lambda j, i: (i, j)),
        compiler_params=_cparams(("arbitrary", "arbitrary")),
        name="proj_residual",
    )(a2d, w_bf16, x2d, modt)


def _glu_res_kernel(a_ref, w1_ref, w2_ref, b1_ref, b2_ref, x_ref, g_ref, o_ref):
    a = a_ref[...]
    z1 = jnp.dot(a, w1_ref[...], preferred_element_type=F32) + b1_ref[...]
    z2 = jnp.dot(a, w2_ref[...], preferred_element_type=F32) + b2_ref[...]
    o_ref[...] = x_ref[...] + g_ref[0] * (z1 * jax.nn.sigmoid(z2))


def _glu_residual(a2d, w_bf16, b_glu, x2d, modt, which_gate, tm, ctx_tiles, tiles_per_batch):
    m, k = a2d.shape
    d = w_bf16.shape[1] // 2
    tn = _pick(d, GLU_TILE_N, LANES)
    nj = d // tn
    b2 = b_glu.reshape(1, 2 * d)
    return pl.pallas_call(
        _glu_res_kernel,
        out_shape=jax.ShapeDtypeStruct((m, d), F32),
        grid=(nj, m // tm),
        in_specs=[pl.BlockSpec((tm, k), lambda j, i: (i, 0)),
                  pl.BlockSpec((k, tn), lambda j, i: (0, j)),
                  pl.BlockSpec((k, tn), lambda j, i: (0, j + nj)),
                  pl.BlockSpec((1, tn), lambda j, i: (0, j)),
                  pl.BlockSpec((1, tn), lambda j, i: (0, j + nj)),
                  pl.BlockSpec((tm, tn), lambda j, i: (i, j)),
                  _gate_spec(d, which_gate, ctx_tiles, tiles_per_batch, tn)],
        out_specs=pl.BlockSpec((tm, tn), lambda j, i: (i, j)),
        compiler_params=_cparams(("arbitrary", "arbitrary")),
        name="glu_residual",
    )(a2d, w_bf16, w_bf16, b2, b2, x2d, modt)


def _s5_tables(a_re, a_im, log_dt, b_re, b_im, c_re, c_im, d_skip):
    q, n_state, pch = SSM_CHUNK, a_re.shape[-1], b_re.shape[-1]
    g = a_re.shape[1]
    hp = lax.Precision.HIGHEST
    lam = lax.complex(jnp.minimum(a_re.astype(F32), DT_FLOOR_RE), a_im.astype(F32))
    lam_dt = lam * jnp.exp(log_dt.astype(F32))[..., None]
    b_bar = ((jnp.exp(lam_dt) - 1.0) / lam)[..., None] * lax.complex(b_re.astype(F32), b_im.astype(F32))
    c_mat = lax.complex(c_re.astype(F32), c_im.astype(F32))
    tau = jnp.arange(q + 1, dtype=F32)
    pw = jnp.exp(lam_dt[:, :, None, :] * tau[None, None, :, None])
    kern = jnp.einsum('dgpn,dgtn,dgnr->dgtpr', c_mat, pw[:, :, :q], b_bar, precision=hp).real
    t_i = jnp.arange(q)[:, None]
    s_i = jnp.arange(q)[None, :]
    kf = jnp.where((t_i >= s_i)[None, :, :, None, None], kern[0][:, jnp.clip(t_i - s_i, 0, q - 1)], 0.0)
    kb = jnp.where((s_i >= t_i)[None, :, :, None, None], kern[1][:, jnp.clip(s_i - t_i, 0, q - 1)], 0.0)
    dsk = d_skip.astype(F32).reshape(g, pch)
    diag = (jnp.eye(q)[None, :, :, None, None] * jnp.eye(pch)[None, None, None] * dsk[:, None, None, :, None])
    mt = (kf + kb + diag).transpose(0, 2, 4, 1, 3).reshape(g, q * pch, q * pch)
    wsf = pw[0][:, ::-1][:, 1:, :, None] * b_bar[0][:, None]
    wsb = pw[1][:, :q, :, None] * b_bar[1][:, None]
    wsf = wsf.transpose(0, 1, 3, 2).reshape(g, q * pch, n_state)
    wsb = wsb.transpose(0, 1, 3, 2).reshape(g, q * pch, n_state)
    w1 = jnp.concatenate([mt, wsf.real, wsb.real, wsf.imag, wsb.imag], axis=-1)
    cf = c_mat[0][:, None] * pw[0][:, 1:, None, :]
    cb = c_mat[1][:, None] * pw[1][:, ::-1][:, :q, None, :]
    cf = cf.transpose(0, 3, 1, 2).reshape(g, n_state, q * pch)
    cb = cb.transpose(0, 3, 1, 2).reshape(g, n_state, q * pch)
    w3 = jnp.concatenate([cf.real, cb.real, -cf.imag, -cb.imag], axis=1)
    aq = pw[:, :, q]
    a_r = jnp.concatenate([aq[0].real, aq[1].real], axis=-1)
    a_i = jnp.concatenate([aq[0].imag, aq[1].imag], axis=-1)
    return w1.astype(BF16), w3.astype(BF16), a_r, a_i


def _s5_in_kernel(u_ref, w_ref, y_ref, sr_ref, si_ref):
    width = u_ref.shape[-1]
    ns2 = sr_ref.shape[-1]
    for g in range(u_ref.shape[0]):
        r = jnp.dot(u_ref[g], w_ref[g], preferred_element_type=F32)
        y_ref[g] = r[:, :width]
        sr_ref[0, g] = r[:, width:width + ns2]
        si_ref[0, g] = r[:, width + ns2:]


def _s5_scan_kernel(sr_ref, si_ref, ar_ref, ai_ref, hr_ref, hi_ref, fr, fi, br, bi, *, n_chunks, ctx_chunks):
    gb = ar_ref.shape[0]
    a_r = ar_ref[...]
    a_i = ai_ref[...]
    lane = lax.broadcasted_iota(jnp.int32, a_r.shape, 1)
    fwd = lane < (a_r.shape[1] // 2)

    def rows(ref, c):
        return ref[pl.ds(c, gb, stride=n_chunks), :]

    def body(i, carry):
        h_r, h_i = carry
        cf = i
        cb = jnp.where(i < ctx_chunks, ctx_chunks - 1 - i, n_chunks + ctx_chunks - 1 - i)
        fr[pl.ds(cf, gb, stride=n_chunks), :] = h_r
        fi[pl.ds(cf, gb, stride=n_chunks), :] = h_i
        br[pl.ds(cb, gb, stride=n_chunks), :] = h_r
        bi[pl.ds(cb, gb, stride=n_chunks), :] = h_i
        s_r = jnp.where(fwd, rows(sr_ref, cf), rows(sr_ref, cb))
        s_i = jnp.where(fwd, rows(si_ref, cf), rows(si_ref, cb))
        return (a_r * h_r - a_i * h_i + s_r, a_r * h_i + a_i * h_r + s_i)

    zero = jnp.zeros(a_r.shape, F32)
    lax.fori_loop(0, n_chunks, body, (zero, zero))
    full = lax.broadcasted_iota(jnp.int32, hr_ref.shape, 1) < (a_r.shape[1] // 2)
    hr_ref[...] = jnp.where(full, fr[...], br[...])
    hi_ref[...] = jnp.where(full, fi[...], bi[...])


def _s5_out_kernel(y_ref, hr_ref, hi_ref, w_ref, o_ref):
    ns2 = hr_ref.shape[-1]
    for g in range(y_ref.shape[0]):
        y = (y_ref[g]
             + jnp.dot(hr_ref[0, g].astype(BF16), w_ref[g, :ns2], preferred_element_type=F32)
             + jnp.dot(hi_ref[0, g].astype(BF16), w_ref[g, ns2:], preferred_element_type=F32))
        o_ref[g] = jax.nn.gelu(y).astype(o_ref.dtype)


def _s5_mix(u, tables, ctx_len):
    w1, w3, a_r, a_i = tables
    b, p, d = u.shape
    q, pch = SSM_CHUNK, SSM_GROUP
    g = d // pch
    nc = p // q
    width = q * pch
    ns2 = 2 * SSM_STATE
    gb = _pick(g, SSM_GROUP_BLOCK, 8)
    ut = u.reshape(b, nc, q, g, pch).transpose(3, 0, 1, 2, 4).reshape(g, b * nc, width)

    y_in, s_r, s_i = pl.pallas_call(
        _s5_in_kernel,
        out_shape=(jax.ShapeDtypeStruct((g, b * nc, width), F32),
                   jax.ShapeDtypeStruct((b, g, nc, ns2), F32),
                   jax.ShapeDtypeStruct((b, g, nc, ns2), F32)),
        grid=(g // gb, b),
        in_specs=[pl.BlockSpec((gb, nc, width), lambda gi, bb: (gi, bb, 0)),
                  pl.BlockSpec((gb, width, width + 2 * ns2), lambda gi, bb: (gi, 0, 0))],
        out_specs=(pl.BlockSpec((gb, nc, width), lambda gi, bb: (gi, bb, 0)),
                   pl.BlockSpec((1, gb, nc, ns2), lambda gi, bb: (bb, gi, 0, 0)),
                   pl.BlockSpec((1, gb, nc, ns2), lambda gi, bb: (bb, gi, 0, 0))),
        compiler_params=_cparams(("arbitrary", "arbitrary")),
        name="s5_chunk_in",
    )(ut, w1)

    flat = lambda gi, bb: (bb * (g // gb) + gi, 0)
    h_r, h_i = pl.pallas_call(
        functools.partial(_s5_scan_kernel, n_chunks=nc, ctx_chunks=ctx_len // q),
        out_shape=(jax.ShapeDtypeStruct((b * g * nc, ns2), F32),) * 2,
        grid=(g // gb, b),
        in_specs=[pl.BlockSpec((gb * nc, ns2), flat), pl.BlockSpec((gb * nc, ns2), flat),
                  pl.BlockSpec((gb, ns2), lambda gi, bb: (gi, 0)),
                  pl.BlockSpec((gb, ns2), lambda gi, bb: (gi, 0))],
        out_specs=(pl.BlockSpec((gb * nc, ns2), flat),) * 2,
        scratch_shapes=[pltpu.VMEM((gb * nc, ns2), F32)] * 4,
        compiler_params=_cparams(("arbitrary", "arbitrary")),
        name="s5_chunk_scan",
    )(s_r.reshape(b * g * nc, ns2), s_i.reshape(b * g * nc, ns2), a_r, a_i)

    yt = pl.pallas_call(
        _s5_out_kernel,
        out_shape=jax.ShapeDtypeStruct((g, b * nc, width), BF16),
        grid=(g // gb, b),
        in_specs=[pl.BlockSpec((gb, nc, width), lambda gi, bb: (gi, bb, 0)),
                  pl.BlockSpec((1, gb, nc, ns2), lambda gi, bb: (bb, gi, 0, 0)),
                  pl.BlockSpec((1, gb, nc, ns2), lambda gi, bb: (bb, gi, 0, 0)),
                  pl.BlockSpec((gb, 2 * ns2, width), lambda gi, bb: (gi, 0, 0))],
        out_specs=pl.BlockSpec((gb, nc, width), lambda gi, bb: (gi, bb, 0)),
        compiler_params=_cparams(("arbitrary", "arbitrary")),
        name="s5_chunk_out",
    )(y_in, h_r.reshape(b, g, nc, ns2), h_i.reshape(b, g, nc, ns2), w3)
    return yt.reshape(g, b, nc, q, pch).transpose(1, 2, 3, 0, 4).reshape(b, p, d)


def _moe_schedule(idx, rank, counts, tm, n_tiles):
    n_exp = counts.shape[0]
    padded = (counts + tm - 1) // tm * tm
    gstart = jnp.cumsum(padded) - padded
    gend = gstart + padded
    n_active = (jnp.sum(padded) // tm).astype(jnp.int32)
    tile_start = jnp.arange(n_tiles, dtype=jnp.int32) * tm
    tile_e = jnp.sum((gend[None, :] <= tile_start[:, None]).astype(jnp.int32), axis=1)
    tile_e = jnp.minimum(tile_e, n_exp - 1)
    tile_e = jnp.where(jnp.arange(n_tiles) < n_active, tile_e, tile_e[jnp.maximum(n_active - 1, 0)])
    start = jnp.zeros(idx.shape, jnp.int32)
    for e in range(n_exp):
        start = jnp.where(idx == e, gstart[e], start)
    pos = (start + rank)[:, :TOP_K].astype(jnp.int32)
    return tile_e.astype(jnp.int32), n_active.reshape(1), pos


def _scatter_kernel(pos_ref, h_ref, zero_hbm, o_hbm, size_buf, sem, *, rows):
    del zero_hbm

    def body(r, c):
        for k in range(TOP_K):
            pltpu.make_async_copy(h_ref.at[pl.ds(r, 1)], o_hbm.at[pl.ds(pos_ref[0, 0, r * TOP_K + k], 1)],
                                  sem).start()
        return c
    lax.fori_loop(0, rows, body, 0, unroll=4)
    for _ in range(TOP_K):
        pltpu.make_async_copy(size_buf, size_buf, sem).wait()


def _moe_scatter(hp2d, pos, n_rows, rows):
    t, half = hp2d.shape
    n_steps = t // rows
    zeros = jnp.zeros((n_rows, half), hp2d.dtype)
    return pl.pallas_call(
        functools.partial(_scatter_kernel, rows=rows),
        out_shape=jax.ShapeDtypeStruct((n_rows, half), hp2d.dtype),
        grid=(n_steps,),
        in_specs=[pl.BlockSpec((1, 1, rows * TOP_K), lambda i: (i, 0, 0), memory_space=pltpu.SMEM),
                  pl.BlockSpec((rows, half), lambda i: (i, 0)),
                  pl.BlockSpec(memory_space=pl.ANY)],
        out_specs=pl.BlockSpec(memory_space=pl.ANY),
        scratch_shapes=[pltpu.VMEM((rows, half), hp2d.dtype), pltpu.SemaphoreType.DMA(())],
        input_output_aliases={2: 0},
        compiler_params=_cparams(("arbitrary",)),
        name="moe_scatter",
    )(pos.reshape(n_steps, 1, rows * TOP_K), hp2d, zeros)


def _moe_prep_kernel(w_ref, sel_ref, o_ref):
    sel = sel_ref[...]
    blk = sel.shape[0]
    per = o_ref.shape[3] // blk
    for c in range(w_ref.shape[3] // blk):
        w = w_ref[0, 0, :, c * blk:(c + 1) * blk].astype(BF16)
        o_ref[0, c // per, :, (c % per) * blk:(c % per + 1) * blk] = jnp.dot(
            w, sel, preferred_element_type=F32).astype(BF16)


def _moe_chunk_width(n):
    return 4 * LANES if n % (4 * LANES) == 0 else 2 * LANES


def _moe_prep(w_gate_up, layer):
    _, n_exp, d, n = w_gate_up.shape
    blk = 2 * LANES
    cw = _moe_chunk_width(n)
    tk = _pick(d, 512, 8)
    src = jnp.arange(blk)
    sel = (jnp.arange(blk)[None, :] == (src // 2 + (src % 2) * LANES)[:, None]).astype(BF16)
    return pl.pallas_call(
        _moe_prep_kernel,
        out_shape=jax.ShapeDtypeStruct((n_exp, n // cw, d, cw), BF16),
        grid=(n_exp, d // tk),
        in_specs=[pl.BlockSpec((1, 1, tk, n), lambda e, k: (layer, e, k, 0)),
                  pl.BlockSpec((blk, blk), lambda e, k: (0, 0))],
        out_specs=pl.BlockSpec((1, n // cw, tk, cw), lambda e, k: (e, 0, k, 0)),
        compiler_params=_cparams(("arbitrary", "arbitrary")),
        name="moe_weight_prep",
    )(w_gate_up, sel)


def _cast_kernel(w_ref, o_ref):
    o_ref[0] = w_ref[0, 0].astype(o_ref.dtype)


def _moe_cast(w_down, layer):
    _, n_exp, f, d = w_down.shape
    tk = _pick(f, 512, 8)
    return pl.pallas_call(
        _cast_kernel,
        out_shape=jax.ShapeDtypeStruct((n_exp, f, d), BF16),
        grid=(n_exp, f // tk),
        in_specs=[pl.BlockSpec((1, 1, tk, d), lambda e, k: (layer, e, k, 0))],
        out_specs=pl.BlockSpec((1, tk, d), lambda e, k: (e, k, 0)),
        compiler_params=_cparams(("arbitrary", "arbitrary")),
        name="moe_weight_cast",
    )(w_down)


def _split_blocks(v):
    lead = v.shape[:-1]
    return v.reshape(lead + (-1, LANES, 2)).swapaxes(-1, -2).reshape(lead + (-1,))


def _moe_kernel(te_ref, na_ref, x_ref, w_ref, b_ref, wd_ref, bd_ref, o_ref):
    i = pl.program_id(0)
    n_act = na_ref[0]

    @pl.when(i < n_act)
    def _():
        lo, hi = _unpack_halves(x_ref[...])
        xb = jnp.concatenate([lo.astype(BF16), hi.astype(BF16)], axis=1)
        acts = []
        for c in range(w_ref.shape[1]):
            h = jnp.dot(xb, w_ref[0, c], preferred_element_type=F32) + b_ref[0, c]
            for q in range(h.shape[1] // (2 * LANES)):
                gate = jnp.minimum(h[:, 2 * q * LANES:(2 * q + 1) * LANES], SWIGLU_LIMIT)
                up = jnp.clip(h[:, (2 * q + 1) * LANES:(2 * q + 2) * LANES], -SWIGLU_LIMIT, SWIGLU_LIMIT)
                acts.append(((up + 1.0) * (gate * jax.nn.sigmoid(SWIGLU_ALPHA * gate))).astype(BF16))
        a = acts[0] if len(acts) == 1 else jnp.concatenate(acts, axis=1)
        y = jnp.dot(a, wd_ref[0], preferred_element_type=F32) + bd_ref[0]
        o_ref[:, 0, :] = _pack_halves(y)

    @pl.when(i >= n_act)
    def _():
        o_ref[...] = jnp.zeros(o_ref.shape, o_ref.dtype)


def _moe_experts(x_sorted, tile_e, n_active, wp, bp, wd, bd, tm):
    r_max, half = x_sorted.shape
    n_exp, n_chunks, d, cw = wp.shape
    f = wd.shape[1]
    n_tiles = r_max // tm
    last = lambda i, te, na: (jnp.minimum(i, jnp.maximum(na[0] - 1, 0)), 0)
    exp3 = lambda i, te, na: (te[i], 0, 0)
    exp4 = lambda i, te, na: (te[i], 0, 0, 0)
    return pl.pallas_call(
        _moe_kernel,
        out_shape=jax.ShapeDtypeStruct((r_max, 1, half), jnp.uint32),
        grid_spec=pltpu.PrefetchScalarGridSpec(
            num_scalar_prefetch=2,
            grid=(n_tiles,),
            in_specs=[pl.BlockSpec((tm, half), last),
                      pl.BlockSpec((1, n_chunks, d, cw), exp4), pl.BlockSpec((1, n_chunks, 1, cw), exp4),
                      pl.BlockSpec((1, f, d), exp3), pl.BlockSpec((1, 1, d), exp3)],
            out_specs=pl.BlockSpec((tm, 1, half), lambda i, te, na: (i, 0, 0))),
        compiler_params=_cparams(("arbitrary",)),
        name="moe_experts",
    )(tile_e, n_active, x_sorted, wp, bp, wd, bd)


def _combine_kernel(pos_ref, posn_ref, gw_ref, x_ref, g_ref, y_hbm, o_ref, buf, sem, *, rows):
    i = pl.program_id(0)
    n = pl.num_programs(0)

    slot = i % 2

    def row_copy(src, k, r, s):
        return pltpu.make_async_copy(y_hbm.at[src], buf.at[s, k, pl.ds(r, 1)], sem.at[s])

    @pl.when(i == 0)
    def _():
        def body(r, c):
            for k in range(TOP_K):
                row_copy(pos_ref[0, 0, r * TOP_K + k], k, r, 0).start()
            return c
        lax.fori_loop(0, rows, body, 0, unroll=2)

    pltpu.make_async_copy(buf.at[1 - slot], buf.at[slot], sem.at[slot]).wait()

    def combine():
        gw = gw_ref[...]
        half = buf.shape[-1]
        acc_lo = acc_hi = None
        for k in range(TOP_K):
            lo, hi = _unpack_halves(buf[slot, k])
            w = gw[:, k:k + 1]
            acc_lo = w * lo if k == 0 else acc_lo + w * lo
            acc_hi = w * hi if k == 0 else acc_hi + w * hi
        g = g_ref[0]
        o_ref[:, :half] = x_ref[:, :half] + g[:, :half] * acc_lo
        o_ref[:, half:] = x_ref[:, half:] + g[:, half:] * acc_hi

    @pl.when(i + 1 < n)
    def _():
        for r in range(rows):
            for k in range(TOP_K):
                row_copy(posn_ref[0, 0, r * TOP_K + k], k, r, 1 - slot).start()
        combine()

    @pl.when(i + 1 == n)
    def _():
        combine()


def _moe_combine(y_sorted, pos, gw2d, x2d, modt, tm_row, ctx_tiles, tiles_per_batch):
    t, d = x2d.shape
    rows = _pick(tm_row, COMBINE_ROWS, 8)
    n_steps = t // rows
    sub = tm_row // rows
    pos3 = pos.reshape(n_steps, 1, rows * TOP_K)

    def gmap(i):
        tile = i // sub
        bb = tile // tiles_per_batch
        seg = jnp.where(tile % tiles_per_batch >= ctx_tiles, 1, 0)
        return (bb * 2 + seg, 0, 5)

    return pl.pallas_call(
        functools.partial(_combine_kernel, rows=rows),
        out_shape=jax.ShapeDtypeStruct((t, d), F32),
        grid=(n_steps,),
        in_specs=[pl.BlockSpec((1, 1, rows * TOP_K), lambda i: (i, 0, 0), memory_space=pltpu.SMEM),
                  pl.BlockSpec((1, 1, rows * TOP_K), lambda i: (jnp.minimum(i + 1, n_steps - 1), 0, 0),
                               memory_space=pltpu.SMEM),
                  pl.BlockSpec((rows, LANES), lambda i: (i, 0)),
                  pl.BlockSpec((rows, d), lambda i: (i, 0)),
                  pl.BlockSpec((1, 1, d), gmap),
                  pl.BlockSpec(memory_space=pl.ANY)],
        out_specs=pl.BlockSpec((rows, d), lambda i: (i, 0)),
        scratch_shapes=[pltpu.VMEM((2, TOP_K, rows, d // 2), jnp.uint32), pltpu.SemaphoreType.DMA((2,))],
        compiler_params=_cparams(("arbitrary",)),
        name="moe_combine",
    )(pos3, pos3, gw2d, x2d, modt, y_sorted)


def _rope_tables(seq, ctx_len):
    n_rows = seq // GRID_W
    axis_rot = HEAD_DIM // 2
    rows = jnp.repeat(jnp.arange(n_rows, dtype=F32), GRID_W)
    cols = jnp.tile(jnp.arange(GRID_W, dtype=F32), n_rows)
    inv_freq = ROPE_BASE ** (-jnp.arange(0, axis_rot, 2, dtype=F32) / axis_rot)
    ang_r = rows[:, None] * inv_freq
    ang_c = cols[:, None] * inv_freq
    ang = jnp.concatenate([ang_r, ang_r, ang_c, ang_c], axis=-1)
    sign = jnp.tile(jnp.concatenate([-jnp.ones(axis_rot // 2, F32), jnp.ones(axis_rot // 2, F32)]), 2)
    cos = jnp.concatenate([jnp.ones((ctx_len, HEAD_DIM), F32), jnp.cos(ang)], axis=0)
    sin = jnp.concatenate([jnp.zeros((ctx_len, HEAD_DIM), F32), jnp.sin(ang) * sign], axis=0)
    return cos, sin


def kernel(x, c, ctx, c_ctx, w_mod, b_mod, norm_mix, norm_ffn, w_router, b_router, w_gate_up, b_gate_up, w_down, b_down, attn_w_qkv, attn_w_o, attn_q_gain, attn_k_gain, attn_sinks, ssm_a_re, ssm_a_im, ssm_log_dt, ssm_b_re, ssm_b_im, ssm_c_re, ssm_c_im, ssm_d, ssm_w_glu, ssm_b_glu):
    b, seq, d = x.shape
    ctx_len = ctx.shape[1]
    depth = w_mod.shape[0]
    p = ctx_len + seq
    t = b * p
    n_exp = w_router.shape[-1]
    n_q = d // HEAD_DIM
    n_kv = (attn_w_qkv.shape[-1] // HEAD_DIM - n_q) // 2
    assert ctx_len % ATT_BLOCK == 0 and seq % ATT_BLOCK == 0 and seq % GRID_W == 0
    assert d % (2 * LANES) == 0 and ctx_len % SSM_CHUNK == 0 and seq % SSM_CHUNK == 0

    tm = _pick(math.gcd(p, ctx_len), ROW_TILE)
    mm_tm = _pick(math.gcd(p, ctx_len), MM_TILE_M)
    tiles_pb = p // mm_tm
    ctx_tiles = ctx_len // mm_tm

    pad = (-(b + 1)) % 8
    cvec = jnp.concatenate([c, c_ctx[None, :], jnp.zeros((pad, d), F32)], axis=0)
    mod_all = _mod_all(cvec, w_mod, b_mod)

    cos_t, sin_t = _rope_tables(seq, ctx_len)
    xs = jnp.concatenate([ctx, x], axis=1)

    tm_moe = min(MOE_TILE, _pick(t * TOP_K, MOE_TILE, 8))
    n_tiles = (t * TOP_K) // tm_moe + n_exp
    scale = HEAD_DIM ** -0.5

    for i in range(depth):
        j = i // 2
        lat = mod_all[i, :b]
        cx = jnp.broadcast_to(mod_all[i, b][None], lat.shape)
        modt = jnp.stack([cx, lat], axis=1).reshape(b * 2, 1, 6 * d)

        x2d = xs.reshape(t, d)
        if i % 2 == 0:
            gain = jnp.concatenate([jnp.tile(attn_q_gain[j] * scale, n_q), jnp.tile(attn_k_gain[j], n_kv),
                                    jnp.ones((n_kv * HEAD_DIM,), F32)]).reshape(1, -1)
            qkv = _qkv_proj(x2d, norm_mix[i], modt, attn_w_qkv[j].astype(BF16), gain, cos_t, sin_t,
                            n_q + n_kv, mm_tm, ctx_tiles, tiles_pb)
            o = _attention(qkv.reshape(b, p, -1), attn_sinks[j].astype(F32), n_q, n_kv, ctx_len)
            x2d = _proj_residual(o.reshape(t, d), attn_w_o[j].astype(BF16), x2d, modt, 2,
                                 mm_tm, ctx_tiles, tiles_pb)
        else:
            u = _norm_mod(xs, norm_mix[i], modt, 0, ctx_len)
            tables = _s5_tables(ssm_a_re[j], ssm_a_im[j], ssm_log_dt[j], ssm_b_re[j], ssm_b_im[j],
                                ssm_c_re[j], ssm_c_im[j], ssm_d[j])
            gy = _s5_mix(u, tables, ctx_len)
            x2d = _glu_residual(gy.reshape(t, d), ssm_w_glu[j].astype(BF16), ssm_b_glu[j], x2d, modt, 2,
                                mm_tm, ctx_tiles, tiles_pb)

        hp, idx, gw, rank, counts = _norm_router(x2d.reshape(b, p, d), norm_ffn[i], modt, w_router[i],
                                                 b_router[i], ctx_len)
        tile_e, n_active, pos = _moe_schedule(idx.reshape(t, LANES), rank.reshape(t, LANES),
                                              counts.reshape(n_exp).astype(jnp.int32), tm_moe, n_tiles)
        x_sorted = _moe_scatter(hp.reshape(t, d // 2), pos, n_tiles * tm_moe, tm)
        wp = _moe_prep(w_gate_up, i)
        bp = _split_blocks(b_gate_up[i]).reshape(n_exp, wp.shape[1], 1, wp.shape[3])
        y_sorted = _moe_experts(x_sorted, tile_e, n_active, wp, bp,
                                _moe_cast(w_down, i), b_down[i].reshape(n_exp, 1, d), tm_moe)
        x2d = _moe_combine(y_sorted, pos, gw.reshape(t, LANES), x2d, modt, tm, ctx_len // tm, p // tm)
        xs = x2d.reshape(b, p, d)

    return xs[:, ctx_len:]
```

```python
import functools
import math

import jax
import jax.numpy as jnp
from jax import lax
from jax.experimental import pallas as pl
from jax.experimental.pallas import tpu as pltpu

F32 = jnp.float32
BF16 = jnp.bfloat16

HEAD_DIM = 128
ATT_BLOCK = 128
GRID_W = 64
ROPE_BASE = 10000.0
SSM_GROUP = 16
SSM_STATE = 64
SSM_CHUNK = 16
TOP_K = 4
SWIGLU_LIMIT = 7.0
SWIGLU_ALPHA = 1.702
EPS = 1e-6
NEG_INF = -1e30
DT_FLOOR_RE = -1e-4

LANES = 128
VMEM_LIMIT = 56 * 1024 * 1024

ROW_TILE = 256
MM_TILE_M = 512
PROJ_TILE_N = 2048
GLU_TILE_N = 1024
MOE_TILE = 512
COMBINE_ROWS = 128
SSM_GROUP_BLOCK = 16


def _cparams(sem):
    return pltpu.CompilerParams(dimension_semantics=sem, vmem_limit_bytes=VMEM_LIMIT)


def _pick(n, pref, mult=8):
    t = min(n, pref)
    while n % t or t % mult:
        t -= 1
    return t


def _mod_kernel(c_ref, w_ref, b_ref, o_ref):
    c = c_ref[...]
    s = c * jax.nn.sigmoid(c)
    o_ref[0] = jnp.dot(s.astype(BF16), w_ref[0].astype(BF16), preferred_element_type=F32) + b_ref[0]


def _mod_all(cvec, w_mod, b_mod):
    depth, d, n = w_mod.shape
    tn = _pick(n, 1024, LANES)
    return pl.pallas_call(
        _mod_kernel,
        out_shape=jax.ShapeDtypeStruct((depth, cvec.shape[0], n), F32),
        grid=(depth, n // tn),
        in_specs=[pl.BlockSpec(cvec.shape, lambda l, j: (0, 0)),
                  pl.BlockSpec((1, d, tn), lambda l, j: (l, 0, j)),
                  pl.BlockSpec((1, 1, tn), lambda l, j: (l, 0, j))],
        out_specs=pl.BlockSpec((1, cvec.shape[0], tn), lambda l, j: (l, 0, j)),
        compiler_params=_cparams(("arbitrary", "arbitrary")),
        name="adaln_rows",
    )(cvec, w_mod, b_mod.reshape(depth, 1, n))


def _normed(x, w, sh, sc):
    r = lax.rsqrt(jnp.mean(x * x, axis=-1, keepdims=True) + EPS)
    return (x * r * w) * (1.0 + sc) + sh


def _norm_mod_kernel(x_ref, w_ref, sh_ref, sc_ref, o_ref):
    o_ref[0] = _normed(x_ref[0], w_ref[...], sh_ref[0], sc_ref[0]).astype(o_ref.dtype)


def _mod_spec(d, which, ctx_tiles):
    return pl.BlockSpec((1, 1, d), lambda b, i: (b * 2 + jnp.where(i >= ctx_tiles, 1, 0), 0, which))


def _norm_mod(x, w, modt, which_shift, ctx_len):
    b, p, d = x.shape
    tm = _pick(math.gcd(p, ctx_len), ROW_TILE)
    ct = ctx_len // tm
    return pl.pallas_call(
        _norm_mod_kernel,
        out_shape=jax.ShapeDtypeStruct((b, p, d), BF16),
        grid=(b, p // tm),
        in_specs=[pl.BlockSpec((1, tm, d), lambda bb, i: (bb, i, 0)),
                  pl.BlockSpec((1, d), lambda bb, i: (0, 0)),
                  _mod_spec(d, which_shift, ct), _mod_spec(d, which_shift + 1, ct)],
        out_specs=pl.BlockSpec((1, tm, d), lambda bb, i: (bb, i, 0)),
        compiler_params=_cparams(("arbitrary", "arbitrary")),
        name="norm_mod",
    )(x, w.reshape(1, d), modt, modt)


def _pack_halves(h):
    half = h.shape[-1] // 2
    bits = lax.bitcast_convert_type(h.astype(BF16).astype(F32), jnp.uint32)
    return (bits[:, :half] >> 16) | (bits[:, half:] & jnp.uint32(0xFFFF0000))


def _unpack_halves(xp):
    lo = lax.bitcast_convert_type(xp << 16, F32)
    hi = lax.bitcast_convert_type(xp & jnp.uint32(0xFFFF0000), F32)
    return lo, hi


def _norm_router_kernel(x_ref, w_ref, sh_ref, sc_ref, wr_ref, br_ref, hp_ref, idx_ref, gw_ref, rank_ref,
                        cnt_ref, base):
    @pl.when((pl.program_id(0) == 0) & (pl.program_id(1) == 0))
    def _():
        base[...] = jnp.zeros(base.shape, F32)

    h = _normed(x_ref[0], w_ref[...], sh_ref[0], sc_ref[0])
    hp_ref[0] = _pack_halves(h)
    wr = wr_ref[...]
    h_hi = h.astype(BF16)
    h_lo = (h - h_hi.astype(F32)).astype(BF16)
    w_hi = wr.astype(BF16)
    w_lo = (wr - w_hi.astype(F32)).astype(BF16)
    logits = (jnp.dot(h_hi, w_hi, preferred_element_type=F32)
              + (jnp.dot(h_lo, w_hi, preferred_element_type=F32)
                 + jnp.dot(h_hi, w_lo, preferred_element_type=F32))) + br_ref[...]
    n_exp = logits.shape[-1]
    lane = lax.broadcasted_iota(jnp.int32, logits.shape, 1)
    out_lane = lax.broadcasted_iota(jnp.int32, idx_ref.shape[1:], 1)
    vals, idxs = [], []
    rest = logits
    for _ in range(TOP_K):
        m = jnp.max(rest, axis=-1, keepdims=True)
        idx = jnp.min(jnp.where(rest == m, lane, n_exp), axis=-1, keepdims=True)
        vals.append(m)
        idxs.append(idx)
        rest = jnp.where(lane == idx, -jnp.inf, rest)
    exps = [jnp.exp(v - vals[0]) for v in vals]
    tot = exps[0]
    for e in exps[1:]:
        tot = tot + e
    idx_out = jnp.zeros(idx_ref.shape[1:], jnp.int32)
    gw_out = jnp.zeros(gw_ref.shape[1:], F32)
    for k in range(TOP_K):
        idx_out = jnp.where(out_lane == k, idxs[k], idx_out)
        gw_out = jnp.where(out_lane == k, exps[k] / tot, gw_out)
    idx_ref[0] = idx_out
    gw_ref[0] = gw_out

    sel = jnp.zeros(logits.shape, F32)
    for k in range(TOP_K):
        sel = sel + (lane == idxs[k]).astype(F32)
    tm = sel.shape[0]
    earlier = (lax.broadcasted_iota(jnp.int32, (tm, tm), 1)
               < lax.broadcasted_iota(jnp.int32, (tm, tm), 0)).astype(BF16)
    before = jnp.dot(earlier, sel.astype(BF16), preferred_element_type=F32) + base[...]
    rank_out = jnp.zeros(rank_ref.shape[1:], jnp.int32)
    for k in range(TOP_K):
        rk = jnp.sum(jnp.where(lane == idxs[k], before, 0.0), axis=-1, keepdims=True)
        rank_out = jnp.where(out_lane == k, rk.astype(jnp.int32), rank_out)
    rank_ref[0] = rank_out
    base[...] = base[...] + jnp.sum(sel, axis=0, keepdims=True)
    cnt_ref[...] = base[...]


def _norm_router(x, w, modt, w_router, b_router, ctx_len):
    b, p, d = x.shape
    n_exp = w_router.shape[-1]
    tm = _pick(math.gcd(p, ctx_len), ROW_TILE)
    ct = ctx_len // tm
    row = lambda bb, i: (bb, i, 0)
    return pl.pallas_call(
        _norm_router_kernel,
        out_shape=(jax.ShapeDtypeStruct((b, p, d // 2), jnp.uint32),
                   jax.ShapeDtypeStruct((b, p, LANES), jnp.int32),
                   jax.ShapeDtypeStruct((b, p, LANES), F32),
                   jax.ShapeDtypeStruct((b, p, LANES), jnp.int32),
                   jax.ShapeDtypeStruct((1, n_exp), F32)),
        grid=(b, p // tm),
        in_specs=[pl.BlockSpec((1, tm, d), row),
                  pl.BlockSpec((1, d), lambda bb, i: (0, 0)),
                  _mod_spec(d, 3, ct), _mod_spec(d, 4, ct),
                  pl.BlockSpec((d, n_exp), lambda bb, i: (0, 0)),
                  pl.BlockSpec((1, n_exp), lambda bb, i: (0, 0))],
        out_specs=(pl.BlockSpec((1, tm, d // 2), row),
                   pl.BlockSpec((1, tm, LANES), row),
                   pl.BlockSpec((1, tm, LANES), row),
                   pl.BlockSpec((1, tm, LANES), row),
                   pl.BlockSpec((1, n_exp), lambda bb, i: (0, 0))),
        scratch_shapes=[pltpu.VMEM((1, n_exp), F32)],
        compiler_params=_cparams(("arbitrary", "arbitrary")),
        name="norm_router",
    )(x, w.reshape(1, d), modt, modt, w_router, b_router.reshape(1, n_exp))


def _qkv_kernel(x_ref, nw_ref, sh_ref, sc_ref, w_ref, g_ref, cos_ref, sin_ref, o_ref, *, n_norm_heads):
    u = _normed(x_ref[...], nw_ref[...], sh_ref[0], sc_ref[0]).astype(BF16)
    acc = jnp.dot(u, w_ref[...], preferred_element_type=F32)
    cos = cos_ref[...]
    sin = sin_ref[...]
    lane = lax.broadcasted_iota(jnp.int32, cos.shape, 1)
    first = (lane // (HEAD_DIM // 4)) % 2 == 0
    for hh in range(acc.shape[1] // HEAD_DIM):
        sl = slice(hh * HEAD_DIM, (hh + 1) * HEAD_DIM)
        xh = acc[:, sl]
        if hh < n_norm_heads:
            r = lax.rsqrt(jnp.mean(xh * xh, axis=-1, keepdims=True) + EPS)
            y = xh * r * g_ref[:, sl]
            partner = jnp.where(first, pltpu.roll(y, HEAD_DIM - HEAD_DIM // 4, 1),
                                pltpu.roll(y, HEAD_DIM // 4, 1))
            xh = y * cos + partner * sin
        o_ref[:, sl] = xh.astype(o_ref.dtype)


def _row_mod_spec(d, which, ctx_tiles, tiles_per_batch):
    def imap(i):
        seg = jnp.where(i % tiles_per_batch >= ctx_tiles, 1, 0)
        return ((i // tiles_per_batch) * 2 + seg, 0, which)

    return pl.BlockSpec((1, 1, d), imap)


def _qkv_proj(x2d, norm_w, modt, w_bf16, gain_row, cos_t, sin_t, n_norm_heads, tm, ctx_tiles, tiles_per_batch):
    m, k = x2d.shape
    n = w_bf16.shape[1]
    pos = lambda i: (i % tiles_per_batch, 0)
    return pl.pallas_call(
        functools.partial(_qkv_kernel, n_norm_heads=n_norm_heads),
        out_shape=jax.ShapeDtypeStruct((m, n), BF16),
        grid=(m // tm,),
        in_specs=[pl.BlockSpec((tm, k), lambda i: (i, 0)),
                  pl.BlockSpec((1, k), lambda i: (0, 0)),
                  _row_mod_spec(k, 0, ctx_tiles, tiles_per_batch),
                  _row_mod_spec(k, 1, ctx_tiles, tiles_per_batch),
                  pl.BlockSpec((k, n), lambda i: (0, 0)),
                  pl.BlockSpec((1, n), lambda i: (0, 0)),
                  pl.BlockSpec((tm, HEAD_DIM), pos),
                  pl.BlockSpec((tm, HEAD_DIM), pos)],
        out_specs=pl.BlockSpec((tm, n), lambda i: (i, 0)),
        compiler_params=_cparams(("arbitrary",)),
        name="qkv_proj",
    )(x2d, norm_w.reshape(1, k), modt, modt, w_bf16, gain_row, cos_t, sin_t)


def _attn_kernel(sink_ref, q_ref, kc_ref, vc_ref, kp_ref, ks_ref, kn_ref, vp_ref, vs_ref, vn_ref,
                 o_ref, *, q_per_kv, heads, ctx_blocks, n_blocks):
    hg = pl.program_id(1)
    i = pl.program_id(2)
    blk = ATT_BLOCK
    rows = q_per_kv * blk
    qw = q_per_kv * HEAD_DIM
    dn = (((1,), (1,)), ((), ()))

    qi = lax.broadcasted_iota(jnp.int32, (rows, blk), 0) % blk
    kj = lax.broadcasted_iota(jnp.int32, (rows, blk), 1)
    is_lat = i >= ctx_blocks
    ok_prev = (kj >= qi) & (i > ctx_blocks)
    ok_self = jnp.broadcast_to(is_lat, (rows, blk))
    ok_next = (kj <= qi) & is_lat & (i < n_blocks - 1)
    valid = jnp.concatenate([ok_prev, ok_self, ok_next], axis=1)
    row_head = lax.broadcasted_iota(jnp.int32, (rows, 1), 0) // blk

    for hh in range(heads):
        hd = slice(hh * HEAD_DIM, (hh + 1) * HEAD_DIM)
        q = q_ref[0, :, hh * qw:(hh + 1) * qw]
        qs = jnp.concatenate([q[:, g * HEAD_DIM:(g + 1) * HEAD_DIM] for g in range(q_per_kv)], axis=0)
        k_loc = jnp.concatenate([kp_ref[0, :, hd], ks_ref[0, :, hd], kn_ref[0, :, hd]], axis=0)
        v_loc = jnp.concatenate([vp_ref[0, :, hd], vs_ref[0, :, hd], vn_ref[0, :, hd]], axis=0)
        s_loc = jnp.where(valid, lax.dot_general(qs, k_loc, dn, preferred_element_type=F32), NEG_INF)
        s_ctx = lax.dot_general(qs, kc_ref[0, :, hd], dn, preferred_element_type=F32)

        sink = jnp.zeros((rows, 1), F32)
        for g in range(q_per_kv):
            sink = jnp.where(row_head == g, sink_ref[(hg * heads + hh) * q_per_kv + g], sink)

        m = jnp.maximum(jnp.maximum(jnp.max(s_loc, axis=-1, keepdims=True),
                                    jnp.max(s_ctx, axis=-1, keepdims=True)), sink)
        p_loc = jnp.exp(s_loc - m)
        p_ctx = jnp.exp(s_ctx - m)
        denom = (jnp.sum(p_loc, axis=-1, keepdims=True) + jnp.sum(p_ctx, axis=-1, keepdims=True)
                 + jnp.exp(sink - m))
        o = (jnp.dot(p_loc.astype(BF16), v_loc, preferred_element_type=F32)
             + jnp.dot(p_ctx.astype(BF16), vc_ref[0, :, hd], preferred_element_type=F32)) / denom
        o = o.astype(o_ref.dtype)
        for g in range(q_per_kv):
            o_ref[0, :, hh * qw + g * HEAD_DIM:hh * qw + (g + 1) * HEAD_DIM] = o[g * blk:(g + 1) * blk]


def _attention(qkv, sinks, n_q, n_kv, ctx_len):
    b, p, _ = qkv.shape
    blk = ATT_BLOCK
    nb = p // blk
    cb = ctx_len // blk
    qpk = n_q // n_kv
    heads = next(h for h in (4, 2, 1) if n_kv % h == 0 and n_q % h == 0)
    qw = heads * qpk * HEAD_DIM
    hw = heads * HEAD_DIM
    kcol = n_q // heads
    vcol = (n_q + n_kv) // heads

    def loc(col0, off):
        return pl.BlockSpec(
            (1, blk, hw),
            lambda bb, h, i, s: (bb, jnp.clip(i + off, cb, nb - 1), col0 + h))

    return pl.pallas_call(
        functools.partial(_attn_kernel, q_per_kv=qpk, heads=heads, ctx_blocks=cb, n_blocks=nb),
        out_shape=jax.ShapeDtypeStruct((b, p, n_q * HEAD_DIM), BF16),
        grid_spec=pltpu.PrefetchScalarGridSpec(
            num_scalar_prefetch=1,
            grid=(b, n_kv // heads, nb),
            in_specs=[pl.BlockSpec((1, blk, qw), lambda bb, h, i, s: (bb, i, h)),
                      pl.BlockSpec((1, ctx_len, hw), lambda bb, h, i, s: (bb, 0, kcol + h)),
                      pl.BlockSpec((1, ctx_len, hw), lambda bb, h, i, s: (bb, 0, vcol + h)),
                      loc(kcol, -1), loc(kcol, 0), loc(kcol, 1),
                      loc(vcol, -1), loc(vcol, 0), loc(vcol, 1)],
            out_specs=pl.BlockSpec((1, blk, qw), lambda bb, h, i, s: (bb, i, h))),
        compiler_params=_cparams(("arbitrary", "arbitrary", "arbitrary")),
        name="window_attn",
    )(sinks, qkv, qkv, qkv, qkv, qkv, qkv, qkv, qkv, qkv)


def _gate_spec(d, which, ctx_tiles, tiles_per_batch, tn):
    per = d // tn

    def imap(j, i):
        bb = i // tiles_per_batch
        seg = jnp.where(i % tiles_per_batch >= ctx_tiles, 1, 0)
        return (bb * 2 + seg, 0, which * per + j)

    return pl.BlockSpec((1, 1, tn), imap)


def _proj_res_kernel(a_ref, w_ref, x_ref, g_ref, o_ref):
    y = jnp.dot(a_ref[...], w_ref[...], preferred_element_type=F32)
    o_ref[...] = x_ref[...] + g_ref[0] * y


def _proj_residual(a2d, w_bf16, x2d, modt, which_gate, tm, ctx_tiles, tiles_per_batch):
    m, k = a2d.shape
    n = w_bf16.shape[1]
    tn = _pick(n, PROJ_TILE_N, LANES)
    return pl.pallas_call(
        _proj_res_kernel,
        out_shape=jax.ShapeDtypeStruct((m, n), F32),
        grid=(n // tn, m // tm),
        in_specs=[pl.BlockSpec((tm, k), lambda j, i: (i, 0)),
                  pl.BlockSpec((k, tn), lambda j, i: (0, j)),
                  pl.BlockSpec((tm, tn), lambda j, i: (i, j)),
                  _gate_spec(n, which_gate, ctx_tiles, tiles_per_batch, tn)],
        out_specs=pl.BlockSpec((tm, tn), lambda j, i: (i, j)),
        compiler_params=_cparams(("arbitrary", "arbitrary")),
        name="proj_residual",
    )(a2d, w_bf16, x2d, modt)


def _glu_res_kernel(a_ref, w1_ref, w2_ref, b1_ref, b2_ref, x_ref, g_ref, o_ref):
    a = a_ref[...]
    z1 = jnp.dot(a, w1_ref[...], preferred_element_type=F32) + b1_ref[...]
    z2 = jnp.dot(a, w2_ref[...], preferred_element_type=F32) + b2_ref[...]
    o_ref[...] = x_ref[...] + g_ref[0] * (z1 * jax.nn.sigmoid(z2))


def _glu_residual(a2d, w_bf16, b_glu, x2d, modt, which_gate, tm, ctx_tiles, tiles_per_batch):
    m, k = a2d.shape
    d = w_bf16.shape[1] // 2
    tn = _pick(d, GLU_TILE_N, LANES)
    nj = d // tn
    b2 = b_glu.reshape(1, 2 * d)
    return pl.pallas_call(
        _glu_res_kernel,
        out_shape=jax.ShapeDtypeStruct((m, d), F32),
        grid=(nj, m // tm),
        in_specs=[pl.BlockSpec((tm, k), lambda j, i: (i, 0)),
                  pl.BlockSpec((k, tn), lambda j, i: (0, j)),
                  pl.BlockSpec((k, tn), lambda j, i: (0, j + nj)),
                  pl.BlockSpec((1, tn), lambda j, i: (0, j)),
                  pl.BlockSpec((1, tn), lambda j, i: (0, j + nj)),
                  pl.BlockSpec((tm, tn), lambda j, i: (i, j)),
                  _gate_spec(d, which_gate, ctx_tiles, tiles_per_batch, tn)],
        out_specs=pl.BlockSpec((tm, tn), lambda j, i: (i, j)),
        compiler_params=_cparams(("arbitrary", "arbitrary")),
        name="glu_residual",
    )(a2d, w_bf16, w_bf16, b2, b2, x2d, modt)


def _s5_tables(a_re, a_im, log_dt, b_re, b_im, c_re, c_im, d_skip):
    q, n_state, pch = SSM_CHUNK, a_re.shape[-1], b_re.shape[-1]
    g = a_re.shape[1]
    hp = lax.Precision.HIGHEST
    lam = lax.complex(jnp.minimum(a_re.astype(F32), DT_FLOOR_RE), a_im.astype(F32))
    lam_dt = lam * jnp.exp(log_dt.astype(F32))[..., None]
    b_bar = ((jnp.exp(lam_dt) - 1.0) / lam)[..., None] * lax.complex(b_re.astype(F32), b_im.astype(F32))
    c_mat = lax.complex(c_re.astype(F32), c_im.astype(F32))
    tau = jnp.arange(q + 1, dtype=F32)
    pw = jnp.exp(lam_dt[:, :, None, :] * tau[None, None, :, None])
    kern = jnp.einsum('dgpn,dgtn,dgnr->dgtpr', c_mat, pw[:, :, :q], b_bar, precision=hp).real
    t_i = jnp.arange(q)[:, None]
    s_i = jnp.arange(q)[None, :]
    kf = jnp.zeros((g, q, q, pch, pch), F32)
    kb = jnp.zeros((g, q, q, pch, pch), F32)
    for lag in range(q):
        kf = jnp.where((t_i - s_i == lag)[None, :, :, None, None], kern[0][:, lag][:, None, None], kf)
        kb = jnp.where((s_i - t_i == lag)[None, :, :, None, None], kern[1][:, lag][:, None, None], kb)
    dsk = d_skip.astype(F32).reshape(g, pch)
    diag = (jnp.eye(q)[None, :, :, None, None] * jnp.eye(pch)[None, None, None] * dsk[:, None, None, :, None])
    mt = (kf + kb + diag).transpose(0, 2, 4, 1, 3).reshape(g, q * pch, q * pch)
    wsf = pw[0][:, ::-1][:, 1:, :, None] * b_bar[0][:, None]
    wsb = pw[1][:, :q, :, None] * b_bar[1][:, None]
    wsf = wsf.transpose(0, 1, 3, 2).reshape(g, q * pch, n_state)
    wsb = wsb.transpose(0, 1, 3, 2).reshape(g, q * pch, n_state)
    w1 = jnp.concatenate([mt, wsf.real, wsb.real, wsf.imag, wsb.imag], axis=-1)
    cf = c_mat[0][:, None] * pw[0][:, 1:, None, :]
    cb = c_mat[1][:, None] * pw[1][:, ::-1][:, :q, None, :]
    cf = cf.transpose(0, 3, 1, 2).reshape(g, n_state, q * pch)
    cb = cb.transpose(0, 3, 1, 2).reshape(g, n_state, q * pch)
    w3 = jnp.concatenate([cf.real, cb.real, -cf.imag, -cb.imag], axis=1)
    aq = pw[:, :, q]
    a_r = jnp.concatenate([aq[0].real, aq[1].real], axis=-1)
    a_i = jnp.concatenate([aq[0].imag, aq[1].imag], axis=-1)
    return w1.astype(BF16), w3.astype(BF16), a_r, a_i


def _s5_in_kernel(u_ref, w_ref, y_ref, sr_ref, si_ref):
    width = u_ref.shape[-1]
    ns2 = sr_ref.shape[-1]
    for g in range(u_ref.shape[0]):
        r = jnp.dot(u_ref[g], w_ref[g], preferred_element_type=F32)
        y_ref[g] = r[:, :width]
        sr_ref[0, g] = r[:, width:width + ns2]
        si_ref[0, g] = r[:, width + ns2:]


def _s5_scan_kernel(sr_ref, si_ref, ar_ref, ai_ref, hr_ref, hi_ref, fr, fi, br, bi, *, n_chunks, ctx_chunks):
    gb = ar_ref.shape[0]
    a_r = ar_ref[...]
    a_i = ai_ref[...]
    lane = lax.broadcasted_iota(jnp.int32, a_r.shape, 1)
    fwd = lane < (a_r.shape[1] // 2)

    def rows(ref, c):
        return ref[pl.ds(c, gb, stride=n_chunks), :]

    def body(i, carry):
        h_r, h_i = carry
        cf = i
        cb = jnp.where(i < ctx_chunks, ctx_chunks - 1 - i, n_chunks + ctx_chunks - 1 - i)
        fr[pl.ds(cf, gb, stride=n_chunks), :] = h_r
        fi[pl.ds(cf, gb, stride=n_chunks), :] = h_i
        br[pl.ds(cb, gb, stride=n_chunks), :] = h_r
        bi[pl.ds(cb, gb, stride=n_chunks), :] = h_i
        s_r = jnp.where(fwd, rows(sr_ref, cf), rows(sr_ref, cb))
        s_i = jnp.where(fwd, rows(si_ref, cf), rows(si_ref, cb))
        return (a_r * h_r - a_i * h_i + s_r, a_r * h_i + a_i * h_r + s_i)

    zero = jnp.zeros(a_r.shape, F32)
    lax.fori_loop(0, n_chunks, body, (zero, zero))
    full = lax.broadcasted_iota(jnp.int32, hr_ref.shape, 1) < (a_r.shape[1] // 2)
    hr_ref[...] = jnp.where(full, fr[...], br[...])
    hi_ref[...] = jnp.where(full, fi[...], bi[...])


def _s5_out_kernel(y_ref, hr_ref, hi_ref, w_ref, o_ref):
    ns2 = hr_ref.shape[-1]
    for g in range(y_ref.shape[0]):
        y = (y_ref[g]
             + jnp.dot(hr_ref[0, g].astype(BF16), w_ref[g, :ns2], preferred_element_type=F32)
             + jnp.dot(hi_ref[0, g].astype(BF16), w_ref[g, ns2:], preferred_element_type=F32))
        o_ref[g] = jax.nn.gelu(y).astype(o_ref.dtype)


def _s5_mix(u, tables, ctx_len):
    w1, w3, a_r, a_i = tables
    b, p, d = u.shape
    q, pch = SSM_CHUNK, SSM_GROUP
    g = d // pch
    nc = p // q
    width = q * pch
    ns2 = 2 * SSM_STATE
    gb = _pick(g, SSM_GROUP_BLOCK, 8)
    ut = u.reshape(b, nc, q, g, pch).transpose(3, 0, 1, 2, 4).reshape(g, b * nc, width)

    y_in, s_r, s_i = pl.pallas_call(
        _s5_in_kernel,
        out_shape=(jax.ShapeDtypeStruct((g, b * nc, width), F32),
                   jax.ShapeDtypeStruct((b, g, nc, ns2), F32),
                   jax.ShapeDtypeStruct((b, g, nc, ns2), F32)),
        grid=(g // gb, b),
        in_specs=[pl.BlockSpec((gb, nc, width), lambda gi, bb: (gi, bb, 0)),
                  pl.BlockSpec((gb, width, width + 2 * ns2), lambda gi, bb: (gi, 0, 0))],
        out_specs=(pl.BlockSpec((gb, nc, width), lambda gi, bb: (gi, bb, 0)),
                   pl.BlockSpec((1, gb, nc, ns2), lambda gi, bb: (bb, gi, 0, 0)),
                   pl.BlockSpec((1, gb, nc, ns2), lambda gi, bb: (bb, gi, 0, 0))),
        compiler_params=_cparams(("arbitrary", "arbitrary")),
        name="s5_chunk_in",
    )(ut, w1)

    flat = lambda gi, bb: (bb * (g // gb) + gi, 0)
    h_r, h_i = pl.pallas_call(
        functools.partial(_s5_scan_kernel, n_chunks=nc, ctx_chunks=ctx_len // q),
        out_shape=(jax.ShapeDtypeStruct((b * g * nc, ns2), F32),) * 2,
        grid=(g // gb, b),
        in_specs=[pl.BlockSpec((gb * nc, ns2), flat), pl.BlockSpec((gb * nc, ns2), flat),
                  pl.BlockSpec((gb, ns2), lambda gi, bb: (gi, 0)),
                  pl.BlockSpec((gb, ns2), lambda gi, bb: (gi, 0))],
        out_specs=(pl.BlockSpec((gb * nc, ns2), flat),) * 2,
        scratch_shapes=[pltpu.VMEM((gb * nc, ns2), F32)] * 4,
        compiler_params=_cparams(("arbitrary", "arbitrary")),
        name="s5_chunk_scan",
    )(s_r.reshape(b * g * nc, ns2), s_i.reshape(b * g * nc, ns2), a_r, a_i)

    yt = pl.pallas_call(
        _s5_out_kernel,
        out_shape=jax.ShapeDtypeStruct((g, b * nc, width), BF16),
        grid=(g // gb, b),
        in_specs=[pl.BlockSpec((gb, nc, width), lambda gi, bb: (gi, bb, 0)),
                  pl.BlockSpec((1, gb, nc, ns2), lambda gi, bb: (bb, gi, 0, 0)),
                  pl.BlockSpec((1, gb, nc, ns2), lambda gi, bb: (bb, gi, 0, 0)),
                  pl.BlockSpec((gb, 2 * ns2, width), lambda gi, bb: (gi, 0, 0))],
        out_specs=pl.BlockSpec((gb, nc, width), lambda gi, bb: (gi, bb, 0)),
        compiler_params=_cparams(("arbitrary", "arbitrary")),
        name="s5_chunk_out",
    )(y_in, h_r.reshape(b, g, nc, ns2), h_i.reshape(b, g, nc, ns2), w3)
    return yt.reshape(g, b, nc, q, pch).transpose(1, 2, 3, 0, 4).reshape(b, p, d)


def _moe_schedule(idx, rank, counts, tm, n_tiles):
    n_exp = counts.shape[0]
    padded = (counts + tm - 1) // tm * tm
    gstart = jnp.cumsum(padded) - padded
    gend = gstart + padded
    n_active = (jnp.sum(padded) // tm).astype(jnp.int32)
    tile_start = jnp.arange(n_tiles, dtype=jnp.int32) * tm
    tile_e = jnp.sum((gend[None, :] <= tile_start[:, None]).astype(jnp.int32), axis=1)
    tile_e = jnp.minimum(tile_e, n_exp - 1)
    tile_e = jnp.where(jnp.arange(n_tiles) < n_active, tile_e, tile_e[jnp.maximum(n_active - 1, 0)])
    start = jnp.zeros(idx.shape, jnp.int32)
    for e in range(n_exp):
        start = jnp.where(idx == e, gstart[e], start)
    pos = (start + rank)[:, :TOP_K].astype(jnp.int32)
    return tile_e.astype(jnp.int32), n_active.reshape(1), pos


def _scatter_kernel(pos_ref, h_ref, base_hbm, o_hbm, size_buf, sem, *, rows):
    del base_hbm

    def body(r, c):
        for k in range(TOP_K):
            pltpu.make_async_copy(h_ref.at[pl.ds(r, 1)], o_hbm.at[pl.ds(pos_ref[0, 0, r * TOP_K + k], 1)],
                                  sem).start()
        return c
    lax.fori_loop(0, rows, body, 0, unroll=4)
    for _ in range(TOP_K):
        pltpu.make_async_copy(size_buf, size_buf, sem).wait()


def _moe_scatter(hp2d, pos, base_rows, rows):
    t, half = hp2d.shape
    n_steps = t // rows
    n_rows = base_rows.shape[0]
    return pl.pallas_call(
        functools.partial(_scatter_kernel, rows=rows),
        out_shape=jax.ShapeDtypeStruct((n_rows, half), hp2d.dtype),
        grid=(n_steps,),
        in_specs=[pl.BlockSpec((1, 1, rows * TOP_K), lambda i: (i, 0, 0), memory_space=pltpu.SMEM),
                  pl.BlockSpec((rows, half), lambda i: (i, 0)),
                  pl.BlockSpec(memory_space=pl.ANY)],
        out_specs=pl.BlockSpec(memory_space=pl.ANY),
        scratch_shapes=[pltpu.VMEM((rows, half), hp2d.dtype), pltpu.SemaphoreType.DMA(())],
        input_output_aliases={2: 0},
        compiler_params=_cparams(("arbitrary",)),
        name="moe_scatter",
    )(pos.reshape(n_steps, 1, rows * TOP_K), hp2d, base_rows)


def _moe_prep_kernel(w_ref, sel_ref, o_ref):
    sel = sel_ref[...]
    blk = sel.shape[0]
    per = o_ref.shape[3] // blk
    for c in range(w_ref.shape[3] // blk):
        w = w_ref[0, 0, :, c * blk:(c + 1) * blk].astype(BF16)
        o_ref[0, c // per, :, (c % per) * blk:(c % per + 1) * blk] = jnp.dot(
            w, sel, preferred_element_type=F32).astype(BF16)


def _moe_chunk_width(n):
    return 4 * LANES if n % (4 * LANES) == 0 else 2 * LANES


def _moe_prep(w_gate_up, layer):
    _, n_exp, d, n = w_gate_up.shape
    blk = 2 * LANES
    cw = _moe_chunk_width(n)
    tk = _pick(d, 512, 8)
    src = jnp.arange(blk)
    sel = (jnp.arange(blk)[None, :] == (src // 2 + (src % 2) * LANES)[:, None]).astype(BF16)
    return pl.pallas_call(
        _moe_prep_kernel,
        out_shape=jax.ShapeDtypeStruct((n_exp, n // cw, d, cw), BF16),
        grid=(n_exp, d // tk),
        in_specs=[pl.BlockSpec((1, 1, tk, n), lambda e, k: (layer, e, k, 0)),
                  pl.BlockSpec((blk, blk), lambda e, k: (0, 0))],
        out_specs=pl.BlockSpec((1, n // cw, tk, cw), lambda e, k: (e, 0, k, 0)),
        compiler_params=_cparams(("arbitrary", "arbitrary")),
        name="moe_weight_prep",
    )(w_gate_up, sel)


def _cast_kernel(w_ref, o_ref):
    o_ref[0] = w_ref[0, 0].astype(o_ref.dtype)


def _moe_cast(w_down, layer):
    _, n_exp, f, d = w_down.shape
    tk = _pick(f, 512, 8)
    return pl.pallas_call(
        _cast_kernel,
        out_shape=jax.ShapeDtypeStruct((n_exp, f, d), BF16),
        grid=(n_exp, f // tk),
        in_specs=[pl.BlockSpec((1, 1, tk, d), lambda e, k: (layer, e, k, 0))],
        out_specs=pl.BlockSpec((1, tk, d), lambda e, k: (e, k, 0)),
        compiler_params=_cparams(("arbitrary", "arbitrary")),
        name="moe_weight_cast",
    )(w_down)


def _split_blocks(v):
    lead = v.shape[:-1]
    return v.reshape(lead + (-1, LANES, 2)).swapaxes(-1, -2).reshape(lead + (-1,))


def _moe_kernel(te_ref, na_ref, x_ref, w_ref, b_ref, wd_ref, bd_ref, o_ref):
    i = pl.program_id(0)
    n_act = na_ref[0]

    @pl.when(i < n_act)
    def _():
        lo, hi = _unpack_halves(x_ref[...])
        xb = jnp.concatenate([lo.astype(BF16), hi.astype(BF16)], axis=1)
        acts = []
        for c in range(w_ref.shape[1]):
            h = jnp.dot(xb, w_ref[0, c], preferred_element_type=F32) + b_ref[0, c]
            for q in range(h.shape[1] // (2 * LANES)):
                gate = jnp.minimum(h[:, 2 * q * LANES:(2 * q + 1) * LANES], SWIGLU_LIMIT)
                up = jnp.clip(h[:, (2 * q + 1) * LANES:(2 * q + 2) * LANES], -SWIGLU_LIMIT, SWIGLU_LIMIT)
                acts.append(((up + 1.0) * (gate * jax.nn.sigmoid(SWIGLU_ALPHA * gate))).astype(BF16))
        a = acts[0] if len(acts) == 1 else jnp.concatenate(acts, axis=1)
        y = jnp.dot(a, wd_ref[0], preferred_element_type=F32) + bd_ref[0]
        o_ref[:, 0, :] = _pack_halves(y)

    @pl.when(i >= n_act)
    def _():
        o_ref[...] = jnp.zeros(o_ref.shape, o_ref.dtype)


def _moe_experts(x_sorted, tile_e, n_active, wp, bp, wd, bd, tm):
    r_max, half = x_sorted.shape
    n_exp, n_chunks, d, cw = wp.shape
    f = wd.shape[1]
    n_tiles = r_max // tm
    last = lambda i, te, na: (jnp.minimum(i, jnp.maximum(na[0] - 1, 0)), 0)
    exp3 = lambda i, te, na: (te[i], 0, 0)
    exp4 = lambda i, te, na: (te[i], 0, 0, 0)
    return pl.pallas_call(
        _moe_kernel,
        out_shape=jax.ShapeDtypeStruct((r_max, 1, half), jnp.uint32),
        grid_spec=pltpu.PrefetchScalarGridSpec(
            num_scalar_prefetch=2,
            grid=(n_tiles,),
            in_specs=[pl.BlockSpec((tm, half), last),
                      pl.BlockSpec((1, n_chunks, d, cw), exp4), pl.BlockSpec((1, n_chunks, 1, cw), exp4),
                      pl.BlockSpec((1, f, d), exp3), pl.BlockSpec((1, 1, d), exp3)],
            out_specs=pl.BlockSpec((tm, 1, half), lambda i, te, na: (i, 0, 0))),
        compiler_params=_cparams(("arbitrary",)),
        name="moe_experts",
    )(tile_e, n_active, x_sorted, wp, bp, wd, bd)


def _combine_kernel(pos_ref, posn_ref, gw_ref, x_ref, g_ref, y_hbm, o_ref, buf, sem, *, rows):
    i = pl.program_id(0)
    n = pl.num_programs(0)

    slot = i % 2

    def row_copy(src, k, r, s):
        return pltpu.make_async_copy(y_hbm.at[src], buf.at[s, k, pl.ds(r, 1)], sem.at[s])

    @pl.when(i == 0)
    def _():
        def body(r, c):
            for k in range(TOP_K):
                row_copy(pos_ref[0, 0, r * TOP_K + k], k, r, 0).start()
            return c
        lax.fori_loop(0, rows, body, 0, unroll=2)

    pltpu.make_async_copy(buf.at[1 - slot], buf.at[slot], sem.at[slot]).wait()

    def combine():
        gw = gw_ref[...]
        half = buf.shape[-1]
        acc_lo = acc_hi = None
        for k in range(TOP_K):
            lo, hi = _unpack_halves(buf[slot, k])
            w = gw[:, k:k + 1]
            acc_lo = w * lo if k == 0 else acc_lo + w * lo
            acc_hi = w * hi if k == 0 else acc_hi + w * hi
        g = g_ref[0]
        o_ref[:, :half] = x_ref[:, :half] + g[:, :half] * acc_lo
        o_ref[:, half:] = x_ref[:, half:] + g[:, half:] * acc_hi

    @pl.when(i + 1 < n)
    def _():
        for r in range(rows):
            for k in range(TOP_K):
                row_copy(posn_ref[0, 0, r * TOP_K + k], k, r, 1 - slot).start()
        combine()

    @pl.when(i + 1 == n)
    def _():
        combine()


def _moe_combine(y_sorted, pos, gw2d, x2d, modt, tm_row, ctx_tiles, tiles_per_batch):
    t, d = x2d.shape
    rows = _pick(tm_row, COMBINE_ROWS, 8)
    n_steps = t // rows
    sub = tm_row // rows
    pos3 = pos.reshape(n_steps, 1, rows * TOP_K)

    def gmap(i):
        tile = i // sub
        bb = tile // tiles_per_batch
        seg = jnp.where(tile % tiles_per_batch >= ctx_tiles, 1, 0)
        return (bb * 2 + seg, 0, 5)

    return pl.pallas_call(
        functools.partial(_combine_kernel, rows=rows),
        out_shape=jax.ShapeDtypeStruct((t, d), F32),
        grid=(n_steps,),
        in_specs=[pl.BlockSpec((1, 1, rows * TOP_K), lambda i: (i, 0, 0), memory_space=pltpu.SMEM),
                  pl.BlockSpec((1, 1, rows * TOP_K), lambda i: (jnp.minimum(i + 1, n_steps - 1), 0, 0),
                               memory_space=pltpu.SMEM),
                  pl.BlockSpec((rows, LANES), lambda i: (i, 0)),
                  pl.BlockSpec((rows, d), lambda i: (i, 0)),
                  pl.BlockSpec((1, 1, d), gmap),
                  pl.BlockSpec(memory_space=pl.ANY)],
        out_specs=pl.BlockSpec((rows, d), lambda i: (i, 0)),
        scratch_shapes=[pltpu.VMEM((2, TOP_K, rows, d // 2), jnp.uint32), pltpu.SemaphoreType.DMA((2,))],
        compiler_params=_cparams(("arbitrary",)),
        name="moe_combine",
    )(pos3, pos3, gw2d, x2d, modt, y_sorted)


def _rope_tables(seq, ctx_len):
    n_rows = seq // GRID_W
    axis_rot = HEAD_DIM // 2
    rows = jnp.repeat(jnp.arange(n_rows, dtype=F32), GRID_W)
    cols = jnp.tile(jnp.arange(GRID_W, dtype=F32), n_rows)
    inv_freq = ROPE_BASE ** (-jnp.arange(0, axis_rot, 2, dtype=F32) / axis_rot)
    ang_r = rows[:, None] * inv_freq
    ang_c = cols[:, None] * inv_freq
    ang = jnp.concatenate([ang_r, ang_r, ang_c, ang_c], axis=-1)
    sign = jnp.tile(jnp.concatenate([-jnp.ones(axis_rot // 2, F32), jnp.ones(axis_rot // 2, F32)]), 2)
    cos = jnp.concatenate([jnp.ones((ctx_len, HEAD_DIM), F32), jnp.cos(ang)], axis=0)
    sin = jnp.concatenate([jnp.zeros((ctx_len, HEAD_DIM), F32), jnp.sin(ang) * sign], axis=0)
    return cos, sin


def kernel(x, c, ctx, c_ctx, w_mod, b_mod, norm_mix, norm_ffn, w_router, b_router, w_gate_up, b_gate_up, w_down, b_down, attn_w_qkv, attn_w_o, attn_q_gain, attn_k_gain, attn_sinks, ssm_a_re, ssm_a_im, ssm_log_dt, ssm_b_re, ssm_b_im, ssm_c_re, ssm_c_im, ssm_d, ssm_w_glu, ssm_b_glu):
    b, seq, d = x.shape
    ctx_len = ctx.shape[1]
    depth = w_mod.shape[0]
    p = ctx_len + seq
    t = b * p
    n_exp = w_router.shape[-1]
    n_q = d // HEAD_DIM
    n_kv = (attn_w_qkv.shape[-1] // HEAD_DIM - n_q) // 2
    assert ctx_len % ATT_BLOCK == 0 and seq % ATT_BLOCK == 0 and seq % GRID_W == 0
    assert d % (2 * LANES) == 0 and ctx_len % SSM_CHUNK == 0 and seq % SSM_CHUNK == 0

    tm = _pick(math.gcd(p, ctx_len), ROW_TILE)
    mm_tm = _pick(math.gcd(p, ctx_len), MM_TILE_M)
    tiles_pb = p // mm_tm
    ctx_tiles = ctx_len // mm_tm

    pad = (-(b + 1)) % 8
    cvec = jnp.concatenate([c, c_ctx[None, :], jnp.zeros((pad, d), F32)], axis=0)
    mod_all = _mod_all(cvec, w_mod, b_mod)

    cos_t, sin_t = _rope_tables(seq, ctx_len)
    xs = jnp.concatenate([ctx, x], axis=1)

    tm_moe = min(MOE_TILE, _pick(t * TOP_K, MOE_TILE, 8))
    n_tiles = (t * TOP_K) // tm_moe + n_exp
    scale = HEAD_DIM ** -0.5
    x_sorted = jnp.zeros((n_tiles * tm_moe, d // 2), jnp.uint32)

    for i in range(depth):
        j = i // 2
        lat = mod_all[i, :b]
        cx = jnp.broadcast_to(mod_all[i, b][None], lat.shape)
        modt = jnp.stack([cx, lat], axis=1).reshape(b * 2, 1, 6 * d)

        x2d = xs.reshape(t, d)
        if i % 2 == 0:
            gain = jnp.concatenate([jnp.tile(attn_q_gain[j] * scale, n_q), jnp.tile(attn_k_gain[j], n_kv),
                                    jnp.ones((n_kv * HEAD_DIM,), F32)]).reshape(1, -1)
            qkv = _qkv_proj(x2d, norm_mix[i], modt, attn_w_qkv[j].astype(BF16), gain, cos_t, sin_t,
                            n_q + n_kv, mm_tm, ctx_tiles, tiles_pb)
            o = _attention(qkv.reshape(b, p, -1), attn_sinks[j].astype(F32), n_q, n_kv, ctx_len)
            x2d = _proj_residual(o.reshape(t, d), attn_w_o[j].astype(BF16), x2d, modt, 2,
                                 mm_tm, ctx_tiles, tiles_pb)
        else:
            u = _norm_mod(xs, norm_mix[i], modt, 0, ctx_len)
            tables = _s5_tables(ssm_a_re[j], ssm_a_im[j], ssm_log_dt[j], ssm_b_re[j], ssm_b_im[j],
                                ssm_c_re[j], ssm_c_im[j], ssm_d[j])
            gy = _s5_mix(u, tables, ctx_len)
            x2d = _glu_residual(gy.reshape(t, d), ssm_w_glu[j].astype(BF16), ssm_b_glu[j], x2d, modt, 2,
                                mm_tm, ctx_tiles, tiles_pb)

        hp, idx, gw, rank, counts = _norm_router(x2d.reshape(b, p, d), norm_ffn[i], modt, w_router[i],
                                                 b_router[i], ctx_len)
        tile_e, n_active, pos = _moe_schedule(idx.reshape(t, LANES), rank.reshape(t, LANES),
                                              counts.reshape(n_exp).astype(jnp.int32), tm_moe, n_tiles)
        x_sorted = _moe_scatter(hp.reshape(t, d // 2), pos, x_sorted, tm)
        wp = _moe_prep(w_gate_up, i)
        bp = _split_blocks(b_gate_up[i]).reshape(n_exp, wp.shape[1], 1, wp.shape[3])
        y_sorted = _moe_experts(x_sorted, tile_e, n_active, wp, bp,
                                _moe_cast(w_down, i), b_down[i].reshape(n_exp, 1, d), tm_moe)
        x2d = _moe_combine(y_sorted, pos, gw.reshape(t, LANES), x2d, modt, tm, ctx_len // tm, p // tm)
        xs = x2d.reshape(b, p, d)

    return xs[:, ctx_len:]
```
